```python
import math
import jax
import jax.numpy as jnp
from jax import lax
import numpy as np

D_MODEL = 1024
BATCH = 4
SEQ = 4096
DEPTH = 1

GRID_W = 64
CTX_LEN = 256
EPS = 1e-6
POS_BASE = 10000.0
N_MOD = 6

GLA_HEADS = 4
GLA_DK = 64
GLA_DV = 128
GLA_RANK = 16
GLA_GATE_NORM = 16.0
GLA_CHUNK = 64
GLA_QK_W = GLA_HEADS * GLA_DK
GLA_V_W = GLA_HEADS * GLA_DV

HY_WIDTH = D_MODEL - GLA_V_W
HY_ORDER = 2
HY_BANDS = 16
HY_EMB = 2 * HY_BANDS + 1
HY_HIDDEN = 64
HY_SHORT = 3
HY_MIN_DECAY = math.log(1e-2) / 1.5
HY_MAX_DECAY = math.log(1e-2) / 0.3
HY_FILTER_SCALE = 0.02

PROJ_SPLITS = (GLA_QK_W, 2 * GLA_QK_W, 2 * GLA_QK_W + GLA_V_W, 2 * GLA_QK_W + 2 * GLA_V_W,
               2 * GLA_QK_W + 2 * GLA_V_W + GLA_RANK, 2 * GLA_QK_W + 2 * GLA_V_W + 2 * GLA_RANK)
D_IN = 2 * GLA_QK_W + 2 * GLA_V_W + 2 * GLA_RANK + (HY_ORDER + 1) * HY_WIDTH

N_EXPERTS = 16
EC_FACTOR = 2
D_EXPERT = 1024

kernel_name = 'hybrid_gla_hyena_ec_prefix_dit'


def _rmsnorm(x, g):
    xf = x.astype(jnp.float32)
    y = xf * lax.rsqrt(jnp.mean(xf * xf, axis=-1, keepdims=True) + EPS)
    return (y * g.astype(jnp.float32)).astype(x.dtype)


def _modulate(h, shift, scale):
    return h * (1.0 + scale) + shift


def _rev(a):
    return a[:, ::-1]


def _pos_embed_2d(rows):
    r = jnp.repeat(jnp.arange(rows, dtype=jnp.float32), GRID_W)
    col = jnp.tile(jnp.arange(GRID_W, dtype=jnp.float32), rows)
    quarter = D_MODEL // 4
    omega = 1.0 / (POS_BASE ** (jnp.arange(quarter, dtype=jnp.float32) / quarter))
    er = r[:, None] * omega
    ec = col[:, None] * omega
    return jnp.concatenate([jnp.sin(er), jnp.cos(er), jnp.sin(ec), jnp.cos(ec)], axis=-1)


def _chunk(a):
    B, L, H, d = a.shape
    return a.reshape(B, L // GLA_CHUNK, GLA_CHUNK, H, d).transpose(0, 3, 1, 2, 4).astype(jnp.float32)


def _unchunk(o):
    B, H, N, C, d = o.shape
    return o.transpose(0, 2, 3, 1, 4).reshape(B, N * C, H, d)


def _gla_direction(q, k, v, la, s0, with_output):
    kc, vc, lac = _chunk(k), _chunk(v), _chunk(la)
    b = jnp.cumsum(lac, axis=3)
    b_last = b[:, :, :, -1]
    upd = jnp.einsum('bhncd,bhnce->bhnde', kc * jnp.exp(b_last[:, :, :, None] - b), vc)

    def step(s, inp):
        dec, u = inp
        return dec[..., None] * s + u, s

    s_fin, s_prev = lax.scan(step, s0, (jnp.moveaxis(jnp.exp(b_last), 2, 0), jnp.moveaxis(upd, 2, 0)))
    if not with_output:
        return None, s_fin
    s_prev = jnp.moveaxis(s_prev, 0, 2)
    qc = _chunk(q)
    ref = b[:, :, :, GLA_CHUNK // 2:GLA_CHUNK // 2 + 1]
    scores = jnp.einsum('bhncd,bhnsd->bhncs', qc * jnp.exp(b - ref), kc * jnp.exp(ref - b))
    mask = jnp.tril(jnp.ones((GLA_CHUNK, GLA_CHUNK), dtype=bool))
    scores = jnp.where(mask, scores, 0.0)
    o = (jnp.einsum('bhncs,bhnse->bhnce', scores, vc)
         + jnp.einsum('bhncd,bhnde->bhnce', qc * jnp.exp(b), s_prev))
    return _unchunk(o), s_fin


def _gla_bidir(q, k, v, la_f, la_b, s_f, s_b, with_output):
    o_f, sf = _gla_direction(q, k, v, la_f, s_f, with_output)
    o_b, sb = _gla_direction(_rev(q), _rev(k), _rev(v), _rev(la_b), s_b, with_output)
    o = o_f + _rev(o_b) if with_output else None
    return o, sf, sb


def _gla_out(o, g, norm_g):
    B, L = o.shape[:2]
    return _rmsnorm(o, norm_g).reshape(B, L, GLA_V_W) * jax.nn.silu(g)


def _short_conv(u, w, b):
    L = u.shape[1]
    pad = HY_SHORT // 2
    up = jnp.pad(u, ((0, 0), (pad, pad), (0, 0)))
    return sum(up[:, j:j + L] * w[j] for j in range(HY_SHORT)) + b


def _hyena_filters(L, w1, b1, w2, b2, w3, freq):
    t = jnp.linspace(0.0, 1.0, L, dtype=jnp.float32)[:, None]
    w = 2.0 * math.pi * jnp.arange(L, dtype=jnp.float32)[:, None] / L
    f = jnp.linspace(1e-4, HY_BANDS - 1, HY_BANDS, dtype=jnp.float32)[None, :]
    z = jnp.concatenate([t, jnp.cos(f * w), -jnp.sin(f * w)], axis=-1)
    hdn = jnp.sin(freq * (z @ w1 + b1))
    hdn = jnp.sin(freq * (hdn @ w2 + b2))
    h = (hdn @ w3).astype(jnp.float32).reshape(L, HY_ORDER, 2, HY_WIDTH)
    deltas = jnp.abs(jnp.linspace(HY_MIN_DECAY, HY_MAX_DECAY, HY_WIDTH, dtype=jnp.float32))
    h = h * jnp.exp(-t * deltas)[:, None, None, :]
    return jnp.transpose(h, (1, 2, 0, 3))


def _bidir_fftconv(u, hf, hb, bias):
    L = u.shape[1]
    uf = u.astype(jnp.float32)
    kern = jnp.concatenate([hf, jnp.zeros_like(hf[:1]), hb[:0:-1]], axis=0)
    y = jnp.fft.irfft(jnp.fft.rfft(uf, n=2 * L, axis=1) * jnp.fft.rfft(kern, axis=0), n=2 * L, axis=1)[:, :L]
    return (y + uf * bias.astype(jnp.float32)).astype(u.dtype)


def _hyena(hp, conv_w, conv_b, filt, bias):
    u = _short_conv(hp, conv_w, conv_b)
    v, x1, x2 = jnp.split(u, 3, axis=-1)
    z = x1 * _bidir_fftconv(v, filt[0, 0], filt[0, 1], bias[0])
    return x2 * _bidir_fftconv(z, filt[1, 0], filt[1, 1], bias[1])


def _ec_moe(h, w_router, w_gate, w_up, w_down):
    B, L, _ = h.shape
    cap = EC_FACTOR * L // N_EXPERTS
    aff = jax.nn.softmax((h @ w_router).astype(jnp.float32), axis=-1)
    vals, idx = lax.top_k(jnp.swapaxes(aff, 1, 2), cap)
    bidx = jnp.arange(B)[:, None, None]
    xs = h[bidx, idx]
    hid = jax.nn.silu(jnp.einsum('becd,edf->becf', xs, w_gate)) * jnp.einsum('becd,edf->becf', xs, w_up)
    y = jnp.einsum('becf,efd->becd', hid, w_down) * vals[..., None].astype(h.dtype)
    return jnp.zeros_like(h).at[bidx, idx].add(y.astype(h.dtype))


def setup_inputs(seed: int = 0) -> dict:
    key = jax.random.key(seed)
    ks = jax.random.split(key, 29)
    f32 = jnp.float32

    def nrm(i, shape, scale):
        return jax.random.normal(ks[i], shape, f32) * scale

    def gain(i, shape):
        return 1.0 + nrm(i, shape, 0.02)

    Dp = DEPTH
    return {
        'x': nrm(0, (BATCH, SEQ, D_MODEL), 1.0),
        'c': nrm(1, (BATCH, D_MODEL), 1.0),
        'ctx': nrm(2, (BATCH, CTX_LEN, D_MODEL), 1.0),
        'c_ctx': nrm(3, (D_MODEL,), 1.0),
        'w_ada': nrm(4, (Dp, D_MODEL, N_MOD * D_MODEL), 0.5 * D_MODEL ** -0.5),
        'b_ada': nrm(5, (Dp, N_MOD * D_MODEL), 0.02),
        'norm_mix_g': gain(6, (Dp, D_MODEL)),
        'w_in': nrm(7, (Dp, D_MODEL, D_IN), D_MODEL ** -0.5),
        'gla_wa_f': nrm(8, (Dp, GLA_RANK, GLA_QK_W), GLA_RANK ** -0.5),
        'gla_ba_f': nrm(9, (Dp, GLA_QK_W), 0.1),
        'gla_wa_b': nrm(10, (Dp, GLA_RANK, GLA_QK_W), GLA_RANK ** -0.5),
        'gla_ba_b': nrm(11, (Dp, GLA_QK_W), 0.1),
        'gla_norm_g': gain(12, (Dp, GLA_DV)),
        'hy_conv_w': nrm(13, (Dp, HY_SHORT, (HY_ORDER + 1) * HY_WIDTH), HY_SHORT ** -0.5),
        'hy_conv_b': nrm(14, (Dp, (HY_ORDER + 1) * HY_WIDTH), 0.02),
        'hy_w1': nrm(15, (Dp, HY_EMB, HY_HIDDEN), HY_EMB ** -0.5),
        'hy_b1': nrm(16, (Dp, HY_HIDDEN), 0.02),
        'hy_w2': nrm(17, (Dp, HY_HIDDEN, HY_HIDDEN), HY_HIDDEN ** -0.5),
        'hy_b2': nrm(18, (Dp, HY_HIDDEN), 0.02),
        'hy_w3': nrm(19, (Dp, HY_HIDDEN, HY_ORDER * 2 * HY_WIDTH), HY_FILTER_SCALE),
        'hy_freq': gain(20, (Dp, HY_HIDDEN)),
        'hy_bias': nrm(21, (Dp, HY_ORDER, HY_WIDTH), 0.5),
        'w_out': nrm(22, (Dp, D_MODEL, D_MODEL), D_MODEL ** -0.5),
        'norm_ffn_g': gain(23, (Dp, D_MODEL)),
        'w_router': nrm(24, (Dp, D_MODEL, N_EXPERTS), D_MODEL ** -0.5),
        'w_gate': nrm(25, (Dp, N_EXPERTS, D_MODEL, D_EXPERT), D_MODEL ** -0.5),
        'w_up': nrm(26, (Dp, N_EXPERTS, D_MODEL, D_EXPERT), D_MODEL ** -0.5),
        'w_down': nrm(27, (Dp, N_EXPERTS, D_EXPERT, D_MODEL), D_EXPERT ** -0.5),
        'norm_final_g': gain(28, (D_MODEL,)),
    }


def reference(x, c, ctx, c_ctx, w_ada, b_ada, norm_mix_g, w_in, gla_wa_f, gla_ba_f, gla_wa_b, gla_ba_b,
              gla_norm_g, hy_conv_w, hy_conv_b, hy_w1, hy_b1, hy_w2, hy_b2, hy_w3, hy_freq, hy_bias,
              w_out, norm_ffn_g, w_router, w_gate, w_up, w_down, norm_final_g):
    B, n_lat, _ = x.shape
    ROWS = n_lat // GRID_W
    x = x + _pos_embed_2d(ROWS).astype(x.dtype)
    s_zero = jnp.zeros((B, GLA_HEADS, GLA_DK, GLA_DV), jnp.float32)
    for l in range(DEPTH):
        last = l == DEPTH - 1
        mod_x = jnp.split((jax.nn.silu(c) @ w_ada[l] + b_ada[l])[:, None, :], N_MOD, axis=-1)
        mod_c = jnp.split((jax.nn.silu(c_ctx) @ w_ada[l] + b_ada[l])[None, None, :], N_MOD, axis=-1)

        def project(h, mod):
            p = _modulate(_rmsnorm(h, norm_mix_g[l]), mod[0], mod[1]) @ w_in[l]
            q, k, v, g, a_f, a_b, hy = jnp.split(p, PROJ_SPLITS, axis=-1)
            Bh, L, _ = p.shape
            shp = (Bh, L, GLA_HEADS, -1)
            la_f = jax.nn.log_sigmoid((a_f @ gla_wa_f[l] + gla_ba_f[l]).astype(jnp.float32)) / GLA_GATE_NORM
            la_b = jax.nn.log_sigmoid((a_b @ gla_wa_b[l] + gla_ba_b[l]).astype(jnp.float32)) / GLA_GATE_NORM
            gla_in = (q.reshape(shp) * (GLA_DK ** -0.5), k.reshape(shp), v.reshape(shp),
                      la_f.reshape(shp), la_b.reshape(shp))
            return gla_in, g, hy

        def merge(o, g, hy):
            filt = _hyena_filters(hy.shape[1], hy_w1[l], hy_b1[l], hy_w2[l], hy_b2[l], hy_w3[l], hy_freq[l])
            y_gla = _gla_out(o, g, gla_norm_g[l])
            y_hy = _hyena(hy, hy_conv_w[l], hy_conv_b[l], filt, hy_bias[l])
            return jnp.concatenate([y_gla, y_hy], axis=-1) @ w_out[l]

        def channel_mix(h, mod):
            hn = _modulate(_rmsnorm(h, norm_ffn_g[l]), mod[3], mod[4])
            return mod[5] * _ec_moe(hn, w_router[l], w_gate[l], w_up[l], w_down[l])

        gla_c, g_c, hy_c = project(ctx, mod_c)
        gla_x, g_x, hy_x = project(x, mod_x)
        o_c, s_f, s_b = _gla_bidir(*gla_c, s_zero, s_zero, not last)
        o_x, _, _ = _gla_bidir(*gla_x, s_f, s_b, True)
        x = x + mod_x[2] * merge(o_x, g_x, hy_x)
        x = x + channel_mix(x, mod_x)
        if not last:
            ctx = ctx + mod_c[2] * merge(o_c, g_c, hy_c)
            ctx = ctx + channel_mix(ctx, mod_c)
    return _rmsnorm(x, norm_final_g)
```

```python
import functools
import math

import numpy as np
import jax
import jax.numpy as jnp
from jax import lax
from jax.experimental import pallas as pl
from jax.experimental.pallas import tpu as pltpu

F32 = jnp.float32
BF16 = jnp.bfloat16

D_MODEL = 1024
GRID_W = 64
EPS = 1e-6
POS_BASE = 10000.0
N_MOD = 6

GLA_HEADS = 4
GLA_DK = 64
GLA_DV = 128
GLA_RANK = 16
GLA_GATE_NORM = 16.0
GLA_CHUNK = 64
GLA_QK_W = GLA_HEADS * GLA_DK
GLA_V_W = GLA_HEADS * GLA_DV

HY_WIDTH = D_MODEL - GLA_V_W
HY_ORDER = 2
HY_BANDS = 16
HY_EMB = 2 * HY_BANDS + 1
HY_HIDDEN = 64
HY_SHORT = 3
HY_MIN_DECAY = math.log(1e-2) / 1.5
HY_MAX_DECAY = math.log(1e-2) / 0.3

N_EXPERTS = 16
EC_FACTOR = 2
D_EXPERT = 1024

VMEM_LIMIT_BIG = 56 * 1024 * 1024
VMEM_LIMIT_MID = 40 * 1024 * 1024
LANES = 128

CONV_T = 512


def _cparams(n_axes, vmem=VMEM_LIMIT_MID):
    return pltpu.CompilerParams(dimension_semantics=("arbitrary",) * n_axes, vmem_limit_bytes=vmem)


def _bdot(a, b):
    return jnp.dot(a.astype(BF16), b.astype(BF16), preferred_element_type=F32)


def _rms(x):
    return x * lax.rsqrt(jnp.mean(x * x, axis=-1, keepdims=True) + EPS)


def _mod_kernel(c_ref, w_ref, b_ref, o_ref):
    c = c_ref[...]
    s = c * jax.nn.sigmoid(c)
    o_ref[...] = _bdot(s, w_ref[...]) + b_ref[...]


def _adaln(cc, w_ada, b_ada):
    rows, d = cc.shape
    n = w_ada.shape[1]
    tn = 1536
    return pl.pallas_call(
        _mod_kernel,
        grid=(n // tn,),
        in_specs=[pl.BlockSpec((rows, d), lambda j: (0, 0)),
                  pl.BlockSpec((d, tn), lambda j: (0, j)),
                  pl.BlockSpec((1, tn), lambda j: (0, j))],
        out_specs=pl.BlockSpec((rows, tn), lambda j: (0, j)),
        out_shape=jax.ShapeDtypeStruct((rows, n), F32),
        compiler_params=_cparams(1),
        name="adaln",
    )(cc, w_ada, b_ada.reshape(1, n))


def _proj_kernel(*refs, with_pos, tm, widths, q_scale_first):
    if with_pos:
        x_ref, rt_ref, ct_ref, mod_ref, g_ref, w_ref, wa_ref, ba_ref = refs[:8]
        outs = refs[8:]
    else:
        x_ref, mod_ref, g_ref, w_ref, wa_ref, ba_ref = refs[:6]
        outs = refs[6:]
    x = x_ref[0]
    if with_pos:
        i = pl.program_id(0)
        rows_per_tile = tm // GRID_W
        r0 = pl.multiple_of(i * rows_per_tile, rows_per_tile)
        rt = rt_ref[pl.ds(r0, rows_per_tile), :]
        half = D_MODEL // 2
        x3 = x.reshape(rows_per_tile, GRID_W, D_MODEL)
        x3 = jnp.concatenate([x3[:, :, :half] + rt[:, None, :], x3[:, :, half:] + ct_ref[...][None]], axis=-1)
        x = x3.reshape(tm, D_MODEL)
        outs[0][0] = x
        outs = outs[1:]
    shift = mod_ref[0, :, 0:D_MODEL]
    scale = mod_ref[0, :, D_MODEL:2 * D_MODEL]
    hn = (_rms(x) * g_ref[...]) * (1.0 + scale) + shift
    p = _bdot(hn, w_ref[...])
    off = 0
    for n, (o_ref, w) in enumerate(zip(outs[:-2], widths)):
        blk = p[:, off:off + w]
        if q_scale_first and n == 0:
            blk = blk * (GLA_DK ** -0.5)
        o_ref[0] = blk
        off += w
    a = p[:, off:off + LANES]
    z = _bdot(a, wa_ref[...]) + ba_ref[...]
    la = jax.nn.log_sigmoid(z) / GLA_GATE_NORM
    outs[-2][0] = la[:, :GLA_QK_W]
    outs[-1][0] = la[:, GLA_QK_W:]


def _project(x, modrows, mod_row0, norm_g, w_cols, wa, ba, tabs, widths, q_scale_first, tm):
    B, L, D = x.shape
    with_pos = tabs is not None
    ncols = w_cols.shape[1]
    in_specs = [pl.BlockSpec((1, tm, D), lambda i, b: (b, i, 0))]
    args = [x]
    if with_pos:
        rt, ct = tabs
        in_specs += [pl.BlockSpec(rt.shape, lambda i, b: (0, 0)), pl.BlockSpec(ct.shape, lambda i, b: (0, 0))]
        args += [rt, ct]
    if mod_row0 is None:
        mod_map = lambda i, b: (b, 0, 0)
    else:
        mod_map = lambda i, b: (mod_row0, 0, 0)
    in_specs += [pl.BlockSpec((1, 1, modrows.shape[-1]), mod_map),
                 pl.BlockSpec((1, D), lambda i, b: (0, 0)),
                 pl.BlockSpec((D, ncols), lambda i, b: (0, 0)),
                 pl.BlockSpec(wa.shape, lambda i, b: (0, 0)),
                 pl.BlockSpec(ba.shape, lambda i, b: (0, 0))]
    args += [modrows, norm_g.reshape(1, D), w_cols, wa, ba]
    out_w = ([D] if with_pos else []) + list(widths) + [GLA_QK_W, GLA_QK_W]
    out_specs = [pl.BlockSpec((1, tm, w), lambda i, b: (b, i, 0)) for w in out_w]
    out_shape = [jax.ShapeDtypeStruct((B, L, w), F32) for w in out_w]
    return pl.pallas_call(
        functools.partial(_proj_kernel, with_pos=with_pos, tm=tm, widths=tuple(widths), q_scale_first=q_scale_first),
        grid=(L // tm, B),
        in_specs=in_specs, out_specs=out_specs, out_shape=out_shape,
        compiler_params=_cparams(2, VMEM_LIMIT_BIG),
        name="project_x" if with_pos else "project_ctx",
    )(*args)


def _gla_kernel(*refs, reverse, with_output, nb):
    if with_output:
        q_ref, k_ref, v_ref, la_ref, s0_ref, o_ref, sfin_ref, s_scr = refs
    else:
        k_ref, v_ref, la_ref, s0_ref, sfin_ref, s_scr = refs
    C = GLA_CHUNK

    @pl.when(pl.program_id(1) == 0)
    def _():
        s_scr[...] = s0_ref[0]

    r = lax.broadcasted_iota(jnp.int32, (C, C), 0)
    c = lax.broadcasted_iota(jnp.int32, (C, C), 1)
    keep = (r <= c) if reverse else (r >= c)
    tri = keep.astype(F32)
    rr = lax.broadcasted_iota(jnp.int32, (GLA_QK_W, GLA_V_W), 0) // GLA_DK
    cc = lax.broadcasted_iota(jnp.int32, (GLA_QK_W, GLA_V_W), 1) // GLA_DV
    bd = (rr == cc).astype(F32)
    lane_head = lax.broadcasted_iota(jnp.int32, (C, GLA_QK_W), 1) // GLA_DK
    ones_c = jnp.ones((C, LANES), F32)
    hi = lax.Precision.HIGHEST

    S = s_scr[...]
    order = range(nb - 1, -1, -1) if reverse else range(nb)
    for ci in order:
        sl = slice(ci * C, (ci + 1) * C)
        la = la_ref[0, sl, :]
        k = k_ref[0, sl, :]
        v = v_ref[0, sl, :]
        b = jnp.dot(tri, la, precision=hi, preferred_element_type=F32)
        last = 0 if reverse else C - 1
        b_tot = b[last:last + 1, :]
        b_tot_col = lax.dot_general(la, ones_c, (((0,), (0,)), ((), ())), precision=hi,
                                    preferred_element_type=F32)
        kdec = k * jnp.exp(b_tot - b)
        upd = lax.dot_general(kdec.astype(BF16), v.astype(BF16), (((0,), (0,)), ((), ())),
                              preferred_element_type=F32) * bd
        if with_output:
            q = q_ref[0, sl, :]
            mid = C // 2 - 1 if reverse else C // 2
            ref = b[mid:mid + 1, :]
            qs = q * jnp.exp(b - ref)
            ks = (k * jnp.exp(ref - b)).astype(BF16)
            qb = q * jnp.exp(b)
            o_inter = _bdot(qb, S)
            parts = []
            for h in range(GLA_HEADS):
                qh = jnp.where(lane_head == h, qs, 0.0).astype(BF16)
                s = lax.dot_general(qh, ks, (((1,), (1,)), ((), ())), preferred_element_type=F32)
                s = jnp.where(keep, s, 0.0)
                parts.append(_bdot(s, v[:, h * GLA_DV:(h + 1) * GLA_DV]))
            o_ref[0, sl, :] = o_inter + jnp.concatenate(parts, axis=-1)
        dec = jnp.exp(b_tot_col)
        S = S * jnp.concatenate([dec] * (GLA_V_W // LANES), axis=-1) + upd
    s_scr[...] = S
    sfin_ref[0] = S


def _gla_pass(q, k, v, la, s0, reverse, nb):
    B, L, _ = k.shape
    with_output = q is not None
    tm = nb * GLA_CHUNK
    nblk = L // tm
    if reverse:
        tmap = lambda b, i: (b, nblk - 1 - i, 0)
    else:
        tmap = lambda b, i: (b, i, 0)
    smap = lambda b, i: (b, 0, 0)
    in_specs, args = [], []
    if with_output:
        in_specs.append(pl.BlockSpec((1, tm, GLA_QK_W), tmap))
        args.append(q)
    in_specs += [pl.BlockSpec((1, tm, GLA_QK_W), tmap), pl.BlockSpec((1, tm, GLA_V_W), tmap),
                 pl.BlockSpec((1, tm, GLA_QK_W), tmap), pl.BlockSpec((1, GLA_QK_W, GLA_V_W), smap)]
    args += [k, v, la, s0]
    out_specs, out_shape = [], []
    if with_output:
        out_specs.append(pl.BlockSpec((1, tm, GLA_V_W), tmap))
        out_shape.append(jax.ShapeDtypeStruct((B, L, GLA_V_W), F32))
    out_specs.append(pl.BlockSpec((1, GLA_QK_W, GLA_V_W), smap))
    out_shape.append(jax.ShapeDtypeStruct((B, GLA_QK_W, GLA_V_W), F32))
    res = pl.pallas_call(
        functools.partial(_gla_kernel, reverse=reverse, with_output=with_output, nb=nb),
        grid=(B, nblk),
        in_specs=in_specs, out_specs=out_specs, out_shape=out_shape,
        scratch_shapes=[pltpu.VMEM((GLA_QK_W, GLA_V_W), F32)],
        compiler_params=_cparams(2),
        name=("gla_" + ("bwd" if reverse else "fwd") + ("" if with_output else "_state")),
    )(*args)
    return (res[0], res[1]) if with_output else (None, res[0])


def _filter_kernel(z_ref, w1_ref, b1_ref, w2_ref, b2_ref, w3_ref, fr_ref, dl_ref, o_ref, *, tm, L):
    hi = lax.Precision.HIGHEST
    fr = fr_ref[...]
    h1 = jnp.sin(fr * (jnp.dot(z_ref[...], w1_ref[...], precision=hi, preferred_element_type=F32) + b1_ref[...]))
    h2 = jnp.sin(fr * (jnp.dot(h1, w2_ref[...], precision=hi, preferred_element_type=F32) + b2_ref[...]))
    h = _bdot(h2, w3_ref[...])
    t = z_ref[:, 0:1]
    o_ref[...] = h * jnp.exp(-t * dl_ref[...])


def _hyena_filters(zfeat, w1p, b1, w2, b2, w3, freq, deltas4):
    L = zfeat.shape[0]
    tm = 512
    n = w3.shape[1]
    full = lambda a: pl.BlockSpec(a.shape, lambda i: (0,) * a.ndim)
    return pl.pallas_call(
        functools.partial(_filter_kernel, tm=tm, L=L),
        grid=(L // tm,),
        in_specs=[pl.BlockSpec((tm, zfeat.shape[1]), lambda i: (i, 0)), full(w1p), full(b1), full(w2), full(b2),
                  full(w3), full(freq), full(deltas4)],
        out_specs=pl.BlockSpec((tm, n), lambda i: (i, 0)),
        out_shape=jax.ShapeDtypeStruct((L, n), F32),
        compiler_params=_cparams(1),
        name="hyena_filters",
    )(zfeat, w1p, b1, w2, b2, w3, freq, deltas4)


def _dft_tables():
    T = CONV_T
    f = np.arange(T, dtype=np.float64)[:, None] + 0.5
    n = np.arange(T, dtype=np.float64)[None, :]
    th = 2.0 * np.pi * f * n / (2 * T)
    thn = 2.0 * np.pi * f * (n - T) / (2 * T)
    fwd = np.concatenate([np.cos(th), -np.sin(th)], axis=0)
    inv = np.concatenate([np.cos(th).T, -np.sin(th).T], axis=1) / T
    nz = np.ones((1, T)); nz[0, 0] = 0.0
    P = np.concatenate([np.cos(th), -np.sin(th)], axis=0)
    Pc = np.concatenate([np.cos(th), np.sin(th)], axis=0)
    N = np.concatenate([np.cos(thn), -np.sin(thn)], axis=0) * nz
    Nc = np.concatenate([np.cos(thn), np.sin(thn)], axis=0) * nz
    m_pos = np.concatenate([N, P], axis=1)
    m_zero = np.concatenate([Pc * nz, P], axis=1)
    m_neg = np.concatenate([Nc, Pc], axis=1)
    mats = np.stack([m_pos, m_zero, m_neg], axis=0)
    as_bf16 = lambda a: jnp.asarray(a, dtype=F32).astype(BF16)
    return as_bf16(fwd), as_bf16(inv), as_bf16(mats)


def _spec_kernel(m_ref, a_ref, b_ref, o_ref):
    T = CONV_T
    o_ref[0, 0] = _bdot(m_ref[0, :, :T], a_ref[...]) + _bdot(m_ref[0, :, T:], b_ref[...])


def _filter_spectra(h, mats):
    L = h.shape[0]
    T = CONV_T
    J = L // T
    W = HY_WIDTH
    ct = 256
    nct = W // ct
    nlag = 2 * J - 1

    def sel(dl):
        d = dl - (J - 1)
        return jnp.where(d > 0, 0, jnp.where(d == 0, 1, 2))

    def a_map(o, dl, c):
        d = dl - (J - 1)
        row = jnp.where(d > 0, d - 1, jnp.where(d == 0, 0, -d - 1))
        dirn = jnp.where(d > 0, 0, 1)
        return (row, (o * 2 + dirn) * nct + c)

    def b_map(o, dl, c):
        d = dl - (J - 1)
        row = jnp.where(d >= 0, d, -d)
        dirn = jnp.where(d >= 0, 0, 1)
        return (row, (o * 2 + dirn) * nct + c)

    return pl.pallas_call(
        _spec_kernel,
        grid=(HY_ORDER, nlag, nct),
        in_specs=[pl.BlockSpec((1, 2 * T, 2 * T), lambda o, dl, c: (sel(dl), 0, 0)),
                  pl.BlockSpec((T, ct), a_map),
                  pl.BlockSpec((T, ct), b_map)],
        out_specs=pl.BlockSpec((1, 1, 2 * T, ct), lambda o, dl, c: (o, dl, 0, c)),
        out_shape=jax.ShapeDtypeStruct((HY_ORDER, nlag, 2 * T, W), F32),
        compiler_params=_cparams(3),
        name="filter_spectra",
    )(mats, h, h)


def _sconv_kernel(x_ref, w_ref, b_ref, o_ref):
    x = x_ref[0]
    L = x.shape[0]
    t = lax.broadcasted_iota(jnp.int32, x.shape, 0)
    prev = jnp.where(t == 0, 0.0, pltpu.roll(x, 1, 0))
    nxt = jnp.where(t == L - 1, 0.0, pltpu.roll(x, L - 1, 0))
    o_ref[0] = prev * w_ref[0:1, :] + x * w_ref[1:2, :] + nxt * w_ref[2:3, :] + b_ref[...]


def _short_conv(hy, w, b):
    B, L, Wd = hy.shape
    ct = 256
    return pl.pallas_call(
        _sconv_kernel,
        grid=(B, Wd // ct),
        in_specs=[pl.BlockSpec((1, L, ct), lambda bb, c: (bb, 0, c)),
                  pl.BlockSpec((HY_SHORT, ct), lambda bb, c: (0, c)),
                  pl.BlockSpec((1, ct), lambda bb, c: (0, c))],
        out_specs=pl.BlockSpec((1, L, ct), lambda bb, c: (bb, 0, c)),
        out_shape=jax.ShapeDtypeStruct((B, L, Wd), F32),
        compiler_params=_cparams(2),
        name="hyena_short_conv",
    )(hy, w, b.reshape(1, Wd))


def _lconv_kernel(u_ref, gate_ref, bias_ref, fwd_ref, inv_ref, h_ref, o_ref, u_scr, y_scr, *, J, ct):
    T = CONV_T
    i = pl.program_id(2)

    @pl.when(i == 0)
    def _():
        for j in range(J):
            u_scr[j] = _bdot(fwd_ref[...], u_ref[0, j * T:(j + 1) * T, :])

    n_rt = T // 8
    n_lt = ct // LANES

    def tile_body(n, carry):
        rt = n // n_lt
        lt = n % n_lt
        r0 = pl.multiple_of(rt * 8, 8)
        c0 = pl.multiple_of(lt * LANES, LANES)
        yre = jnp.zeros((8, LANES), F32)
        yim = jnp.zeros((8, LANES), F32)
        for j in range(J):
            lag = i - j + (J - 1)
            hre = h_ref[0, lag, pl.ds(r0, 8), pl.ds(c0, LANES)]
            him = h_ref[0, lag, pl.ds(T + r0, 8), pl.ds(c0, LANES)]
            ure = u_scr[j, pl.ds(r0, 8), pl.ds(c0, LANES)]
            uim = u_scr[j, pl.ds(T + r0, 8), pl.ds(c0, LANES)]
            yre = yre + hre * ure - him * uim
            yim = yim + hre * uim + him * ure
        y_scr[pl.ds(r0, 8), pl.ds(c0, LANES)] = yre
        y_scr[pl.ds(T + r0, 8), pl.ds(c0, LANES)] = yim
        return carry

    lax.fori_loop(0, n_rt * n_lt, tile_body, 0)
    y = _bdot(inv_ref[...], y_scr[...])
    t0 = pl.multiple_of(i * T, T)
    ui = u_ref[0, pl.ds(t0, T), :]
    o_ref[0] = gate_ref[0] * (y + ui * bias_ref[...])


def _long_conv(u_arr, u_blk0, gate_arr, gate_blk0, bias_row, fwd, inv, hspec, order):
    B, L, _ = u_arr.shape
    T = CONV_T
    J = L // T
    ct = 256
    nct = HY_WIDTH // ct
    nlag = 2 * J - 1
    return pl.pallas_call(
        functools.partial(_lconv_kernel, J=J, ct=ct),
        grid=(nct, B, J),
        in_specs=[pl.BlockSpec((1, L, ct), lambda c, b, i: (b, 0, u_blk0 + c)),
                  pl.BlockSpec((1, T, ct), lambda c, b, i: (b, i, gate_blk0 + c)),
                  pl.BlockSpec((1, ct), lambda c, b, i: (0, c)),
                  pl.BlockSpec(fwd.shape, lambda c, b, i: (0, 0)),
                  pl.BlockSpec(inv.shape, lambda c, b, i: (0, 0)),
                  pl.BlockSpec((1, nlag, 2 * T, ct), lambda c, b, i: (order, 0, 0, c),
                               pipeline_mode=pl.Buffered(1))],
        out_specs=pl.BlockSpec((1, T, ct), lambda c, b, i: (b, i, c)),
        out_shape=jax.ShapeDtypeStruct((B, L, HY_WIDTH), F32),
        scratch_shapes=[pltpu.VMEM((J, 2 * T, ct), F32), pltpu.VMEM((2 * T, ct), F32)],
        compiler_params=_cparams(3, VMEM_LIMIT_BIG),
        name="hyena_long_conv%d" % order,
    )(u_arr, gate_arr, bias_row, fwd, inv, hspec)


def _merge_kernel(of_ref, ob_ref, g_ref, hy_ref, x0_ref, mod_ref, gn_ref, wout_ref, fg_ref, wr_ref,
                  x1_ref, lg_ref):
    o = of_ref[0] + ob_ref[0]
    g = g_ref[0]
    parts = []
    for h in range(GLA_HEADS):
        oh = o[:, h * GLA_DV:(h + 1) * GLA_DV]
        parts.append(_rms(oh) * gn_ref[...])
    y_gla = jnp.concatenate(parts, axis=-1) * (g * jax.nn.sigmoid(g))
    ycat = jnp.concatenate([y_gla, hy_ref[0]], axis=-1)
    m = _bdot(ycat, wout_ref[...])
    gate = mod_ref[0, :, 2 * D_MODEL:3 * D_MODEL]
    x1 = x0_ref[0] + gate * m
    x1_ref[0] = x1
    shift = mod_ref[0, :, 3 * D_MODEL:4 * D_MODEL]
    scale = mod_ref[0, :, 4 * D_MODEL:5 * D_MODEL]
    hn = ((_rms(x1) * fg_ref[...]) * (1.0 + scale) + shift).astype(BF16)
    lg_ref[0] = lax.dot_general(wr_ref[...], hn, (((1,), (1,)), ((), ())), preferred_element_type=F32)


def _merge(o_f, o_b, g, y_hy, x0, modrows, gla_norm_g, w_out, norm_ffn_g, w_router_t):
    B, L, D = x0.shape
    tm = 256
    tmap = lambda b, i: (b, i, 0)
    full2 = lambda a: pl.BlockSpec(a.shape, lambda b, i: (0, 0))
    return pl.pallas_call(
        _merge_kernel,
        grid=(B, L // tm),
        in_specs=[pl.BlockSpec((1, tm, GLA_V_W), tmap), pl.BlockSpec((1, tm, GLA_V_W), tmap),
                  pl.BlockSpec((1, tm, GLA_V_W), tmap), pl.BlockSpec((1, tm, HY_WIDTH), tmap),
                  pl.BlockSpec((1, tm, D), tmap),
                  pl.BlockSpec((1, 1, modrows.shape[-1]), lambda b, i: (b, 0, 0)),
                  full2(gla_norm_g), full2(w_out), full2(norm_ffn_g), full2(w_router_t)],
        out_specs=[pl.BlockSpec((1, tm, D), tmap), pl.BlockSpec((1, N_EXPERTS, tm), lambda b, i: (b, 0, i))],
        out_shape=[jax.ShapeDtypeStruct((B, L, D), F32), jax.ShapeDtypeStruct((B, N_EXPERTS, L), F32)],
        compiler_params=_cparams(2),
        name="merge_router",
    )(o_f, o_b, g, y_hy, x0, modrows, gla_norm_g, w_out, norm_ffn_g, w_router_t)


def _topk_kernel(lg_ref, idx_ref, wt_ref, pos_scr, *, cap):
    E, L = lg_ref.shape[1], lg_ref.shape[2]
    lg = lg_ref[0]
    mx = jnp.max(lg, axis=0, keepdims=True)
    ex = jnp.exp(lg - mx)
    aff = ex / jnp.sum(ex, axis=0, keepdims=True)

    def count(mask):
        return jnp.sum(mask.astype(jnp.int32), axis=1, keepdims=True)

    def tbody(n, cur):
        cand = cur | lax.shift_left(jnp.int32(1), 30 - n)
        return jnp.where(count(aff >= lax.bitcast_convert_type(cand, F32)) >= cap, cand, cur)

    thr = lax.bitcast_convert_type(lax.fori_loop(0, 31, tbody, jnp.zeros((E, 1), jnp.int32)), F32)
    gt = aff > thr
    eq = aff == thr
    need = cap - count(gt)
    tok = lax.broadcasted_iota(jnp.int32, (E, L), 1)

    def ibody(n, cur):
        cand = cur + lax.shift_left(jnp.int32(1), 12 - n)
        ok = (cand <= L) & (count(eq & (tok < cand)) <= need)
        return jnp.where(ok, cand, cur)

    bound = lax.fori_loop(0, 13, ibody, jnp.zeros((E, 1), jnp.int32))
    sel = gt | (eq & (tok < bound))

    li = lax.broadcasted_iota(jnp.int32, (LANES, LANES), 0)
    lj = lax.broadcasted_iota(jnp.int32, (LANES, LANES), 1)
    ustrict = (li < lj).astype(BF16)
    off = jnp.zeros((E, 1), F32)
    for r in range(L // LANES):
        s_r = sel[:, r * LANES:(r + 1) * LANES]
        sf = s_r.astype(F32)
        pre = jnp.dot(sf.astype(BF16), ustrict, preferred_element_type=F32)
        pos_scr[:, r * LANES:(r + 1) * LANES] = jnp.where(s_r, pre + off, -1.0)
        off = off + jnp.sum(sf, axis=1, keepdims=True)

    slot = lax.broadcasted_iota(jnp.int32, (cap, L), 0).astype(F32)
    tokf = lax.broadcasted_iota(jnp.int32, (8, L), 1)
    row = lax.broadcasted_iota(jnp.int32, (8, L), 0)
    t_hi = (tokf // 64).astype(F32)
    t_lo = (tokf % 64).astype(F32)
    for e in range(E):
        a = aff[e:e + 1, :]
        a1 = a.astype(BF16).astype(F32)
        a2 = (a - a1).astype(BF16).astype(F32)
        a3 = a - a1 - a2
        vals = jnp.where(row == 0, t_hi, jnp.where(row == 1, t_lo, jnp.where(row == 2, a1, jnp.where(
            row == 3, a2, jnp.where(row == 4, a3, 0.0)))))
        onehot = jnp.where(pos_scr[e:e + 1, :] == slot, 1.0, 0.0).astype(BF16)
        res = lax.dot_general(vals.astype(BF16), onehot, (((1,), (1,)), ((), ())), preferred_element_type=F32)
        idx_ref[0, e:e + 1, :] = (res[0:1, :] * 64.0 + res[1:2, :]).astype(jnp.int32)
        wt_ref[0, e:e + 1, :] = res[2:3, :] + res[3:4, :] + res[4:5, :]


def _topk(logits_t, cap):
    B, E, L = logits_t.shape
    return pl.pallas_call(
        functools.partial(_topk_kernel, cap=cap),
        grid=(B,),
        in_specs=[pl.BlockSpec((1, E, L), lambda b: (b, 0, 0))],
        out_specs=[pl.BlockSpec((1, E, cap), lambda b: (b, 0, 0)), pl.BlockSpec((1, E, cap), lambda b: (b, 0, 0))],
        out_shape=[jax.ShapeDtypeStruct((B, E, cap), jnp.int32), jax.ShapeDtypeStruct((B, E, cap), F32)],
        scratch_shapes=[pltpu.VMEM((E, L), F32)],
        compiler_params=_cparams(1),
        name="ec_topk",
    )(logits_t)


def _moe_kernel(idx_ref, wt_ref, x1_hbm, mod_ref, ng_ref, wg_ref, wu_ref, wd_ref, fg_ref, out_hbm,
                acc, hn_tab, xs_scr, y_scr, sem, *, cap, n_exp):
    b = pl.program_id(0)
    e = pl.program_id(1)
    L = acc.shape[0]
    rows = 512

    @pl.when(e == 0)
    def _():
        cp = pltpu.make_async_copy(x1_hbm.at[b], acc, sem.at[0])
        cp.start()
        cp.wait()
        shift = mod_ref[0, :, 3 * D_MODEL:4 * D_MODEL]
        scale = mod_ref[0, :, 4 * D_MODEL:5 * D_MODEL]

        def norm(n, carry):
            r0 = pl.multiple_of(n * rows, rows)
            hn_tab[pl.ds(r0, rows), :] = (_rms(acc[pl.ds(r0, rows), :]) * ng_ref[...]) * (1.0 + scale) + shift
            return carry

        lax.fori_loop(0, L // rows, norm, 0)

    def gather(i, carry):
        t = idx_ref[0, 0, i]
        xs_scr[pl.ds(i, 1), :] = hn_tab[pl.ds(t, 1), :]
        return carry

    lax.fori_loop(0, cap, gather, 0, unroll=8)
    xs = xs_scr[...].astype(BF16)
    fc = 512
    y = jnp.zeros((cap, D_MODEL), F32)
    for f0 in range(0, D_EXPERT, fc):
        gte = jnp.dot(xs, wg_ref[0, :, f0:f0 + fc], preferred_element_type=F32)
        up = jnp.dot(xs, wu_ref[0, :, f0:f0 + fc], preferred_element_type=F32)
        hid = (gte * jax.nn.sigmoid(gte) * up).astype(BF16)
        y = y + jnp.dot(hid, wd_ref[0, f0:f0 + fc, :], preferred_element_type=F32)
    y_scr[...] = y * mod_ref[0, :, 5 * D_MODEL:6 * D_MODEL]

    def scatter(i, carry):
        t = idx_ref[0, 0, i]
        w = wt_ref[0, 0, i]
        acc[pl.ds(t, 1), :] = acc[pl.ds(t, 1), :] + y_scr[pl.ds(i, 1), :] * w
        return carry

    lax.fori_loop(0, cap, scatter, 0, unroll=8)

    @pl.when(e == n_exp - 1)
    def _():
        def fin(n, carry):
            r0 = pl.multiple_of(n * rows, rows)
            xr = acc[pl.ds(r0, rows), :]
            acc[pl.ds(r0, rows), :] = _rms(xr) * fg_ref[...]
            return carry

        lax.fori_loop(0, L // rows, fin, 0)
        cp = pltpu.make_async_copy(acc, out_hbm.at[b], sem.at[1])
        cp.start()
        cp.wait()


def _moe(idx, wts, x1, modrows, norm_ffn_g, wg, wu, wd, norm_final_g):
    B, L, D = x1.shape
    E, cap = idx.shape[1], idx.shape[2]
    idx3 = idx.reshape(B * E, 1, cap)
    wts3 = wts.reshape(B * E, 1, cap)
    smem_spec = pl.BlockSpec((1, 1, cap), lambda b, e: (b * E + e, 0, 0), memory_space=pltpu.SMEM)
    wspec = lambda: pl.BlockSpec((1, D, D_EXPERT), lambda b, e: (e, 0, 0))
    return pl.pallas_call(
        functools.partial(_moe_kernel, cap=cap, n_exp=E),
        grid=(B, E),
        in_specs=[smem_spec, smem_spec,
                  pl.BlockSpec(memory_space=pl.ANY),
                  pl.BlockSpec((1, 1, modrows.shape[-1]), lambda b, e: (b, 0, 0)),
                  pl.BlockSpec((1, D), lambda b, e: (0, 0)),
                  wspec(), wspec(),
                  pl.BlockSpec((1, D_EXPERT, D), lambda b, e: (e, 0, 0)),
                  pl.BlockSpec((1, D), lambda b, e: (0, 0))],
        out_specs=pl.BlockSpec(memory_space=pl.ANY),
        out_shape=jax.ShapeDtypeStruct((B, L, D), F32),
        scratch_shapes=[pltpu.VMEM((L, D), F32), pltpu.VMEM((L, D), F32), pltpu.VMEM((cap, D), F32),
                        pltpu.VMEM((cap, D), F32), pltpu.SemaphoreType.DMA((2,))],
        compiler_params=_cparams(2, VMEM_LIMIT_BIG),
        name="ec_moe",
    )(idx3, wts3, x1, modrows, norm_ffn_g.reshape(1, D), wg, wu, wd, norm_final_g.reshape(1, D))


def _pos_tables():
    quarter = D_MODEL // 4
    omega = 1.0 / (POS_BASE ** (np.arange(quarter, dtype=np.float64) / quarter))
    n = np.arange(GRID_W, dtype=np.float64)[:, None] * omega[None, :]
    tab = np.concatenate([np.sin(n), np.cos(n)], axis=-1).astype(np.float32)
    return tab


def _filter_features(L):
    t = np.linspace(0.0, 1.0, L, dtype=np.float32).astype(np.float64)[:, None]
    w = 2.0 * math.pi * np.arange(L, dtype=np.float64)[:, None] / L
    f = np.linspace(1e-4, HY_BANDS - 1, HY_BANDS, dtype=np.float32).astype(np.float64)[None, :]
    z = np.concatenate([t, np.cos(f * w), -np.sin(f * w)], axis=-1)
    zp = np.zeros((L, LANES), np.float32)
    zp[:, :HY_EMB] = z.astype(np.float32)
    return zp


def kernel(x, c, ctx, c_ctx, w_ada, b_ada, norm_mix_g, w_in, gla_wa_f, gla_ba_f, gla_wa_b, gla_ba_b, gla_norm_g,
           hy_conv_w, hy_conv_b, hy_w1, hy_b1, hy_w2, hy_b2, hy_w3, hy_freq, hy_bias, w_out, norm_ffn_g, w_router,
           w_gate, w_up, w_down, norm_final_g):
    B, L, D = x.shape
    assert w_ada.shape[0] == 1 and D == D_MODEL and L % (GRID_W * 8) == 0
    l = 0
    cap = EC_FACTOR * L // N_EXPERTS

    cc = jnp.zeros((8, D), F32).at[:B].set(c).at[B].set(c_ctx)
    modrows = _adaln(cc, w_ada[l], b_ada[l]).reshape(8, 1, N_MOD * D)

    s = (GLA_QK_W, 2 * GLA_QK_W, 2 * GLA_QK_W + GLA_V_W, 2 * GLA_QK_W + 2 * GLA_V_W,
         2 * GLA_QK_W + 2 * GLA_V_W + GLA_RANK, 2 * GLA_QK_W + 2 * GLA_V_W + 2 * GLA_RANK)
    w = w_in[l]
    w_q, w_k, w_v, w_g, w_af, w_ab, w_hy = (w[:, :s[0]], w[:, s[0]:s[1]], w[:, s[1]:s[2]], w[:, s[2]:s[3]],
                                            w[:, s[3]:s[4]], w[:, s[4]:s[5]], w[:, s[5]:])
    w_a = jnp.concatenate([w_af, w_ab, jnp.zeros((D, LANES - 2 * GLA_RANK), F32)], axis=1)
    w_x = jnp.concatenate([w_q, w_k, w_v, w_g, w_hy, w_a], axis=1).astype(BF16)
    w_c = jnp.concatenate([w_k, w_v, w_a], axis=1).astype(BF16)
    wa = jnp.zeros((LANES, 2 * GLA_QK_W), F32)
    wa = wa.at[:GLA_RANK, :GLA_QK_W].set(gla_wa_f[l]).at[GLA_RANK:2 * GLA_RANK, GLA_QK_W:].set(gla_wa_b[l])
    wa = wa.astype(BF16)
    ba = jnp.concatenate([gla_ba_f[l], gla_ba_b[l]]).reshape(1, 2 * GLA_QK_W)

    tab = jnp.asarray(_pos_tables())
    x0, q, k, v, g, hy, la_f, la_b = _project(
        x, modrows, None, norm_mix_g[l], w_x, wa, ba, (tab, tab),
        (GLA_QK_W, GLA_QK_W, GLA_V_W, GLA_V_W, 3 * HY_WIDTH), True, 512)
    k_c, v_c, laf_c, lab_c = _project(ctx, modrows, B, norm_mix_g[l], w_c, wa, ba, None,
                                      (GLA_QK_W, GLA_V_W), False, ctx.shape[1])

    s_zero = jnp.zeros((B, GLA_QK_W, GLA_V_W), F32)
    nb_c = ctx.shape[1] // GLA_CHUNK
    _, s_f = _gla_pass(None, k_c, v_c, laf_c, s_zero, False, nb_c)
    _, s_b = _gla_pass(None, k_c, v_c, lab_c, s_zero, True, nb_c)
    o_f, _ = _gla_pass(q, k, v, la_f, s_f, False, 8)
    o_b, _ = _gla_pass(q, k, v, la_b, s_b, True, 8)

    w1p = jnp.zeros((LANES, HY_HIDDEN), F32).at[:HY_EMB].set(hy_w1[l])
    deltas = np.abs(np.linspace(HY_MIN_DECAY, HY_MAX_DECAY, HY_WIDTH, dtype=np.float32))
    deltas4 = jnp.asarray(np.tile(deltas, HY_ORDER * 2).reshape(1, -1))
    h = _hyena_filters(jnp.asarray(_filter_features(L)), w1p, hy_b1[l].reshape(1, -1), hy_w2[l],
                       hy_b2[l].reshape(1, -1), hy_w3[l], hy_freq[l].reshape(1, -1), deltas4)
    fwd, inv, mats = _dft_tables()
    hspec = _filter_spectra(h, mats)

    u = _short_conv(hy, hy_conv_w[l], hy_conv_b[l])
    nct = HY_WIDTH // 256
    z1 = _long_conv(u, 0, u, nct, hy_bias[l][0:1], fwd, inv, hspec, 0)
    y_hy = _long_conv(z1, 0, u, 2 * nct, hy_bias[l][1:2], fwd, inv, hspec, 1)

    x1, logits_t = _merge(o_f, o_b, g, y_hy, x0, modrows, gla_norm_g[l].reshape(1, GLA_DV),
                                   w_out[l].astype(BF16), norm_ffn_g[l].reshape(1, D),
                                   w_router[l].T.astype(BF16))
    idx, wts = _topk(logits_t, cap)
    return _moe(idx, wts, x1, modrows, norm_ffn_g[l], w_gate[l].astype(BF16), w_up[l].astype(BF16),
                w_down[l].astype(BF16), norm_final_g)
```

```python
import functools
import math

import numpy as np
import jax
import jax.numpy as jnp
from jax import lax
from jax.experimental import pallas as pl
from jax.experimental.pallas import tpu as pltpu

F32 = jnp.float32
BF16 = jnp.bfloat16

D_MODEL = 1024
GRID_W = 64
EPS = 1e-6
POS_BASE = 10000.0
N_MOD = 6

GLA_HEADS = 4
GLA_DK = 64
GLA_DV = 128
GLA_RANK = 16
GLA_GATE_NORM = 16.0
GLA_CHUNK = 64
GLA_QK_W = GLA_HEADS * GLA_DK
GLA_V_W = GLA_HEADS * GLA_DV

HY_WIDTH = D_MODEL - GLA_V_W
HY_ORDER = 2
HY_BANDS = 16
HY_EMB = 2 * HY_BANDS + 1
HY_HIDDEN = 64
HY_SHORT = 3
HY_MIN_DECAY = math.log(1e-2) / 1.5
HY_MAX_DECAY = math.log(1e-2) / 0.3

N_EXPERTS = 16
EC_FACTOR = 2
D_EXPERT = 1024

VMEM_LIMIT_BIG = 56 * 1024 * 1024
VMEM_LIMIT_MID = 40 * 1024 * 1024
LANES = 128

CONV_T = 512


def _cparams(n_axes, vmem=VMEM_LIMIT_MID):
    return pltpu.CompilerParams(dimension_semantics=("arbitrary",) * n_axes, vmem_limit_bytes=vmem)


def _bdot(a, b):
    return jnp.dot(a.astype(BF16), b.astype(BF16), preferred_element_type=F32)


def _rms(x):
    return x * lax.rsqrt(jnp.mean(x * x, axis=-1, keepdims=True) + EPS)


def _mod_kernel(c_ref, w_ref, b_ref, o_ref):
    c = c_ref[...]
    s = c * jax.nn.sigmoid(c)
    o_ref[...] = _bdot(s, w_ref[...]) + b_ref[...]


def _adaln(cc, w_ada, b_ada):
    rows, d = cc.shape
    n = w_ada.shape[1]
    tn = 1536
    return pl.pallas_call(
        _mod_kernel,
        grid=(n // tn,),
        in_specs=[pl.BlockSpec((rows, d), lambda j: (0, 0)),
                  pl.BlockSpec((d, tn), lambda j: (0, j)),
                  pl.BlockSpec((1, tn), lambda j: (0, j))],
        out_specs=pl.BlockSpec((rows, tn), lambda j: (0, j)),
        out_shape=jax.ShapeDtypeStruct((rows, n), F32),
        compiler_params=_cparams(1),
        name="adaln",
    )(cc, w_ada, b_ada.reshape(1, n))


def _proj_kernel(*refs, with_pos, tm, widths, q_scale_first):
    if with_pos:
        x_ref, rt_ref, ct_ref, mod_ref, g_ref, w_ref, wa_ref, ba_ref = refs[:8]
        outs = refs[8:]
    else:
        x_ref, mod_ref, g_ref, w_ref, wa_ref, ba_ref = refs[:6]
        outs = refs[6:]
    x = x_ref[0]
    if with_pos:
        i = pl.program_id(0)
        rows_per_tile = tm // GRID_W
        r0 = pl.multiple_of(i * rows_per_tile, rows_per_tile)
        rt = rt_ref[pl.ds(r0, rows_per_tile), :]
        half = D_MODEL // 2
        x3 = x.reshape(rows_per_tile, GRID_W, D_MODEL)
        x3 = jnp.concatenate([x3[:, :, :half] + rt[:, None, :], x3[:, :, half:] + ct_ref[...][None]], axis=-1)
        x = x3.reshape(tm, D_MODEL)
        outs[0][0] = x
        outs = outs[1:]
    shift = mod_ref[0, :, 0:D_MODEL]
    scale = mod_ref[0, :, D_MODEL:2 * D_MODEL]
    hn = (_rms(x) * g_ref[...]) * (1.0 + scale) + shift
    p = _bdot(hn, w_ref[...])
    off = 0
    for n, (o_ref, w) in enumerate(zip(outs[:-2], widths)):
        blk = p[:, off:off + w]
        if q_scale_first and n == 0:
            blk = blk * (GLA_DK ** -0.5)
        o_ref[0] = blk
        off += w
    a = p[:, off:off + LANES]
    z = _bdot(a, wa_ref[...]) + ba_ref[...]
    la = jax.nn.log_sigmoid(z) / GLA_GATE_NORM
    outs[-2][0] = la[:, :GLA_QK_W]
    outs[-1][0] = la[:, GLA_QK_W:]


def _project(x, modrows, mod_row0, norm_g, w_cols, wa, ba, tabs, widths, q_scale_first, tm):
    B, L, D = x.shape
    with_pos = tabs is not None
    ncols = w_cols.shape[1]
    in_specs = [pl.BlockSpec((1, tm, D), lambda i, b: (b, i, 0))]
    args = [x]
    if with_pos:
        rt, ct = tabs
        in_specs += [pl.BlockSpec(rt.shape, lambda i, b: (0, 0)), pl.BlockSpec(ct.shape, lambda i, b: (0, 0))]
        args += [rt, ct]
    if mod_row0 is None:
        mod_map = lambda i, b: (b, 0, 0)
    else:
        mod_map = lambda i, b: (mod_row0, 0, 0)
    in_specs += [pl.BlockSpec((1, 1, modrows.shape[-1]), mod_map),
                 pl.BlockSpec((1, D), lambda i, b: (0, 0)),
                 pl.BlockSpec((D, ncols), lambda i, b: (0, 0)),
                 pl.BlockSpec(wa.shape, lambda i, b: (0, 0)),
                 pl.BlockSpec(ba.shape, lambda i, b: (0, 0))]
    args += [modrows, norm_g.reshape(1, D), w_cols, wa, ba]
    out_w = ([D] if with_pos else []) + list(widths) + [GLA_QK_W, GLA_QK_W]
    out_specs = [pl.BlockSpec((1, tm, w), lambda i, b: (b, i, 0)) for w in out_w]
    out_shape = [jax.ShapeDtypeStruct((B, L, w), F32) for w in out_w]
    return pl.pallas_call(
        functools.partial(_proj_kernel, with_pos=with_pos, tm=tm, widths=tuple(widths), q_scale_first=q_scale_first),
        grid=(L // tm, B),
        in_specs=in_specs, out_specs=out_specs, out_shape=out_shape,
        compiler_params=_cparams(2, VMEM_LIMIT_BIG),
        name="project_x" if with_pos else "project_ctx",
    )(*args)


def _gla_kernel(*refs, reverse, with_output, nb):
    if with_output:
        q_ref, k_ref, v_ref, la_ref, s0_ref, o_ref, sfin_ref, s_scr = refs
    else:
        k_ref, v_ref, la_ref, s0_ref, sfin_ref, s_scr = refs
    C = GLA_CHUNK

    @pl.when(pl.program_id(1) == 0)
    def _():
        s_scr[...] = s0_ref[0]

    H = GLA_HEADS
    nt = (((1,), (1,)), ((), ()))
    tn = (((0,), (0,)), ((), ()))
    row = lax.broadcasted_iota(jnp.int32, (C, GLA_QK_W), 0)
    sc_r = lax.broadcasted_iota(jnp.int32, (C, H * C), 0)
    sc_s = lax.broadcasted_iota(jnp.int32, (C, H * C), 1) % C
    keep = (sc_r <= sc_s) if reverse else (sc_r >= sc_s)
    def head_mask(shape, rdiv, cdiv):
        return (lax.broadcasted_iota(jnp.int32, shape, 0) // rdiv) == (lax.broadcasted_iota(jnp.int32, shape, 1) // cdiv)
    m_k = head_mask((H * C, GLA_QK_W), C, GLA_DK)
    m_v = head_mask((H * C, GLA_V_W), C, GLA_DV)
    m_s = head_mask((GLA_V_W, GLA_QK_W), GLA_DV, GLA_DK)

    def prefix_sum(x):
        s = 1
        while s < C:
            if reverse:
                x = x + jnp.where(row < C - s, pltpu.roll(x, C - s, 0), 0.0)
            else:
                x = x + jnp.where(row >= s, pltpu.roll(x, s, 0), 0.0)
            s *= 2
        return x

    order = range(nb - 1, -1, -1) if reverse else range(nb)
    for ci in order:
        sl = slice(ci * C, (ci + 1) * C)
        k = k_ref[0, sl, :]
        v = v_ref[0, sl, :]
        b = prefix_sum(la_ref[0, sl, :])
        last = 0 if reverse else C - 1
        b_tot = b[last:last + 1, :]
        kdec = (k * jnp.exp(b_tot - b)).astype(BF16)
        vb = v.astype(BF16)
        upd_t = jnp.where(m_s, lax.dot_general(vb, kdec, tn, preferred_element_type=F32), 0.0)
        st = s_scr[...]
        if with_output:
            q = q_ref[0, sl, :]
            mid = C // 2 - 1 if reverse else C // 2
            ref = b[mid:mid + 1, :]
            qs = (q * jnp.exp(b - ref)).astype(BF16)
            ks = (k * jnp.exp(ref - b)).astype(BF16)
            qb = (q * jnp.exp(b)).astype(BF16)
            o_inter = lax.dot_general(qb, st.astype(BF16), nt, preferred_element_type=F32)
            ks4 = jnp.where(m_k, jnp.concatenate([ks] * H, axis=0), jnp.zeros((), BF16))
            sc = lax.dot_general(qs, ks4, nt, preferred_element_type=F32)
            sc = jnp.where(keep, sc, 0.0).astype(BF16)
            v4 = jnp.where(m_v, jnp.concatenate([vb] * H, axis=0), jnp.zeros((), BF16))
            o_ref[0, sl, :] = o_inter + jnp.dot(sc, v4, preferred_element_type=F32)
        s_scr[...] = st * jnp.exp(b_tot) + upd_t
    sfin_ref[0] = s_scr[...]


def _gla_pass(q, k, v, la, s0, reverse, nb):
    B, L, _ = k.shape
    with_output = q is not None
    tm = nb * GLA_CHUNK
    nblk = L // tm
    if reverse:
        tmap = lambda b, i: (b, nblk - 1 - i, 0)
    else:
        tmap = lambda b, i: (b, i, 0)
    smap = lambda b, i: (b, 0, 0)
    in_specs, args = [], []
    if with_output:
        in_specs.append(pl.BlockSpec((1, tm, GLA_QK_W), tmap))
        args.append(q)
    in_specs += [pl.BlockSpec((1, tm, GLA_QK_W), tmap), pl.BlockSpec((1, tm, GLA_V_W), tmap),
                 pl.BlockSpec((1, tm, GLA_QK_W), tmap), pl.BlockSpec((1, GLA_V_W, GLA_QK_W), smap)]
    args += [k, v, la, s0]
    out_specs, out_shape = [], []
    if with_output:
        out_specs.append(pl.BlockSpec((1, tm, GLA_V_W), tmap))
        out_shape.append(jax.ShapeDtypeStruct((B, L, GLA_V_W), F32))
    out_specs.append(pl.BlockSpec((1, GLA_V_W, GLA_QK_W), smap))
    out_shape.append(jax.ShapeDtypeStruct((B, GLA_V_W, GLA_QK_W), F32))
    res = pl.pallas_call(
        functools.partial(_gla_kernel, reverse=reverse, with_output=with_output, nb=nb),
        grid=(B, nblk),
        in_specs=in_specs, out_specs=out_specs, out_shape=out_shape,
        scratch_shapes=[pltpu.VMEM((GLA_V_W, GLA_QK_W), F32)],
        compiler_params=_cparams(2),
        name=("gla_" + ("bwd" if reverse else "fwd") + ("" if with_output else "_state")),
    )(*args)
    return (res[0], res[1]) if with_output else (None, res[0])


def _filter_kernel(z_ref, w1_ref, b1_ref, w2_ref, b2_ref, w3_ref, fr_ref, dl_ref, o_ref, *, tm, L):
    hi = lax.Precision.HIGHEST
    fr = fr_ref[...]
    h1 = jnp.sin(fr * (jnp.dot(z_ref[...], w1_ref[...], precision=hi, preferred_element_type=F32) + b1_ref[...]))
    h2 = jnp.sin(fr * (jnp.dot(h1, w2_ref[...], precision=hi, preferred_element_type=F32) + b2_ref[...]))
    h = _bdot(h2, w3_ref[...])
    t = z_ref[:, 0:1]
    o_ref[...] = h * jnp.exp(-t * dl_ref[...])


def _hyena_filters(zfeat, w1p, b1, w2, b2, w3, freq, deltas4):
    L = zfeat.shape[0]
    tm = 512
    n = w3.shape[1]
    full = lambda a: pl.BlockSpec(a.shape, lambda i: (0,) * a.ndim)
    return pl.pallas_call(
        functools.partial(_filter_kernel, tm=tm, L=L),
        grid=(L // tm,),
        in_specs=[pl.BlockSpec((tm, zfeat.shape[1]), lambda i: (i, 0)), full(w1p), full(b1), full(w2), full(b2),
                  full(w3), full(freq), full(deltas4)],
        out_specs=pl.BlockSpec((tm, n), lambda i: (i, 0)),
        out_shape=jax.ShapeDtypeStruct((L, n), F32),
        compiler_params=_cparams(1),
        name="hyena_filters",
    )(zfeat, w1p, b1, w2, b2, w3, freq, deltas4)


def _dft_tables():
    T = CONV_T
    f = np.arange(T, dtype=np.float64)[:, None] + 0.5
    n = np.arange(T, dtype=np.float64)[None, :]
    th = 2.0 * np.pi * f * n / (2 * T)
    thn = 2.0 * np.pi * f * (n - T) / (2 * T)
    fwd = np.concatenate([np.cos(th), -np.sin(th)], axis=0)
    inv = np.concatenate([np.cos(th).T, -np.sin(th).T], axis=1) / T
    nz = np.ones((1, T)); nz[0, 0] = 0.0
    P = np.concatenate([np.cos(th), -np.sin(th)], axis=0)
    Pc = np.concatenate([np.cos(th), np.sin(th)], axis=0)
    N = np.concatenate([np.cos(thn), -np.sin(thn)], axis=0) * nz
    Nc = np.concatenate([np.cos(thn), np.sin(thn)], axis=0) * nz
    m_pos = np.concatenate([N, P], axis=1)
    m_zero = np.concatenate([Pc * nz, P], axis=1)
    m_neg = np.concatenate([Nc, Pc], axis=1)
    mats = np.stack([m_pos, m_zero, m_neg], axis=0)
    as_bf16 = lambda a: jnp.asarray(a, dtype=F32).astype(BF16)
    return as_bf16(fwd), as_bf16(inv), as_bf16(mats)


def _spec_kernel(m_ref, a_ref, b_ref, o_ref):
    T = CONV_T
    o_ref[0, 0] = _bdot(m_ref[0, :, :T], a_ref[...]) + _bdot(m_ref[0, :, T:], b_ref[...])


def _filter_spectra(h, mats):
    L = h.shape[0]
    T = CONV_T
    J = L // T
    W = HY_WIDTH
    ct = 256
    nct = W // ct
    nlag = 2 * J - 1

    def sel(dl):
        d = dl - (J - 1)
        return jnp.where(d > 0, 0, jnp.where(d == 0, 1, 2))

    def a_map(o, dl, c):
        d = dl - (J - 1)
        row = jnp.where(d > 0, d - 1, jnp.where(d == 0, 0, -d - 1))
        dirn = jnp.where(d > 0, 0, 1)
        return (row, (o * 2 + dirn) * nct + c)

    def b_map(o, dl, c):
        d = dl - (J - 1)
        row = jnp.where(d >= 0, d, -d)
        dirn = jnp.where(d >= 0, 0, 1)
        return (row, (o * 2 + dirn) * nct + c)

    return pl.pallas_call(
        _spec_kernel,
        grid=(HY_ORDER, nlag, nct),
        in_specs=[pl.BlockSpec((1, 2 * T, 2 * T), lambda o, dl, c: (sel(dl), 0, 0)),
                  pl.BlockSpec((T, ct), a_map),
                  pl.BlockSpec((T, ct), b_map)],
        out_specs=pl.BlockSpec((1, 1, 2 * T, ct), lambda o, dl, c: (o, dl, 0, c)),
        out_shape=jax.ShapeDtypeStruct((HY_ORDER, nlag, 2 * T, W), F32),
        compiler_params=_cparams(3),
        name="filter_spectra",
    )(mats, h, h)


def _sconv_kernel(x_ref, w_ref, b_ref, o_ref):
    x = x_ref[0]
    L = x.shape[0]
    t = lax.broadcasted_iota(jnp.int32, x.shape, 0)
    prev = jnp.where(t == 0, 0.0, pltpu.roll(x, 1, 0))
    nxt = jnp.where(t == L - 1, 0.0, pltpu.roll(x, L - 1, 0))
    o_ref[0] = prev * w_ref[0:1, :] + x * w_ref[1:2, :] + nxt * w_ref[2:3, :] + b_ref[...]


def _short_conv(hy, w, b):
    B, L, Wd = hy.shape
    ct = 256
    return pl.pallas_call(
        _sconv_kernel,
        grid=(B, Wd // ct),
        in_specs=[pl.BlockSpec((1, L, ct), lambda bb, c: (bb, 0, c)),
                  pl.BlockSpec((HY_SHORT, ct), lambda bb, c: (0, c)),
                  pl.BlockSpec((1, ct), lambda bb, c: (0, c))],
        out_specs=pl.BlockSpec((1, L, ct), lambda bb, c: (bb, 0, c)),
        out_shape=jax.ShapeDtypeStruct((B, L, Wd), F32),
        compiler_params=_cparams(2),
        name="hyena_short_conv",
    )(hy, w, b.reshape(1, Wd))


def _lconv_kernel(u_ref, gate_ref, bias_ref, fwd_ref, inv_ref, h_ref, o_ref, u_scr, y_scr, *, J, ct, GI):
    T = CONV_T
    g = pl.program_id(2)

    @pl.when(g == 0)
    def _():
        for j in range(J):
            u_scr[j] = _bdot(fwd_ref[...], u_ref[0, j * T:(j + 1) * T, :])

    def run_group(i0):
        def tile_body(rt, carry):
            r0 = pl.multiple_of(rt * 8, 8)
            for lt in range(ct // LANES):
                cs = slice(lt * LANES, (lt + 1) * LANES)
                ure = [u_scr[j, pl.ds(r0, 8), cs] for j in range(J)]
                uim = [u_scr[j, pl.ds(T + r0, 8), cs] for j in range(J)]
                for ii in range(GI):
                    yre = jnp.zeros((8, LANES), F32)
                    yim = jnp.zeros((8, LANES), F32)
                    for j in range(J):
                        lag = i0 + ii - j + (J - 1)
                        hre = h_ref[0, lag, pl.ds(r0, 8), cs]
                        him = h_ref[0, lag, pl.ds(T + r0, 8), cs]
                        yre = yre + hre * ure[j] - him * uim[j]
                        yim = yim + hre * uim[j] + him * ure[j]
                    y_scr[ii, pl.ds(r0, 8), cs] = yre
                    y_scr[ii, pl.ds(T + r0, 8), cs] = yim
            return carry

        lax.fori_loop(0, T // 8, tile_body, 0)
        for ii in range(GI):
            y = _bdot(inv_ref[...], y_scr[ii])
            ui = u_ref[0, (i0 + ii) * T:(i0 + ii + 1) * T, :]
            o_ref[0, ii * T:(ii + 1) * T, :] = gate_ref[0, ii * T:(ii + 1) * T, :] * (y + ui * bias_ref[...])

    for gg in range(J // GI):
        pl.when(g == gg)(functools.partial(run_group, gg * GI))


def _long_conv(u_arr, u_blk0, gate_arr, gate_blk0, bias_row, fwd, inv, hspec, order):
    B, L, _ = u_arr.shape
    T = CONV_T
    J = L // T
    ct = 256
    nct = HY_WIDTH // ct
    nlag = 2 * J - 1
    GI = 4
    return pl.pallas_call(
        functools.partial(_lconv_kernel, J=J, ct=ct, GI=GI),
        grid=(nct, B, J // GI),
        in_specs=[pl.BlockSpec((1, L, ct), lambda c, b, i: (b, 0, u_blk0 + c)),
                  pl.BlockSpec((1, GI * T, ct), lambda c, b, i: (b, i, gate_blk0 + c)),
                  pl.BlockSpec((1, ct), lambda c, b, i: (0, c)),
                  pl.BlockSpec(fwd.shape, lambda c, b, i: (0, 0)),
                  pl.BlockSpec(inv.shape, lambda c, b, i: (0, 0)),
                  pl.BlockSpec((1, nlag, 2 * T, ct), lambda c, b, i: (order, 0, 0, c),
                               pipeline_mode=pl.Buffered(1))],
        out_specs=pl.BlockSpec((1, GI * T, ct), lambda c, b, i: (b, i, c)),
        out_shape=jax.ShapeDtypeStruct((B, L, HY_WIDTH), F32),
        scratch_shapes=[pltpu.VMEM((J, 2 * T, ct), F32), pltpu.VMEM((GI, 2 * T, ct), F32)],
        compiler_params=_cparams(3, VMEM_LIMIT_BIG),
        name="hyena_long_conv%d" % order,
    )(u_arr, gate_arr, bias_row, fwd, inv, hspec)


def _merge_kernel(of_ref, ob_ref, g_ref, hy_ref, x0_ref, mod_ref, gn_ref, wout_ref, fg_ref, wr_ref,
                  x1_ref, lg_ref):
    o = of_ref[0] + ob_ref[0]
    g = g_ref[0]
    parts = []
    for h in range(GLA_HEADS):
        oh = o[:, h * GLA_DV:(h + 1) * GLA_DV]
        parts.append(_rms(oh) * gn_ref[...])
    y_gla = jnp.concatenate(parts, axis=-1) * (g * jax.nn.sigmoid(g))
    ycat = jnp.concatenate([y_gla, hy_ref[0]], axis=-1)
    m = _bdot(ycat, wout_ref[...])
    gate = mod_ref[0, :, 2 * D_MODEL:3 * D_MODEL]
    x1 = x0_ref[0] + gate * m
    x1_ref[0] = x1
    shift = mod_ref[0, :, 3 * D_MODEL:4 * D_MODEL]
    scale = mod_ref[0, :, 4 * D_MODEL:5 * D_MODEL]
    hn = ((_rms(x1) * fg_ref[...]) * (1.0 + scale) + shift).astype(BF16)
    lg_ref[0] = lax.dot_general(wr_ref[...], hn, (((1,), (1,)), ((), ())), preferred_element_type=F32)


def _merge(o_f, o_b, g, y_hy, x0, modrows, gla_norm_g, w_out, norm_ffn_g, w_router_t):
    B, L, D = x0.shape
    tm = 256
    tmap = lambda b, i: (b, i, 0)
    full2 = lambda a: pl.BlockSpec(a.shape, lambda b, i: (0, 0))
    return pl.pallas_call(
        _merge_kernel,
        grid=(B, L // tm),
        in_specs=[pl.BlockSpec((1, tm, GLA_V_W), tmap), pl.BlockSpec((1, tm, GLA_V_W), tmap),
                  pl.BlockSpec((1, tm, GLA_V_W), tmap), pl.BlockSpec((1, tm, HY_WIDTH), tmap),
                  pl.BlockSpec((1, tm, D), tmap),
                  pl.BlockSpec((1, 1, modrows.shape[-1]), lambda b, i: (b, 0, 0)),
                  full2(gla_norm_g), full2(w_out), full2(norm_ffn_g), full2(w_router_t)],
        out_specs=[pl.BlockSpec((1, tm, D), tmap), pl.BlockSpec((1, N_EXPERTS, tm), lambda b, i: (b, 0, i))],
        out_shape=[jax.ShapeDtypeStruct((B, L, D), F32), jax.ShapeDtypeStruct((B, N_EXPERTS, L), F32)],
        compiler_params=_cparams(2),
        name="merge_router",
    )(o_f, o_b, g, y_hy, x0, modrows, gla_norm_g, w_out, norm_ffn_g, w_router_t)


def _topk_kernel(lg_ref, idx_ref, wt_ref, pos_scr, *, cap):
    E, L = lg_ref.shape[1], lg_ref.shape[2]
    lg = lg_ref[0]
    mx = jnp.max(lg, axis=0, keepdims=True)
    ex = jnp.exp(lg - mx)
    aff = ex / jnp.sum(ex, axis=0, keepdims=True)

    def count(mask):
        return jnp.sum(mask.astype(jnp.int32), axis=1, keepdims=True)

    def tbody(n, cur):
        cand = cur | lax.shift_left(jnp.int32(1), 30 - n)
        return jnp.where(count(aff >= lax.bitcast_convert_type(cand, F32)) >= cap, cand, cur)

    thr = lax.bitcast_convert_type(lax.fori_loop(0, 31, tbody, jnp.zeros((E, 1), jnp.int32)), F32)
    gt = aff > thr
    eq = aff == thr
    need = cap - count(gt)
    tok = lax.broadcasted_iota(jnp.int32, (E, L), 1)

    def ibody(n, cur):
        cand = cur + lax.shift_left(jnp.int32(1), 12 - n)
        ok = (cand <= L) & (count(eq & (tok < cand)) <= need)
        return jnp.where(ok, cand, cur)

    bound = lax.fori_loop(0, 13, ibody, jnp.zeros((E, 1), jnp.int32))
    sel = gt | (eq & (tok < bound))

    li = lax.broadcasted_iota(jnp.int32, (LANES, LANES), 0)
    lj = lax.broadcasted_iota(jnp.int32, (LANES, LANES), 1)
    ustrict = (li < lj).astype(BF16)
    off = jnp.zeros((E, 1), F32)
    for r in range(L // LANES):
        s_r = sel[:, r * LANES:(r + 1) * LANES]
        sf = s_r.astype(F32)
        pre = jnp.dot(sf.astype(BF16), ustrict, preferred_element_type=F32)
        pos_scr[:, r * LANES:(r + 1) * LANES] = jnp.where(s_r, pre + off, -1.0)
        off = off + jnp.sum(sf, axis=1, keepdims=True)

    slot = lax.broadcasted_iota(jnp.int32, (cap, L), 0).astype(F32)
    tokf = lax.broadcasted_iota(jnp.int32, (8, L), 1)
    row = lax.broadcasted_iota(jnp.int32, (8, L), 0)
    t_hi = (tokf // 64).astype(F32)
    t_lo = (tokf % 64).astype(F32)
    for e in range(E):
        a = aff[e:e + 1, :]
        a1 = a.astype(BF16).astype(F32)
        a2 = (a - a1).astype(BF16).astype(F32)
        a3 = a - a1 - a2
        vals = jnp.where(row == 0, t_hi, jnp.where(row == 1, t_lo, jnp.where(row == 2, a1, jnp.where(
            row == 3, a2, jnp.where(row == 4, a3, 0.0)))))
        onehot = jnp.where(pos_scr[e:e + 1, :] == slot, 1.0, 0.0).astype(BF16)
        res = lax.dot_general(vals.astype(BF16), onehot, (((1,), (1,)), ((), ())), preferred_element_type=F32)
        idx_ref[0, e:e + 1, :] = (res[0:1, :] * 64.0 + res[1:2, :]).astype(jnp.int32)
        wt_ref[0, e:e + 1, :] = res[2:3, :] + res[3:4, :] + res[4:5, :]


def _topk(logits_t, cap):
    B, E, L = logits_t.shape
    return pl.pallas_call(
        functools.partial(_topk_kernel, cap=cap),
        grid=(B,),
        in_specs=[pl.BlockSpec((1, E, L), lambda b: (b, 0, 0))],
        out_specs=[pl.BlockSpec((1, E, cap), lambda b: (b, 0, 0)), pl.BlockSpec((1, E, cap), lambda b: (b, 0, 0))],
        out_shape=[jax.ShapeDtypeStruct((B, E, cap), jnp.int32), jax.ShapeDtypeStruct((B, E, cap), F32)],
        scratch_shapes=[pltpu.VMEM((E, L), F32)],
        compiler_params=_cparams(1),
        name="ec_topk",
    )(logits_t)


def _moe_kernel(idx_ref, wt_ref, x1_hbm, mod_ref, ng_ref, wg_ref, wu_ref, wd_ref, fg_ref, out_hbm,
                acc, hn_tab, xs_scr, y_scr, sem, *, cap, n_exp, n_half):
    b = pl.program_id(0)
    e = pl.program_id(1)
    hf = pl.program_id(2)
    L = acc.shape[0]
    rows = 512
    group = 8

    @pl.when((e == 0) & (hf == 0))
    def _():
        cp = pltpu.make_async_copy(x1_hbm.at[b], acc, sem.at[0])
        cp.start()
        cp.wait()
        shift = mod_ref[0, :, 3 * D_MODEL:4 * D_MODEL]
        scale = mod_ref[0, :, 4 * D_MODEL:5 * D_MODEL]

        def norm(n, carry):
            r0 = pl.multiple_of(n * rows, rows)
            hn_tab[pl.ds(r0, rows), :] = (_rms(acc[pl.ds(r0, rows), :]) * ng_ref[...]) * (1.0 + scale) + shift
            return carry

        lax.fori_loop(0, L // rows, norm, 0)

    @pl.when(hf == 0)
    def _():
        def gather(gi, carry):
            i0 = pl.multiple_of(gi * group, group)
            for r in range(group):
                t = idx_ref[0, 0, i0 + r]
                y_scr[pl.ds(i0 + r, 1), :] = hn_tab[pl.ds(t, 1), :]
            return carry

        lax.fori_loop(0, cap // group, gather, 0)
        xs_scr[...] = y_scr[...].astype(BF16)

    xs = xs_scr[...]
    gte = _bdot(xs, wg_ref[0])
    up = _bdot(xs, wu_ref[0])
    hid = gte * jax.nn.sigmoid(gte) * up
    part = _bdot(hid, wd_ref[0])

    @pl.when(hf == 0)
    def _():
        y_scr[...] = part

    @pl.when(hf == n_half - 1)
    def _():
        y_scr[...] = (y_scr[...] + part) * mod_ref[0, :, 5 * D_MODEL:6 * D_MODEL]

        def scatter(gi, carry):
            i0 = pl.multiple_of(gi * group, group)
            ts = [idx_ref[0, 0, i0 + r] for r in range(group)]
            ws = [wt_ref[0, 0, i0 + r] for r in range(group)]
            new = [acc[pl.ds(ts[r], 1), :] + y_scr[pl.ds(i0 + r, 1), :] * ws[r] for r in range(group)]
            for r in range(group):
                acc[pl.ds(ts[r], 1), :] = new[r]
            return carry

        lax.fori_loop(0, cap // group, scatter, 0)

    @pl.when((e == n_exp - 1) & (hf == n_half - 1))
    def _():
        def fin(n, carry):
            r0 = pl.multiple_of(n * rows, rows)
            xr = acc[pl.ds(r0, rows), :]
            acc[pl.ds(r0, rows), :] = _rms(xr) * fg_ref[...]
            return carry

        lax.fori_loop(0, L // rows, fin, 0)
        cp = pltpu.make_async_copy(acc, out_hbm.at[b], sem.at[1])
        cp.start()
        cp.wait()


def _moe(idx, wts, x1, modrows, norm_ffn_g, wg, wu, wd, norm_final_g):
    B, L, D = x1.shape
    E, cap = idx.shape[1], idx.shape[2]
    idx3 = idx.reshape(B * E, 1, cap)
    wts3 = wts.reshape(B * E, 1, cap)
    smem_spec = pl.BlockSpec((1, 1, cap), lambda b, e, h: (b * E + e, 0, 0), memory_space=pltpu.SMEM)
    n_half = 2
    fc = D_EXPERT // n_half
    wspec = lambda: pl.BlockSpec((1, D, fc), lambda b, e, h: (e, 0, h))
    return pl.pallas_call(
        functools.partial(_moe_kernel, cap=cap, n_exp=E, n_half=n_half),
        grid=(B, E, n_half),
        in_specs=[smem_spec, smem_spec,
                  pl.BlockSpec(memory_space=pl.ANY),
                  pl.BlockSpec((1, 1, modrows.shape[-1]), lambda b, e, h: (b, 0, 0)),
                  pl.BlockSpec((1, D), lambda b, e, h: (0, 0)),
                  wspec(), wspec(),
                  pl.BlockSpec((1, fc, D), lambda b, e, h: (e, h, 0)),
                  pl.BlockSpec((1, D), lambda b, e, h: (0, 0))],
        out_specs=pl.BlockSpec(memory_space=pl.ANY),
        out_shape=jax.ShapeDtypeStruct((B, L, D), F32),
        scratch_shapes=[pltpu.VMEM((L, D), F32), pltpu.VMEM((L, D), F32), pltpu.VMEM((cap, D), BF16),
                        pltpu.VMEM((cap, D), F32), pltpu.SemaphoreType.DMA((2,))],
        compiler_params=_cparams(3, VMEM_LIMIT_BIG),
        name="ec_moe",
    )(idx3, wts3, x1, modrows, norm_ffn_g.reshape(1, D), wg, wu, wd, norm_final_g.reshape(1, D))


def _pos_tables():
    quarter = D_MODEL // 4
    omega = 1.0 / (POS_BASE ** (np.arange(quarter, dtype=np.float64) / quarter))
    n = np.arange(GRID_W, dtype=np.float64)[:, None] * omega[None, :]
    tab = np.concatenate([np.sin(n), np.cos(n)], axis=-1).astype(np.float32)
    return tab


def _filter_features(L):
    t = np.linspace(0.0, 1.0, L, dtype=np.float32).astype(np.float64)[:, None]
    w = 2.0 * math.pi * np.arange(L, dtype=np.float64)[:, None] / L
    f = np.linspace(1e-4, HY_BANDS - 1, HY_BANDS, dtype=np.float32).astype(np.float64)[None, :]
    z = np.concatenate([t, np.cos(f * w), -np.sin(f * w)], axis=-1)
    zp = np.zeros((L, LANES), np.float32)
    zp[:, :HY_EMB] = z.astype(np.float32)
    return zp


def kernel(x, c, ctx, c_ctx, w_ada, b_ada, norm_mix_g, w_in, gla_wa_f, gla_ba_f, gla_wa_b, gla_ba_b, gla_norm_g,
           hy_conv_w, hy_conv_b, hy_w1, hy_b1, hy_w2, hy_b2, hy_w3, hy_freq, hy_bias, w_out, norm_ffn_g, w_router,
           w_gate, w_up, w_down, norm_final_g):
    B, L, D = x.shape
    assert w_ada.shape[0] == 1 and D == D_MODEL and L % (GRID_W * 8) == 0
    l = 0
    cap = EC_FACTOR * L // N_EXPERTS

    cc = jnp.zeros((8, D), F32).at[:B].set(c).at[B].set(c_ctx)
    modrows = _adaln(cc, w_ada[l], b_ada[l]).reshape(8, 1, N_MOD * D)

    s = (GLA_QK_W, 2 * GLA_QK_W, 2 * GLA_QK_W + GLA_V_W, 2 * GLA_QK_W + 2 * GLA_V_W,
         2 * GLA_QK_W + 2 * GLA_V_W + GLA_RANK, 2 * GLA_QK_W + 2 * GLA_V_W + 2 * GLA_RANK)
    w = w_in[l]
    w_q, w_k, w_v, w_g, w_af, w_ab, w_hy = (w[:, :s[0]], w[:, s[0]:s[1]], w[:, s[1]:s[2]], w[:, s[2]:s[3]],
                                            w[:, s[3]:s[4]], w[:, s[4]:s[5]], w[:, s[5]:])
    w_a = jnp.concatenate([w_af, w_ab, jnp.zeros((D, LANES - 2 * GLA_RANK), F32)], axis=1)
    w_x = jnp.concatenate([w_q, w_k, w_v, w_g, w_hy, w_a], axis=1).astype(BF16)
    w_c = jnp.concatenate([w_k, w_v, w_a], axis=1).astype(BF16)
    wa = jnp.zeros((LANES, 2 * GLA_QK_W), F32)
    wa = wa.at[:GLA_RANK, :GLA_QK_W].set(gla_wa_f[l]).at[GLA_RANK:2 * GLA_RANK, GLA_QK_W:].set(gla_wa_b[l])
    wa = wa.astype(BF16)
    ba = jnp.concatenate([gla_ba_f[l], gla_ba_b[l]]).reshape(1, 2 * GLA_QK_W)

    tab = jnp.asarray(_pos_tables())
    x0, q, k, v, g, hy, la_f, la_b = _project(
        x, modrows, None, norm_mix_g[l], w_x, wa, ba, (tab, tab),
        (GLA_QK_W, GLA_QK_W, GLA_V_W, GLA_V_W, 3 * HY_WIDTH), True, 512)
    k_c, v_c, laf_c, lab_c = _project(ctx, modrows, B, norm_mix_g[l], w_c, wa, ba, None,
                                      (GLA_QK_W, GLA_V_W), False, ctx.shape[1])

    s_zero = jnp.zeros((B, GLA_V_W, GLA_QK_W), F32)
    nb_c = ctx.shape[1] // GLA_CHUNK
    _, s_f = _gla_pass(None, k_c, v_c, laf_c, s_zero, False, nb_c)
    _, s_b = _gla_pass(None, k_c, v_c, lab_c, s_zero, True, nb_c)
    o_f, _ = _gla_pass(q, k, v, la_f, s_f, False, 8)
    o_b, _ = _gla_pass(q, k, v, la_b, s_b, True, 8)

    w1p = jnp.zeros((LANES, HY_HIDDEN), F32).at[:HY_EMB].set(hy_w1[l])
    deltas = np.abs(np.linspace(HY_MIN_DECAY, HY_MAX_DECAY, HY_WIDTH, dtype=np.float32))
    deltas4 = jnp.asarray(np.tile(deltas, HY_ORDER * 2).reshape(1, -1))
    h = _hyena_filters(jnp.asarray(_filter_features(L)), w1p, hy_b1[l].reshape(1, -1), hy_w2[l],
                       hy_b2[l].reshape(1, -1), hy_w3[l], hy_freq[l].reshape(1, -1), deltas4)
    fwd, inv, mats = _dft_tables()
    hspec = _filter_spectra(h, mats)

    u = _short_conv(hy, hy_conv_w[l], hy_conv_b[l])
    nct = HY_WIDTH // 256
    z1 = _long_conv(u, 0, u, nct, hy_bias[l][0:1], fwd, inv, hspec, 0)
    y_hy = _long_conv(z1, 0, u, 2 * nct, hy_bias[l][1:2], fwd, inv, hspec, 1)

    x1, logits_t = _merge(o_f, o_b, g, y_hy, x0, modrows, gla_norm_g[l].reshape(1, GLA_DV),
                                   w_out[l].astype(BF16), norm_ffn_g[l].reshape(1, D),
                                   w_router[l].T.astype(BF16))
    idx, wts = _topk(logits_t, cap)
    return _moe(idx, wts, x1, modrows, norm_ffn_g[l], w_gate[l], w_up[l], w_down[l], norm_final_g)
```

```python
import functools
import math

import numpy as np
import jax
import jax.numpy as jnp
from jax import lax
from jax.experimental import pallas as pl
from jax.experimental.pallas import tpu as pltpu

F32 = jnp.float32
BF16 = jnp.bfloat16

D_MODEL = 1024
GRID_W = 64
EPS = 1e-6
POS_BASE = 10000.0
N_MOD = 6

GLA_HEADS = 4
GLA_DK = 64
GLA_DV = 128
GLA_RANK = 16
GLA_GATE_NORM = 16.0
GLA_CHUNK = 64
GLA_QK_W = GLA_HEADS * GLA_DK
GLA_V_W = GLA_HEADS * GLA_DV

HY_WIDTH = D_MODEL - GLA_V_W
HY_ORDER = 2
HY_BANDS = 16
HY_EMB = 2 * HY_BANDS + 1
HY_HIDDEN = 64
HY_SHORT = 3
HY_MIN_DECAY = math.log(1e-2) / 1.5
HY_MAX_DECAY = math.log(1e-2) / 0.3

N_EXPERTS = 16
EC_FACTOR = 2
D_EXPERT = 1024

VMEM_LIMIT_BIG = 56 * 1024 * 1024
VMEM_LIMIT_MID = 40 * 1024 * 1024
LANES = 128

CONV_T = 512


def _cparams(n_axes, vmem=VMEM_LIMIT_MID):
    return pltpu.CompilerParams(dimension_semantics=("arbitrary",) * n_axes, vmem_limit_bytes=vmem)


def _bdot(a, b):
    return jnp.dot(a.astype(BF16), b.astype(BF16), preferred_element_type=F32)


def _rms(x):
    return x * lax.rsqrt(jnp.mean(x * x, axis=-1, keepdims=True) + EPS)


def _mod_kernel(c_ref, w_ref, b_ref, o_ref):
    c = c_ref[...]
    s = c * jax.nn.sigmoid(c)
    o_ref[...] = _bdot(s, w_ref[...]) + b_ref[...]


def _adaln(cc, w_ada, b_ada):
    rows, d = cc.shape
    n = w_ada.shape[1]
    tn = 1536
    return pl.pallas_call(
        _mod_kernel,
        grid=(n // tn,),
        in_specs=[pl.BlockSpec((rows, d), lambda j: (0, 0)),
                  pl.BlockSpec((d, tn), lambda j: (0, j)),
                  pl.BlockSpec((1, tn), lambda j: (0, j))],
        out_specs=pl.BlockSpec((rows, tn), lambda j: (0, j)),
        out_shape=jax.ShapeDtypeStruct((rows, n), F32),
        compiler_params=_cparams(1),
        name="adaln",
    )(cc, w_ada, b_ada.reshape(1, n))


def _proj_kernel(*refs, with_pos, tm, widths, q_scale_first):
    if with_pos:
        x_ref, rt_ref, ct_ref, mod_ref, g_ref, w_ref, wa_ref, ba_ref = refs[:8]
        outs = refs[8:]
    else:
        x_ref, mod_ref, g_ref, w_ref, wa_ref, ba_ref = refs[:6]
        outs = refs[6:]
    x = x_ref[0]
    if with_pos:
        i = pl.program_id(0)
        rows_per_tile = tm // GRID_W
        r0 = pl.multiple_of(i * rows_per_tile, rows_per_tile)
        rt = rt_ref[pl.ds(r0, rows_per_tile), :]
        half = D_MODEL // 2
        x3 = x.reshape(rows_per_tile, GRID_W, D_MODEL)
        x3 = jnp.concatenate([x3[:, :, :half] + rt[:, None, :], x3[:, :, half:] + ct_ref[...][None]], axis=-1)
        x = x3.reshape(tm, D_MODEL)
        outs[0][0] = x
        outs = outs[1:]
    shift = mod_ref[0, :, 0:D_MODEL]
    scale = mod_ref[0, :, D_MODEL:2 * D_MODEL]
    hn = (_rms(x) * g_ref[...]) * (1.0 + scale) + shift
    p = _bdot(hn, w_ref[...])
    off = 0
    for n, (o_ref, w) in enumerate(zip(outs[:-2], widths)):
        blk = p[:, off:off + w]
        if q_scale_first and n == 0:
            blk = blk * (GLA_DK ** -0.5)
        o_ref[0] = blk
        off += w
    a = p[:, off:off + LANES]
    z = _bdot(a, wa_ref[...]) + ba_ref[...]
    la = jax.nn.log_sigmoid(z) / GLA_GATE_NORM
    outs[-2][0] = la[:, :GLA_QK_W]
    outs[-1][0] = la[:, GLA_QK_W:]


def _project(x, modrows, mod_row0, norm_g, w_cols, wa, ba, tabs, widths, q_scale_first, tm):
    B, L, D = x.shape
    with_pos = tabs is not None
    ncols = w_cols.shape[1]
    in_specs = [pl.BlockSpec((1, tm, D), lambda i, b: (b, i, 0))]
    args = [x]
    if with_pos:
        rt, ct = tabs
        in_specs += [pl.BlockSpec(rt.shape, lambda i, b: (0, 0)), pl.BlockSpec(ct.shape, lambda i, b: (0, 0))]
        args += [rt, ct]
    if mod_row0 is None:
        mod_map = lambda i, b: (b, 0, 0)
    else:
        mod_map = lambda i, b: (mod_row0, 0, 0)
    in_specs += [pl.BlockSpec((1, 1, modrows.shape[-1]), mod_map),
                 pl.BlockSpec((1, D), lambda i, b: (0, 0)),
                 pl.BlockSpec((D, ncols), lambda i, b: (0, 0)),
                 pl.BlockSpec(wa.shape, lambda i, b: (0, 0)),
                 pl.BlockSpec(ba.shape, lambda i, b: (0, 0))]
    args += [modrows, norm_g.reshape(1, D), w_cols, wa, ba]
    out_w = ([D] if with_pos else []) + list(widths) + [GLA_QK_W, GLA_QK_W]
    out_specs = [pl.BlockSpec((1, tm, w), lambda i, b: (b, i, 0)) for w in out_w]
    out_shape = [jax.ShapeDtypeStruct((B, L, w), F32) for w in out_w]
    return pl.pallas_call(
        functools.partial(_proj_kernel, with_pos=with_pos, tm=tm, widths=tuple(widths), q_scale_first=q_scale_first),
        grid=(L // tm, B),
        in_specs=in_specs, out_specs=out_specs, out_shape=out_shape,
        compiler_params=_cparams(2, VMEM_LIMIT_BIG),
        name="project_x" if with_pos else "project_ctx",
    )(*args)


def _gla_kernel(*refs, reverse, with_output, nb):
    if with_output:
        q_ref, k_ref, v_ref, la_ref, s0_ref, o_ref, sfin_ref, s_scr = refs
    else:
        k_ref, v_ref, la_ref, s0_ref, sfin_ref, s_scr = refs
    C = GLA_CHUNK

    @pl.when(pl.program_id(1) == 0)
    def _():
        s_scr[...] = s0_ref[0]

    H = GLA_HEADS
    nt = (((1,), (1,)), ((), ()))
    tn = (((0,), (0,)), ((), ()))
    row = lax.broadcasted_iota(jnp.int32, (C, GLA_QK_W), 0)
    sc_r = lax.broadcasted_iota(jnp.int32, (C, H * C), 0)
    sc_s = lax.broadcasted_iota(jnp.int32, (C, H * C), 1) % C
    keep = (sc_r <= sc_s) if reverse else (sc_r >= sc_s)
    def head_mask(shape, rdiv, cdiv):
        return (lax.broadcasted_iota(jnp.int32, shape, 0) // rdiv) == (lax.broadcasted_iota(jnp.int32, shape, 1) // cdiv)
    m_k = head_mask((H * C, GLA_QK_W), C, GLA_DK)
    m_v = head_mask((H * C, GLA_V_W), C, GLA_DV)
    m_s = head_mask((GLA_V_W, GLA_QK_W), GLA_DV, GLA_DK)

    def prefix_sum(x):
        s = 1
        while s < C:
            if reverse:
                x = x + jnp.where(row < C - s, pltpu.roll(x, C - s, 0), 0.0)
            else:
                x = x + jnp.where(row >= s, pltpu.roll(x, s, 0), 0.0)
            s *= 2
        return x

    order = range(nb - 1, -1, -1) if reverse else range(nb)
    for ci in order:
        sl = slice(ci * C, (ci + 1) * C)
        k = k_ref[0, sl, :]
        v = v_ref[0, sl, :]
        b = prefix_sum(la_ref[0, sl, :])
        last = 0 if reverse else C - 1
        b_tot = b[last:last + 1, :]
        kdec = (k * jnp.exp(b_tot - b)).astype(BF16)
        vb = v.astype(BF16)
        upd_t = jnp.where(m_s, lax.dot_general(vb, kdec, tn, preferred_element_type=F32), 0.0)
        st = s_scr[...]
        if with_output:
            q = q_ref[0, sl, :]
            mid = C // 2 - 1 if reverse else C // 2
            ref = b[mid:mid + 1, :]
            qs = (q * jnp.exp(b - ref)).astype(BF16)
            ks = (k * jnp.exp(ref - b)).astype(BF16)
            qb = (q * jnp.exp(b)).astype(BF16)
            o_inter = lax.dot_general(qb, st.astype(BF16), nt, preferred_element_type=F32)
            ks4 = jnp.where(m_k, jnp.concatenate([ks] * H, axis=0), jnp.zeros((), BF16))
            sc = lax.dot_general(qs, ks4, nt, preferred_element_type=F32)
            sc = jnp.where(keep, sc, 0.0).astype(BF16)
            v4 = jnp.where(m_v, jnp.concatenate([vb] * H, axis=0), jnp.zeros((), BF16))
            o_ref[0, sl, :] = o_inter + jnp.dot(sc, v4, preferred_element_type=F32)
        s_scr[...] = st * jnp.exp(b_tot) + upd_t
    sfin_ref[0] = s_scr[...]


def _gla_pass(q, k, v, la, s0, reverse, nb):
    B, L, _ = k.shape
    with_output = q is not None
    tm = nb * GLA_CHUNK
    nblk = L // tm
    if reverse:
        tmap = lambda b, i: (b, nblk - 1 - i, 0)
    else:
        tmap = lambda b, i: (b, i, 0)
    smap = lambda b, i: (b, 0, 0)
    in_specs, args = [], []
    if with_output:
        in_specs.append(pl.BlockSpec((1, tm, GLA_QK_W), tmap))
        args.append(q)
    in_specs += [pl.BlockSpec((1, tm, GLA_QK_W), tmap), pl.BlockSpec((1, tm, GLA_V_W), tmap),
                 pl.BlockSpec((1, tm, GLA_QK_W), tmap), pl.BlockSpec((1, GLA_V_W, GLA_QK_W), smap)]
    args += [k, v, la, s0]
    out_specs, out_shape = [], []
    if with_output:
        out_specs.append(pl.BlockSpec((1, tm, GLA_V_W), tmap))
        out_shape.append(jax.ShapeDtypeStruct((B, L, GLA_V_W), F32))
    out_specs.append(pl.BlockSpec((1, GLA_V_W, GLA_QK_W), smap))
    out_shape.append(jax.ShapeDtypeStruct((B, GLA_V_W, GLA_QK_W), F32))
    res = pl.pallas_call(
        functools.partial(_gla_kernel, reverse=reverse, with_output=with_output, nb=nb),
        grid=(B, nblk),
        in_specs=in_specs, out_specs=out_specs, out_shape=out_shape,
        scratch_shapes=[pltpu.VMEM((GLA_V_W, GLA_QK_W), F32)],
        compiler_params=_cparams(2),
        name=("gla_" + ("bwd" if reverse else "fwd") + ("" if with_output else "_state")),
    )(*args)
    return (res[0], res[1]) if with_output else (None, res[0])


def _filter_kernel(z_ref, w1_ref, b1_ref, w2_ref, b2_ref, w3_ref, fr_ref, dl_ref, o_ref, *, tm, L):
    hi = lax.Precision.HIGHEST
    fr = fr_ref[...]
    h1 = jnp.sin(fr * (jnp.dot(z_ref[...], w1_ref[...], precision=hi, preferred_element_type=F32) + b1_ref[...]))
    h2 = jnp.sin(fr * (jnp.dot(h1, w2_ref[...], precision=hi, preferred_element_type=F32) + b2_ref[...]))
    h = _bdot(h2, w3_ref[...])
    t = z_ref[:, 0:1]
    o_ref[...] = h * jnp.exp(-t * dl_ref[...])


def _hyena_filters(zfeat, w1p, b1, w2, b2, w3, freq, deltas4):
    L = zfeat.shape[0]
    tm = 512
    n = w3.shape[1]
    full = lambda a: pl.BlockSpec(a.shape, lambda i: (0,) * a.ndim)
    return pl.pallas_call(
        functools.partial(_filter_kernel, tm=tm, L=L),
        grid=(L // tm,),
        in_specs=[pl.BlockSpec((tm, zfeat.shape[1]), lambda i: (i, 0)), full(w1p), full(b1), full(w2), full(b2),
                  full(w3), full(freq), full(deltas4)],
        out_specs=pl.BlockSpec((tm, n), lambda i: (i, 0)),
        out_shape=jax.ShapeDtypeStruct((L, n), F32),
        compiler_params=_cparams(1),
        name="hyena_filters",
    )(zfeat, w1p, b1, w2, b2, w3, freq, deltas4)


def _dft_tables():
    T = CONV_T
    f = np.arange(T, dtype=np.float64)[:, None] + 0.5
    n = np.arange(T, dtype=np.float64)[None, :]
    th = 2.0 * np.pi * f * n / (2 * T)
    thn = 2.0 * np.pi * f * (n - T) / (2 * T)
    fwd = np.concatenate([np.cos(th), -np.sin(th)], axis=0)
    inv = np.concatenate([np.cos(th).T, -np.sin(th).T], axis=1) / T
    nz = np.ones((1, T)); nz[0, 0] = 0.0
    P = np.concatenate([np.cos(th), -np.sin(th)], axis=0)
    Pc = np.concatenate([np.cos(th), np.sin(th)], axis=0)
    N = np.concatenate([np.cos(thn), -np.sin(thn)], axis=0) * nz
    Nc = np.concatenate([np.cos(thn), np.sin(thn)], axis=0) * nz
    m_pos = np.concatenate([N, P], axis=1)
    m_zero = np.concatenate([Pc * nz, P], axis=1)
    m_neg = np.concatenate([Nc, Pc], axis=1)
    mats = np.stack([m_pos, m_zero, m_neg], axis=0)
    as_bf16 = lambda a: jnp.asarray(a, dtype=F32).astype(BF16)
    return as_bf16(fwd), as_bf16(inv), as_bf16(mats)


def _spec_kernel(m_ref, a_ref, b_ref, o_ref):
    T = CONV_T
    o_ref[0, 0] = _bdot(m_ref[0, :, :T], a_ref[...]) + _bdot(m_ref[0, :, T:], b_ref[...])


def _filter_spectra(h, mats):
    L = h.shape[0]
    T = CONV_T
    J = L // T
    W = HY_WIDTH
    ct = 256
    nct = W // ct
    nlag = 2 * J - 1

    def sel(dl):
        d = dl - (J - 1)
        return jnp.where(d > 0, 0, jnp.where(d == 0, 1, 2))

    def a_map(o, dl, c):
        d = dl - (J - 1)
        row = jnp.where(d > 0, d - 1, jnp.where(d == 0, 0, -d - 1))
        dirn = jnp.where(d > 0, 0, 1)
        return (row, (o * 2 + dirn) * nct + c)

    def b_map(o, dl, c):
        d = dl - (J - 1)
        row = jnp.where(d >= 0, d, -d)
        dirn = jnp.where(d >= 0, 0, 1)
        return (row, (o * 2 + dirn) * nct + c)

    return pl.pallas_call(
        _spec_kernel,
        grid=(HY_ORDER, nlag, nct),
        in_specs=[pl.BlockSpec((1, 2 * T, 2 * T), lambda o, dl, c: (sel(dl), 0, 0)),
                  pl.BlockSpec((T, ct), a_map),
                  pl.BlockSpec((T, ct), b_map)],
        out_specs=pl.BlockSpec((1, 1, 2 * T, ct), lambda o, dl, c: (o, dl, 0, c)),
        out_shape=jax.ShapeDtypeStruct((HY_ORDER, nlag, 2 * T, W), F32),
        compiler_params=_cparams(3),
        name="filter_spectra",
    )(mats, h, h)


def _sconv_kernel(x_ref, w_ref, b_ref, o_ref):
    x = x_ref[0]
    L = x.shape[0]
    t = lax.broadcasted_iota(jnp.int32, x.shape, 0)
    prev = jnp.where(t == 0, 0.0, pltpu.roll(x, 1, 0))
    nxt = jnp.where(t == L - 1, 0.0, pltpu.roll(x, L - 1, 0))
    o_ref[0] = prev * w_ref[0:1, :] + x * w_ref[1:2, :] + nxt * w_ref[2:3, :] + b_ref[...]


def _short_conv(hy, w, b):
    B, L, Wd = hy.shape
    ct = 256
    return pl.pallas_call(
        _sconv_kernel,
        grid=(B, Wd // ct),
        in_specs=[pl.BlockSpec((1, L, ct), lambda bb, c: (bb, 0, c)),
                  pl.BlockSpec((HY_SHORT, ct), lambda bb, c: (0, c)),
                  pl.BlockSpec((1, ct), lambda bb, c: (0, c))],
        out_specs=pl.BlockSpec((1, L, ct), lambda bb, c: (bb, 0, c)),
        out_shape=jax.ShapeDtypeStruct((B, L, Wd), F32),
        compiler_params=_cparams(2),
        name="hyena_short_conv",
    )(hy, w, b.reshape(1, Wd))


def _lconv_kernel(u_ref, gate_ref, bias_ref, fwd_ref, inv_ref, h_ref, o_ref, u_scr, y_scr, *, J, ct, GI):
    T = CONV_T
    g = pl.program_id(2)

    @pl.when(g == 0)
    def _():
        for j in range(J):
            u_scr[j] = _bdot(fwd_ref[...], u_ref[0, j * T:(j + 1) * T, :])

    def run_group(i0):
        def tile_body(rt, carry):
            r0 = pl.multiple_of(rt * 8, 8)
            for lt in range(ct // LANES):
                cs = slice(lt * LANES, (lt + 1) * LANES)
                ure = [u_scr[j, pl.ds(r0, 8), cs] for j in range(J)]
                uim = [u_scr[j, pl.ds(T + r0, 8), cs] for j in range(J)]
                for ii in range(GI):
                    yre = jnp.zeros((8, LANES), F32)
                    yim = jnp.zeros((8, LANES), F32)
                    for j in range(J):
                        lag = i0 + ii - j + (J - 1)
                        hre = h_ref[0, lag, pl.ds(r0, 8), cs]
                        him = h_ref[0, lag, pl.ds(T + r0, 8), cs]
                        yre = yre + hre * ure[j] - him * uim[j]
                        yim = yim + hre * uim[j] + him * ure[j]
                    y_scr[ii, pl.ds(r0, 8), cs] = yre
                    y_scr[ii, pl.ds(T + r0, 8), cs] = yim
            return carry

        lax.fori_loop(0, T // 8, tile_body, 0)
        for ii in range(GI):
            y = _bdot(inv_ref[...], y_scr[ii])
            ui = u_ref[0, (i0 + ii) * T:(i0 + ii + 1) * T, :]
            o_ref[0, ii * T:(ii + 1) * T, :] = gate_ref[0, ii * T:(ii + 1) * T, :] * (y + ui * bias_ref[...])

    for gg in range(J // GI):
        pl.when(g == gg)(functools.partial(run_group, gg * GI))


def _long_conv(u_arr, u_blk0, gate_arr, gate_blk0, bias_row, fwd, inv, hspec, order):
    B, L, _ = u_arr.shape
    T = CONV_T
    J = L // T
    ct = 256
    nct = HY_WIDTH // ct
    nlag = 2 * J - 1
    GI = 4
    return pl.pallas_call(
        functools.partial(_lconv_kernel, J=J, ct=ct, GI=GI),
        grid=(nct, B, J // GI),
        in_specs=[pl.BlockSpec((1, L, ct), lambda c, b, i: (b, 0, u_blk0 + c)),
                  pl.BlockSpec((1, GI * T, ct), lambda c, b, i: (b, i, gate_blk0 + c)),
                  pl.BlockSpec((1, ct), lambda c, b, i: (0, c)),
                  pl.BlockSpec(fwd.shape, lambda c, b, i: (0, 0)),
                  pl.BlockSpec(inv.shape, lambda c, b, i: (0, 0)),
                  pl.BlockSpec((1, nlag, 2 * T, ct), lambda c, b, i: (order, 0, 0, c),
                               pipeline_mode=pl.Buffered(1))],
        out_specs=pl.BlockSpec((1, GI * T, ct), lambda c, b, i: (b, i, c)),
        out_shape=jax.ShapeDtypeStruct((B, L, HY_WIDTH), F32),
        scratch_shapes=[pltpu.VMEM((J, 2 * T, ct), F32), pltpu.VMEM((GI, 2 * T, ct), F32)],
        compiler_params=_cparams(3, VMEM_LIMIT_BIG),
        name="hyena_long_conv%d" % order,
    )(u_arr, gate_arr, bias_row, fwd, inv, hspec)


def _merge_kernel(of_ref, ob_ref, g_ref, hy_ref, x0_ref, mod_ref, gn_ref, wout_ref, fg_ref, wr_ref,
                  x1_ref, lg_ref):
    o = of_ref[0] + ob_ref[0]
    g = g_ref[0]
    parts = []
    for h in range(GLA_HEADS):
        oh = o[:, h * GLA_DV:(h + 1) * GLA_DV]
        parts.append(_rms(oh) * gn_ref[...])
    y_gla = jnp.concatenate(parts, axis=-1) * (g * jax.nn.sigmoid(g))
    ycat = jnp.concatenate([y_gla, hy_ref[0]], axis=-1)
    m = _bdot(ycat, wout_ref[...])
    gate = mod_ref[0, :, 2 * D_MODEL:3 * D_MODEL]
    x1 = x0_ref[0] + gate * m
    x1_ref[0] = x1
    shift = mod_ref[0, :, 3 * D_MODEL:4 * D_MODEL]
    scale = mod_ref[0, :, 4 * D_MODEL:5 * D_MODEL]
    hn = ((_rms(x1) * fg_ref[...]) * (1.0 + scale) + shift).astype(BF16)
    lg_ref[0] = lax.dot_general(wr_ref[...], hn, (((1,), (1,)), ((), ())), preferred_element_type=F32)


def _merge(o_f, o_b, g, y_hy, x0, modrows, gla_norm_g, w_out, norm_ffn_g, w_router_t):
    B, L, D = x0.shape
    tm = 256
    tmap = lambda b, i: (b, i, 0)
    full2 = lambda a: pl.BlockSpec(a.shape, lambda b, i: (0, 0))
    return pl.pallas_call(
        _merge_kernel,
        grid=(B, L // tm),
        in_specs=[pl.BlockSpec((1, tm, GLA_V_W), tmap), pl.BlockSpec((1, tm, GLA_V_W), tmap),
                  pl.BlockSpec((1, tm, GLA_V_W), tmap), pl.BlockSpec((1, tm, HY_WIDTH), tmap),
                  pl.BlockSpec((1, tm, D), tmap),
                  pl.BlockSpec((1, 1, modrows.shape[-1]), lambda b, i: (b, 0, 0)),
                  full2(gla_norm_g), full2(w_out), full2(norm_ffn_g), full2(w_router_t)],
        out_specs=[pl.BlockSpec((1, tm, D), tmap), pl.BlockSpec((1, N_EXPERTS, tm), lambda b, i: (b, 0, i))],
        out_shape=[jax.ShapeDtypeStruct((B, L, D), F32), jax.ShapeDtypeStruct((B, N_EXPERTS, L), F32)],
        compiler_params=_cparams(2),
        name="merge_router",
    )(o_f, o_b, g, y_hy, x0, modrows, gla_norm_g, w_out, norm_ffn_g, w_router_t)


def _topk_kernel(lg_ref, idx_ref, wt_ref, pos_scr, *, cap):
    E, L = lg_ref.shape[1], lg_ref.shape[2]
    lg = lg_ref[0]
    mx = jnp.max(lg, axis=0, keepdims=True)
    ex = jnp.exp(lg - mx)
    aff = ex / jnp.sum(ex, axis=0, keepdims=True)

    def count(mask):
        return jnp.sum(mask.astype(jnp.int32), axis=1, keepdims=True)

    def tbody(n, cur):
        cand = cur | lax.shift_left(jnp.int32(1), 30 - n)
        return jnp.where(count(aff >= lax.bitcast_convert_type(cand, F32)) >= cap, cand, cur)

    thr = lax.bitcast_convert_type(lax.fori_loop(0, 31, tbody, jnp.zeros((E, 1), jnp.int32)), F32)
    gt = aff > thr
    eq = aff == thr
    need = cap - count(gt)
    tok = lax.broadcasted_iota(jnp.int32, (E, L), 1)

    def ibody(n, cur):
        cand = cur + lax.shift_left(jnp.int32(1), 12 - n)
        ok = (cand <= L) & (count(eq & (tok < cand)) <= need)
        return jnp.where(ok, cand, cur)

    bound = lax.fori_loop(0, 13, ibody, jnp.zeros((E, 1), jnp.int32))
    sel = gt | (eq & (tok < bound))

    li = lax.broadcasted_iota(jnp.int32, (LANES, LANES), 0)
    lj = lax.broadcasted_iota(jnp.int32, (LANES, LANES), 1)
    ustrict = (li < lj).astype(BF16)
    off = jnp.zeros((E, 1), F32)
    for r in range(L // LANES):
        s_r = sel[:, r * LANES:(r + 1) * LANES]
        sf = s_r.astype(F32)
        pre = jnp.dot(sf.astype(BF16), ustrict, preferred_element_type=F32)
        pos_scr[:, r * LANES:(r + 1) * LANES] = jnp.where(s_r, pre + off, -1.0)
        off = off + jnp.sum(sf, axis=1, keepdims=True)

    slot = lax.broadcasted_iota(jnp.int32, (cap, L), 0).astype(F32)
    tokf = lax.broadcasted_iota(jnp.int32, (8, L), 1)
    row = lax.broadcasted_iota(jnp.int32, (8, L), 0)
    t_hi = (tokf // 64).astype(F32)
    t_lo = (tokf % 64).astype(F32)
    for e in range(E):
        a = aff[e:e + 1, :]
        a1 = a.astype(BF16).astype(F32)
        a2 = (a - a1).astype(BF16).astype(F32)
        a3 = a - a1 - a2
        vals = jnp.where(row == 0, t_hi, jnp.where(row == 1, t_lo, jnp.where(row == 2, a1, jnp.where(
            row == 3, a2, jnp.where(row == 4, a3, 0.0)))))
        onehot = jnp.where(pos_scr[e:e + 1, :] == slot, 1.0, 0.0).astype(BF16)
        res = lax.dot_general(vals.astype(BF16), onehot, (((1,), (1,)), ((), ())), preferred_element_type=F32)
        idx_ref[0, e:e + 1, :] = (res[0:1, :] * 64.0 + res[1:2, :]).astype(jnp.int32)
        wt_ref[0, e:e + 1, :] = res[2:3, :] + res[3:4, :] + res[4:5, :]


def _topk(logits_t, cap):
    B, E, L = logits_t.shape
    return pl.pallas_call(
        functools.partial(_topk_kernel, cap=cap),
        grid=(B,),
        in_specs=[pl.BlockSpec((1, E, L), lambda b: (b, 0, 0))],
        out_specs=[pl.BlockSpec((1, E, cap), lambda b: (b, 0, 0)), pl.BlockSpec((1, E, cap), lambda b: (b, 0, 0))],
        out_shape=[jax.ShapeDtypeStruct((B, E, cap), jnp.int32), jax.ShapeDtypeStruct((B, E, cap), F32)],
        scratch_shapes=[pltpu.VMEM((E, L), F32)],
        compiler_params=_cparams(1),
        name="ec_topk",
    )(logits_t)


def _moe_kernel(idxc_ref, wtc_ref, idxn_ref, idxp_ref, wtp_ref, x1_hbm, mod_ref, ng_ref, wg_ref, wu_ref, wd_ref,
                fg_ref, out_hbm, acc, tab, xsp, xs_scr, y_part, y_fin, sem, *, cap, n_exp):
    b = pl.program_id(0)
    e = pl.program_id(1)
    hf = pl.program_id(2)
    L = acc.shape[0]
    rows = 512
    group = 8
    half = D_MODEL // 2

    def gather_row(idx_ref, i):
        xsp[pl.ds(i, 1), :] = tab[pl.ds(idx_ref[0, 0, i], 1), :]

    def scatter_group(idx_ref, wt_ref, i0):
        ts = [idx_ref[0, 0, i0 + r] for r in range(group)]
        new = [acc[pl.ds(ts[r], 1), :] + y_fin[pl.ds(i0 + r, 1), :] * wt_ref[0, 0, i0 + r] for r in range(group)]
        for r in range(group):
            acc[pl.ds(ts[r], 1), :] = new[r]

    def ffn_part():
        xs = xs_scr[...]
        gte = _bdot(xs, wg_ref[0])
        up = _bdot(xs, wu_ref[0])
        return _bdot(gte * jax.nn.sigmoid(gte) * up, wd_ref[0])

    def unpack_rows():
        xp = xsp[...]
        lo = pltpu.unpack_elementwise(xp, index=0, packed_dtype=BF16, unpacked_dtype=F32)
        hi = pltpu.unpack_elementwise(xp, index=1, packed_dtype=BF16, unpacked_dtype=F32)
        xs_scr[...] = jnp.concatenate([lo, hi], axis=-1).astype(BF16)

    @pl.when((e == 0) & (hf == 0))
    def _():
        cp = pltpu.make_async_copy(x1_hbm.at[b], acc, sem.at[0])
        cp.start()
        cp.wait()
        shift = mod_ref[0, :, 3 * D_MODEL:4 * D_MODEL]
        scale = mod_ref[0, :, 4 * D_MODEL:5 * D_MODEL]

        def norm(n, carry):
            r0 = pl.multiple_of(n * rows, rows)
            hn = (_rms(acc[pl.ds(r0, rows), :]) * ng_ref[...]) * (1.0 + scale) + shift
            tab[pl.ds(r0, rows), :] = pltpu.pack_elementwise([hn[:, :half], hn[:, half:]], packed_dtype=BF16)
            return carry

        lax.fori_loop(0, L // rows, norm, 0)

        def gather0(gi, carry):
            i0 = pl.multiple_of(gi * group, group)
            for r in range(group):
                gather_row(idxc_ref, i0 + r)
            return carry

        lax.fori_loop(0, cap // group, gather0, 0)
        unpack_rows()
        y_part[...] = ffn_part()

    @pl.when((e > 0) & (hf == 0))
    def _():
        unpack_rows()
        y_part[...] = ffn_part()
        for i0 in range(0, cap, group):
            scatter_group(idxp_ref, wtp_ref, i0)

    @pl.when(hf == 1)
    def _():
        y_fin[...] = (y_part[...] + ffn_part()) * mod_ref[0, :, 5 * D_MODEL:6 * D_MODEL]
        for i in range(cap):
            gather_row(idxn_ref, i)

    @pl.when((e == n_exp - 1) & (hf == 1))
    def _():
        def scatter_last(gi, carry):
            scatter_group(idxc_ref, wtc_ref, pl.multiple_of(gi * group, group))
            return carry

        lax.fori_loop(0, cap // group, scatter_last, 0)

        def fin(n, carry):
            r0 = pl.multiple_of(n * rows, rows)
            acc[pl.ds(r0, rows), :] = _rms(acc[pl.ds(r0, rows), :]) * fg_ref[...]
            return carry

        lax.fori_loop(0, L // rows, fin, 0)
        cp = pltpu.make_async_copy(acc, out_hbm.at[b], sem.at[1])
        cp.start()
        cp.wait()


def _moe(idx, wts, x1, modrows, norm_ffn_g, wg, wu, wd, norm_final_g):
    B, L, D = x1.shape
    E, cap = idx.shape[1], idx.shape[2]
    idx3 = idx.reshape(B * E, 1, cap)
    wts3 = wts.reshape(B * E, 1, cap)

    def smem_spec(shift):
        return pl.BlockSpec((1, 1, cap), lambda b, e, h: (b * E + jnp.clip(e + shift, 0, E - 1), 0, 0),
                            memory_space=pltpu.SMEM)

    n_half = 2
    fc = D_EXPERT // n_half
    wspec = lambda: pl.BlockSpec((1, D, fc), lambda b, e, h: (e, 0, h))
    return pl.pallas_call(
        functools.partial(_moe_kernel, cap=cap, n_exp=E),
        grid=(B, E, n_half),
        in_specs=[smem_spec(0), smem_spec(0), smem_spec(1), smem_spec(-1), smem_spec(-1),
                  pl.BlockSpec(memory_space=pl.ANY),
                  pl.BlockSpec((1, 1, modrows.shape[-1]), lambda b, e, h: (b, 0, 0)),
                  pl.BlockSpec((1, D), lambda b, e, h: (0, 0)),
                  wspec(), wspec(),
                  pl.BlockSpec((1, fc, D), lambda b, e, h: (e, h, 0)),
                  pl.BlockSpec((1, D), lambda b, e, h: (0, 0))],
        out_specs=pl.BlockSpec(memory_space=pl.ANY),
        out_shape=jax.ShapeDtypeStruct((B, L, D), F32),
        scratch_shapes=[pltpu.VMEM((L, D), F32), pltpu.VMEM((L, D // 2), jnp.int32),
                        pltpu.VMEM((cap, D // 2), jnp.int32), pltpu.VMEM((cap, D), BF16),
                        pltpu.VMEM((cap, D), F32), pltpu.VMEM((cap, D), F32), pltpu.SemaphoreType.DMA((2,))],
        compiler_params=_cparams(3, VMEM_LIMIT_BIG),
        name="ec_moe",
    )(idx3, wts3, idx3, idx3, wts3, x1, modrows, norm_ffn_g.reshape(1, D), wg, wu, wd, norm_final_g.reshape(1, D))


def _pos_tables():
    quarter = D_MODEL // 4
    omega = 1.0 / (POS_BASE ** (np.arange(quarter, dtype=np.float64) / quarter))
    n = np.arange(GRID_W, dtype=np.float64)[:, None] * omega[None, :]
    tab = np.concatenate([np.sin(n), np.cos(n)], axis=-1).astype(np.float32)
    return tab


def _filter_features(L):
    t = np.linspace(0.0, 1.0, L, dtype=np.float32).astype(np.float64)[:, None]
    w = 2.0 * math.pi * np.arange(L, dtype=np.float64)[:, None] / L
    f = np.linspace(1e-4, HY_BANDS - 1, HY_BANDS, dtype=np.float32).astype(np.float64)[None, :]
    z = np.concatenate([t, np.cos(f * w), -np.sin(f * w)], axis=-1)
    zp = np.zeros((L, LANES), np.float32)
    zp[:, :HY_EMB] = z.astype(np.float32)
    return zp


def kernel(x, c, ctx, c_ctx, w_ada, b_ada, norm_mix_g, w_in, gla_wa_f, gla_ba_f, gla_wa_b, gla_ba_b, gla_norm_g,
           hy_conv_w, hy_conv_b, hy_w1, hy_b1, hy_w2, hy_b2, hy_w3, hy_freq, hy_bias, w_out, norm_ffn_g, w_router,
           w_gate, w_up, w_down, norm_final_g):
    B, L, D = x.shape
    assert w_ada.shape[0] == 1 and D == D_MODEL and L % (GRID_W * 8) == 0
    l = 0
    cap = EC_FACTOR * L // N_EXPERTS

    cc = jnp.zeros((8, D), F32).at[:B].set(c).at[B].set(c_ctx)
    modrows = _adaln(cc, w_ada[l], b_ada[l]).reshape(8, 1, N_MOD * D)

    s = (GLA_QK_W, 2 * GLA_QK_W, 2 * GLA_QK_W + GLA_V_W, 2 * GLA_QK_W + 2 * GLA_V_W,
         2 * GLA_QK_W + 2 * GLA_V_W + GLA_RANK, 2 * GLA_QK_W + 2 * GLA_V_W + 2 * GLA_RANK)
    w = w_in[l]
    w_q, w_k, w_v, w_g, w_af, w_ab, w_hy = (w[:, :s[0]], w[:, s[0]:s[1]], w[:, s[1]:s[2]], w[:, s[2]:s[3]],
                                            w[:, s[3]:s[4]], w[:, s[4]:s[5]], w[:, s[5]:])
    w_a = jnp.concatenate([w_af, w_ab, jnp.zeros((D, LANES - 2 * GLA_RANK), F32)], axis=1)
    w_x = jnp.concatenate([w_q, w_k, w_v, w_g, w_hy, w_a], axis=1).astype(BF16)
    w_c = jnp.concatenate([w_k, w_v, w_a], axis=1).astype(BF16)
    wa = jnp.zeros((LANES, 2 * GLA_QK_W), F32)
    wa = wa.at[:GLA_RANK, :GLA_QK_W].set(gla_wa_f[l]).at[GLA_RANK:2 * GLA_RANK, GLA_QK_W:].set(gla_wa_b[l])
    wa = wa.astype(BF16)
    ba = jnp.concatenate([gla_ba_f[l], gla_ba_b[l]]).reshape(1, 2 * GLA_QK_W)

    tab = jnp.asarray(_pos_tables())
    x0, q, k, v, g, hy, la_f, la_b = _project(
        x, modrows, None, norm_mix_g[l], w_x, wa, ba, (tab, tab),
        (GLA_QK_W, GLA_QK_W, GLA_V_W, GLA_V_W, 3 * HY_WIDTH), True, 512)
    k_c, v_c, laf_c, lab_c = _project(ctx, modrows, B, norm_mix_g[l], w_c, wa, ba, None,
                                      (GLA_QK_W, GLA_V_W), False, ctx.shape[1])

    s_zero = jnp.zeros((B, GLA_V_W, GLA_QK_W), F32)
    nb_c = ctx.shape[1] // GLA_CHUNK
    _, s_f = _gla_pass(None, k_c, v_c, laf_c, s_zero, False, nb_c)
    _, s_b = _gla_pass(None, k_c, v_c, lab_c, s_zero, True, nb_c)
    o_f, _ = _gla_pass(q, k, v, la_f, s_f, False, 8)
    o_b, _ = _gla_pass(q, k, v, la_b, s_b, True, 8)

    w1p = jnp.zeros((LANES, HY_HIDDEN), F32).at[:HY_EMB].set(hy_w1[l])
    deltas = np.abs(np.linspace(HY_MIN_DECAY, HY_MAX_DECAY, HY_WIDTH, dtype=np.float32))
    deltas4 = jnp.asarray(np.tile(deltas, HY_ORDER * 2).reshape(1, -1))
    h = _hyena_filters(jnp.asarray(_filter_features(L)), w1p, hy_b1[l].reshape(1, -1), hy_w2[l],
                       hy_b2[l].reshape(1, -1), hy_w3[l], hy_freq[l].reshape(1, -1), deltas4)
    fwd, inv, mats = _dft_tables()
    hspec = _filter_spectra(h, mats)

    u = _short_conv(hy, hy_conv_w[l], hy_conv_b[l])
    nct = HY_WIDTH // 256
    z1 = _long_conv(u, 0, u, nct, hy_bias[l][0:1], fwd, inv, hspec, 0)
    y_hy = _long_conv(z1, 0, u, 2 * nct, hy_bias[l][1:2], fwd, inv, hspec, 1)

    x1, logits_t = _merge(o_f, o_b, g, y_hy, x0, modrows, gla_norm_g[l].reshape(1, GLA_DV),
                                   w_out[l].astype(BF16), norm_ffn_g[l].reshape(1, D),
                                   w_router[l].T.astype(BF16))
    idx, wts = _topk(logits_t, cap)
    return _moe(idx, wts, x1, modrows, norm_ffn_g[l], w_gate[l], w_up[l], w_down[l], norm_final_g)
```

```python
import functools
import math

import numpy as np
import jax
import jax.numpy as jnp
from jax import lax
from jax.experimental import pallas as pl
from jax.experimental.pallas import tpu as pltpu

F32 = jnp.float32
BF16 = jnp.bfloat16

D_MODEL = 1024
GRID_W = 64
EPS = 1e-6
POS_BASE = 10000.0
N_MOD = 6

GLA_HEADS = 4
GLA_DK = 64
GLA_DV = 128
GLA_RANK = 16
GLA_GATE_NORM = 16.0
GLA_CHUNK = 64
GLA_QK_W = GLA_HEADS * GLA_DK
GLA_V_W = GLA_HEADS * GLA_DV

HY_WIDTH = D_MODEL - GLA_V_W
HY_ORDER = 2
HY_BANDS = 16
HY_EMB = 2 * HY_BANDS + 1
HY_HIDDEN = 64
HY_SHORT = 3
HY_MIN_DECAY = math.log(1e-2) / 1.5
HY_MAX_DECAY = math.log(1e-2) / 0.3

N_EXPERTS = 16
EC_FACTOR = 2
D_EXPERT = 1024

VMEM_LIMIT_BIG = 56 * 1024 * 1024
VMEM_LIMIT_MID = 40 * 1024 * 1024
LANES = 128

CONV_T = 512


def _cparams(n_axes, vmem=VMEM_LIMIT_MID):
    return pltpu.CompilerParams(dimension_semantics=("arbitrary",) * n_axes, vmem_limit_bytes=vmem)


def _bdot(a, b):
    return jnp.dot(a.astype(BF16), b.astype(BF16), preferred_element_type=F32)


def _rms(x):
    return x * lax.rsqrt(jnp.mean(x * x, axis=-1, keepdims=True) + EPS)


def _mod_kernel(c_ref, w_ref, b_ref, o_ref):
    c = c_ref[...]
    s = c * jax.nn.sigmoid(c)
    o_ref[...] = _bdot(s, w_ref[...]) + b_ref[...]


def _adaln(cc, w_ada, b_ada):
    rows, d = cc.shape
    n = w_ada.shape[1]
    tn = 1536
    return pl.pallas_call(
        _mod_kernel,
        grid=(n // tn,),
        in_specs=[pl.BlockSpec((rows, d), lambda j: (0, 0)),
                  pl.BlockSpec((d, tn), lambda j: (0, j)),
                  pl.BlockSpec((1, tn), lambda j: (0, j))],
        out_specs=pl.BlockSpec((rows, tn), lambda j: (0, j)),
        out_shape=jax.ShapeDtypeStruct((rows, n), F32),
        compiler_params=_cparams(1),
        name="adaln",
    )(cc, w_ada, b_ada.reshape(1, n))


def _proj_kernel(*refs, with_pos, tm, widths, q_scale_first):
    if with_pos:
        x_ref, xp_ref, xn_ref, rt_ref, ct_ref, cw_ref, cb_ref, mod_ref, g_ref, w_ref, wa_ref, ba_ref = refs[:12]
        outs = refs[12:]
    else:
        x_ref, mod_ref, g_ref, w_ref, wa_ref, ba_ref = refs[:6]
        outs = refs[6:]
    x = x_ref[0]
    if with_pos:
        i = pl.program_id(0)
        n_tiles = pl.num_programs(0)
        rows_per_tile = tm // GRID_W
        n_rows = rt_ref.shape[0]
        halo = xp_ref.shape[1]
        r0 = pl.multiple_of(i * rows_per_tile, rows_per_tile)
        rt = rt_ref[pl.ds(r0, rows_per_tile), :]
        half = D_MODEL // 2
        x3 = x.reshape(rows_per_tile, GRID_W, D_MODEL)
        x3 = jnp.concatenate([x3[:, :, :half] + rt[:, None, :], x3[:, :, half:] + ct_ref[...][None]], axis=-1)
        x = x3.reshape(tm, D_MODEL)
        outs[0][0] = x
        outs = outs[1:]
        rt_prev = rt_ref[pl.ds(jnp.maximum(r0 - 1, 0), 1), :]
        rt_next = rt_ref[pl.ds(jnp.minimum(r0 + rows_per_tile, n_rows - 1), 1), :]
        x_prev = xp_ref[0] + jnp.concatenate(
            [jnp.broadcast_to(rt_prev, (halo, half)), ct_ref[GRID_W - halo:GRID_W, :]], axis=-1)
        x_next = xn_ref[0] + jnp.concatenate([jnp.broadcast_to(rt_next, (halo, half)), ct_ref[0:halo, :]], axis=-1)
        x = jnp.concatenate([x, x_prev, x_next], axis=0)
    shift = mod_ref[0, :, 0:D_MODEL]
    scale = mod_ref[0, :, D_MODEL:2 * D_MODEL]
    hn = (_rms(x) * g_ref[...]) * (1.0 + scale) + shift
    p_all = _bdot(hn, w_ref[...])
    p = p_all[:tm]
    off = 0
    for n, (o_ref, w) in enumerate(zip(outs[:-2], widths)):
        blk = p[:, off:off + w]
        if q_scale_first and n == 0:
            blk = blk * (GLA_DK ** -0.5)
        if with_pos and n == len(widths) - 1:
            before = jnp.where(i == 0, 0.0, p_all[tm + halo - 1:tm + halo, off:off + w])
            after = jnp.where(i == n_tiles - 1, 0.0, p_all[tm + halo:tm + halo + 1, off:off + w])
            t = lax.broadcasted_iota(jnp.int32, blk.shape, 0)
            up = jnp.where(t == 0, before, pltpu.roll(blk, 1, 0))
            dn = jnp.where(t == tm - 1, after, pltpu.roll(blk, tm - 1, 0))
            blk = up * cw_ref[0:1, :] + blk * cw_ref[1:2, :] + dn * cw_ref[2:3, :] + cb_ref[...]
        o_ref[0] = blk
        off += w
    a = p[:, off:off + LANES]
    z = _bdot(a, wa_ref[...]) + ba_ref[...]
    la = jax.nn.log_sigmoid(z) / GLA_GATE_NORM
    outs[-2][0] = la[:, :GLA_QK_W]
    outs[-1][0] = la[:, GLA_QK_W:]


def _project(x, modrows, mod_row0, norm_g, w_cols, wa, ba, tabs, widths, q_scale_first, tm, conv=None):
    B, L, D = x.shape
    with_pos = tabs is not None
    ncols = w_cols.shape[1]
    in_specs = [pl.BlockSpec((1, tm, D), lambda i, b: (b, i, 0))]
    args = [x]
    if with_pos:
        rt, ct = tabs
        cw, cb = conv
        halo = 8
        per = tm // halo
        in_specs += [pl.BlockSpec((1, halo, D), lambda i, b: (b, jnp.maximum(i * per - 1, 0), 0)),
                     pl.BlockSpec((1, halo, D), lambda i, b: (b, jnp.minimum((i + 1) * per, L // halo - 1), 0)),
                     pl.BlockSpec(rt.shape, lambda i, b: (0, 0)), pl.BlockSpec(ct.shape, lambda i, b: (0, 0)),
                     pl.BlockSpec(cw.shape, lambda i, b: (0, 0)), pl.BlockSpec(cb.shape, lambda i, b: (0, 0))]
        args += [x, x, rt, ct, cw, cb]
    if mod_row0 is None:
        mod_map = lambda i, b: (b, 0, 0)
    else:
        mod_map = lambda i, b: (mod_row0, 0, 0)
    in_specs += [pl.BlockSpec((1, 1, modrows.shape[-1]), mod_map),
                 pl.BlockSpec((1, D), lambda i, b: (0, 0)),
                 pl.BlockSpec((D, ncols), lambda i, b: (0, 0)),
                 pl.BlockSpec(wa.shape, lambda i, b: (0, 0)),
                 pl.BlockSpec(ba.shape, lambda i, b: (0, 0))]
    args += [modrows, norm_g.reshape(1, D), w_cols, wa, ba]
    out_w = ([D] if with_pos else []) + list(widths) + [GLA_QK_W, GLA_QK_W]
    out_specs = [pl.BlockSpec((1, tm, w), lambda i, b: (b, i, 0)) for w in out_w]
    out_shape = [jax.ShapeDtypeStruct((B, L, w), F32) for w in out_w]
    return pl.pallas_call(
        functools.partial(_proj_kernel, with_pos=with_pos, tm=tm, widths=tuple(widths), q_scale_first=q_scale_first),
        grid=(L // tm, B),
        in_specs=in_specs, out_specs=out_specs, out_shape=out_shape,
        compiler_params=_cparams(2, VMEM_LIMIT_BIG),
        name="project_x" if with_pos else "project_ctx",
    )(*args)


def _gla_kernel(*refs, reverse, with_output, nb):
    if with_output:
        q_ref, k_ref, v_ref, la_ref, s0_ref, o_ref, sfin_ref, s_scr = refs
    else:
        k_ref, v_ref, la_ref, s0_ref, sfin_ref, s_scr = refs
    C = GLA_CHUNK

    @pl.when(pl.program_id(1) == 0)
    def _():
        s_scr[...] = s0_ref[0]

    H = GLA_HEADS
    nt = (((1,), (1,)), ((), ()))
    tn = (((0,), (0,)), ((), ()))
    row = lax.broadcasted_iota(jnp.int32, (C, GLA_QK_W), 0)
    sc_r = lax.broadcasted_iota(jnp.int32, (C, H * C), 0)
    sc_s = lax.broadcasted_iota(jnp.int32, (C, H * C), 1) % C
    keep = (sc_r <= sc_s) if reverse else (sc_r >= sc_s)
    def head_mask(shape, rdiv, cdiv):
        return (lax.broadcasted_iota(jnp.int32, shape, 0) // rdiv) == (lax.broadcasted_iota(jnp.int32, shape, 1) // cdiv)
    m_k = head_mask((H * C, GLA_QK_W), C, GLA_DK)
    m_v = head_mask((H * C, GLA_V_W), C, GLA_DV)
    m_s = head_mask((GLA_V_W, GLA_QK_W), GLA_DV, GLA_DK)

    def prefix_sum(x):
        s = 1
        while s < C:
            if reverse:
                x = x + jnp.where(row < C - s, pltpu.roll(x, C - s, 0), 0.0)
            else:
                x = x + jnp.where(row >= s, pltpu.roll(x, s, 0), 0.0)
            s *= 2
        return x

    order = range(nb - 1, -1, -1) if reverse else range(nb)
    for ci in order:
        sl = slice(ci * C, (ci + 1) * C)
        k = k_ref[0, sl, :]
        v = v_ref[0, sl, :]
        b = prefix_sum(la_ref[0, sl, :])
        last = 0 if reverse else C - 1
        b_tot = b[last:last + 1, :]
        kdec = (k * jnp.exp(b_tot - b)).astype(BF16)
        vb = v.astype(BF16)
        upd_t = jnp.where(m_s, lax.dot_general(vb, kdec, tn, preferred_element_type=F32), 0.0)
        st = s_scr[...]
        if with_output:
            q = q_ref[0, sl, :]
            mid = C // 2 - 1 if reverse else C // 2
            ref = b[mid:mid + 1, :]
            qs = (q * jnp.exp(b - ref)).astype(BF16)
            ks = (k * jnp.exp(ref - b)).astype(BF16)
            qb = (q * jnp.exp(b)).astype(BF16)
            o_inter = lax.dot_general(qb, st.astype(BF16), nt, preferred_element_type=F32)
            ks4 = jnp.where(m_k, jnp.concatenate([ks] * H, axis=0), jnp.zeros((), BF16))
            sc = lax.dot_general(qs, ks4, nt, preferred_element_type=F32)
            sc = jnp.where(keep, sc, 0.0).astype(BF16)
            v4 = jnp.where(m_v, jnp.concatenate([vb] * H, axis=0), jnp.zeros((), BF16))
            o_ref[0, sl, :] = o_inter + jnp.dot(sc, v4, preferred_element_type=F32)
        s_scr[...] = st * jnp.exp(b_tot) + upd_t
    sfin_ref[0] = s_scr[...]


def _gla_pass(q, k, v, la, s0, reverse, nb):
    B, L, _ = k.shape
    with_output = q is not None
    tm = nb * GLA_CHUNK
    nblk = L // tm
    if reverse:
        tmap = lambda b, i: (b, nblk - 1 - i, 0)
    else:
        tmap = lambda b, i: (b, i, 0)
    smap = lambda b, i: (b, 0, 0)
    in_specs, args = [], []
    if with_output:
        in_specs.append(pl.BlockSpec((1, tm, GLA_QK_W), tmap))
        args.append(q)
    in_specs += [pl.BlockSpec((1, tm, GLA_QK_W), tmap), pl.BlockSpec((1, tm, GLA_V_W), tmap),
                 pl.BlockSpec((1, tm, GLA_QK_W), tmap), pl.BlockSpec((1, GLA_V_W, GLA_QK_W), smap)]
    args += [k, v, la, s0]
    out_specs, out_shape = [], []
    if with_output:
        out_specs.append(pl.BlockSpec((1, tm, GLA_V_W), tmap))
        out_shape.append(jax.ShapeDtypeStruct((B, L, GLA_V_W), F32))
    out_specs.append(pl.BlockSpec((1, GLA_V_W, GLA_QK_W), smap))
    out_shape.append(jax.ShapeDtypeStruct((B, GLA_V_W, GLA_QK_W), F32))
    res = pl.pallas_call(
        functools.partial(_gla_kernel, reverse=reverse, with_output=with_output, nb=nb),
        grid=(B, nblk),
        in_specs=in_specs, out_specs=out_specs, out_shape=out_shape,
        scratch_shapes=[pltpu.VMEM((GLA_V_W, GLA_QK_W), F32)],
        compiler_params=_cparams(2),
        name=("gla_" + ("bwd" if reverse else "fwd") + ("" if with_output else "_state")),
    )(*args)
    return (res[0], res[1]) if with_output else (None, res[0])


def _filter_kernel(z_ref, w1_ref, b1_ref, w2_ref, b2_ref, w3_ref, fr_ref, dl_ref, o_ref, *, tm, L):
    hi = lax.Precision.HIGHEST
    fr = fr_ref[...]
    h1 = jnp.sin(fr * (jnp.dot(z_ref[...], w1_ref[...], precision=hi, preferred_element_type=F32) + b1_ref[...]))
    h2 = jnp.sin(fr * (jnp.dot(h1, w2_ref[...], precision=hi, preferred_element_type=F32) + b2_ref[...]))
    h = _bdot(h2, w3_ref[...])
    t = z_ref[:, 0:1]
    o_ref[...] = h * jnp.exp(-t * dl_ref[...])


def _hyena_filters(zfeat, w1p, b1, w2, b2, w3, freq, deltas4):
    L = zfeat.shape[0]
    tm = 512
    n = w3.shape[1]
    full = lambda a: pl.BlockSpec(a.shape, lambda i: (0,) * a.ndim)
    return pl.pallas_call(
        functools.partial(_filter_kernel, tm=tm, L=L),
        grid=(L // tm,),
        in_specs=[pl.BlockSpec((tm, zfeat.shape[1]), lambda i: (i, 0)), full(w1p), full(b1), full(w2), full(b2),
                  full(w3), full(freq), full(deltas4)],
        out_specs=pl.BlockSpec((tm, n), lambda i: (i, 0)),
        out_shape=jax.ShapeDtypeStruct((L, n), F32),
        compiler_params=_cparams(1),
        name="hyena_filters",
    )(zfeat, w1p, b1, w2, b2, w3, freq, deltas4)


def _dft_tables():
    T = CONV_T
    f = np.arange(T, dtype=np.float64)[:, None] + 0.5
    n = np.arange(T, dtype=np.float64)[None, :]
    th = 2.0 * np.pi * f * n / (2 * T)
    thn = 2.0 * np.pi * f * (n - T) / (2 * T)
    fwd = np.concatenate([np.cos(th), -np.sin(th)], axis=0)
    inv = np.concatenate([np.cos(th).T, -np.sin(th).T], axis=1) / T
    nz = np.ones((1, T)); nz[0, 0] = 0.0
    P = np.concatenate([np.cos(th), -np.sin(th)], axis=0)
    Pc = np.concatenate([np.cos(th), np.sin(th)], axis=0)
    N = np.concatenate([np.cos(thn), -np.sin(thn)], axis=0) * nz
    Nc = np.concatenate([np.cos(thn), np.sin(thn)], axis=0) * nz
    m_pos = np.concatenate([N, P], axis=1)
    m_zero = np.concatenate([Pc * nz, P], axis=1)
    m_neg = np.concatenate([Nc, Pc], axis=1)
    mats = np.stack([m_pos, m_zero, m_neg], axis=0)
    as_bf16 = lambda a: jnp.asarray(a, dtype=F32).astype(BF16)
    return as_bf16(fwd), as_bf16(inv), as_bf16(mats)


def _spec_kernel(m_ref, a_ref, b_ref, o_ref):
    T = CONV_T
    o_ref[0, 0] = _bdot(m_ref[0, :, :T], a_ref[...]) + _bdot(m_ref[0, :, T:], b_ref[...])


def _filter_spectra(h, mats):
    L = h.shape[0]
    T = CONV_T
    J = L // T
    W = HY_WIDTH
    ct = W
    nct = W // ct
    nlag = 2 * J - 1

    def sel(dl):
        d = dl - (J - 1)
        return jnp.where(d > 0, 0, jnp.where(d == 0, 1, 2))

    def a_map(o, dl, c):
        d = dl - (J - 1)
        row = jnp.where(d > 0, d - 1, jnp.where(d == 0, 0, -d - 1))
        dirn = jnp.where(d > 0, 0, 1)
        return (row, (o * 2 + dirn) * nct + c)

    def b_map(o, dl, c):
        d = dl - (J - 1)
        row = jnp.where(d >= 0, d, -d)
        dirn = jnp.where(d >= 0, 0, 1)
        return (row, (o * 2 + dirn) * nct + c)

    return pl.pallas_call(
        _spec_kernel,
        grid=(HY_ORDER, nlag, nct),
        in_specs=[pl.BlockSpec((1, 2 * T, 2 * T), lambda o, dl, c: (sel(dl), 0, 0)),
                  pl.BlockSpec((T, ct), a_map),
                  pl.BlockSpec((T, ct), b_map)],
        out_specs=pl.BlockSpec((1, 1, 2 * T, ct), lambda o, dl, c: (o, dl, 0, c)),
        out_shape=jax.ShapeDtypeStruct((HY_ORDER, nlag, 2 * T, W), F32),
        compiler_params=_cparams(3),
        name="filter_spectra",
    )(mats, h, h)


def _lconv_kernel(u_ref, gate_ref, bias_ref, fwd_ref, inv_ref, h_ref, o_ref, u_scr, y_scr, *, J, ct, GI):
    T = CONV_T
    g = pl.program_id(2)

    @pl.when(g == 0)
    def _():
        for j in range(J):
            u_scr[j] = _bdot(fwd_ref[...], u_ref[0, j * T:(j + 1) * T, :])

    def run_group(i0):
        def tile_body(rt, carry):
            r0 = pl.multiple_of(rt * 8, 8)
            for lt in range(ct // LANES):
                cs = slice(lt * LANES, (lt + 1) * LANES)
                ure = [u_scr[j, pl.ds(r0, 8), cs] for j in range(J)]
                uim = [u_scr[j, pl.ds(T + r0, 8), cs] for j in range(J)]
                for ii in range(GI):
                    yre = jnp.zeros((8, LANES), F32)
                    yim = jnp.zeros((8, LANES), F32)
                    for j in range(J):
                        lag = i0 + ii - j + (J - 1)
                        hre = h_ref[0, lag, pl.ds(r0, 8), cs]
                        him = h_ref[0, lag, pl.ds(T + r0, 8), cs]
                        yre = yre + hre * ure[j] - him * uim[j]
                        yim = yim + hre * uim[j] + him * ure[j]
                    y_scr[ii, pl.ds(r0, 8), cs] = yre
                    y_scr[ii, pl.ds(T + r0, 8), cs] = yim
            return carry

        lax.fori_loop(0, T // 8, tile_body, 0)
        for ii in range(GI):
            y = _bdot(inv_ref[...], y_scr[ii])
            ui = u_ref[0, (i0 + ii) * T:(i0 + ii + 1) * T, :]
            o_ref[0, ii * T:(ii + 1) * T, :] = gate_ref[0, ii * T:(ii + 1) * T, :] * (y + ui * bias_ref[...])

    for gg in range(J // GI):
        pl.when(g == gg)(functools.partial(run_group, gg * GI))


def _long_conv(u_arr, u_blk0, gate_arr, gate_blk0, bias_row, fwd, inv, hspec, order):
    B, L, _ = u_arr.shape
    T = CONV_T
    J = L // T
    ct = 256
    nct = HY_WIDTH // ct
    nlag = 2 * J - 1
    GI = 4
    return pl.pallas_call(
        functools.partial(_lconv_kernel, J=J, ct=ct, GI=GI),
        grid=(nct, B, J // GI),
        in_specs=[pl.BlockSpec((1, L, ct), lambda c, b, i: (b, 0, u_blk0 + c)),
                  pl.BlockSpec((1, GI * T, ct), lambda c, b, i: (b, i, gate_blk0 + c)),
                  pl.BlockSpec((1, ct), lambda c, b, i: (0, c)),
                  pl.BlockSpec(fwd.shape, lambda c, b, i: (0, 0)),
                  pl.BlockSpec(inv.shape, lambda c, b, i: (0, 0)),
                  pl.BlockSpec((1, nlag, 2 * T, ct), lambda c, b, i: (order, 0, 0, c),
                               pipeline_mode=pl.Buffered(1))],
        out_specs=pl.BlockSpec((1, GI * T, ct), lambda c, b, i: (b, i, c)),
        out_shape=jax.ShapeDtypeStruct((B, L, HY_WIDTH), F32),
        scratch_shapes=[pltpu.VMEM((J, 2 * T, ct), F32), pltpu.VMEM((GI, 2 * T, ct), F32)],
        compiler_params=_cparams(3, VMEM_LIMIT_BIG),
        name="hyena_long_conv%d" % order,
    )(u_arr, gate_arr, bias_row, fwd, inv, hspec)


def _merge_kernel(of_ref, ob_ref, g_ref, hy_ref, x0_ref, mod_ref, gn_ref, wout_ref, fg_ref, wr_ref,
                  x1_ref, lg_ref):
    o = of_ref[0] + ob_ref[0]
    g = g_ref[0]
    parts = []
    for h in range(GLA_HEADS):
        oh = o[:, h * GLA_DV:(h + 1) * GLA_DV]
        parts.append(_rms(oh) * gn_ref[...])
    y_gla = jnp.concatenate(parts, axis=-1) * (g * jax.nn.sigmoid(g))
    ycat = jnp.concatenate([y_gla, hy_ref[0]], axis=-1)
    m = _bdot(ycat, wout_ref[...])
    gate = mod_ref[0, :, 2 * D_MODEL:3 * D_MODEL]
    x1 = x0_ref[0] + gate * m
    x1_ref[0] = x1
    shift = mod_ref[0, :, 3 * D_MODEL:4 * D_MODEL]
    scale = mod_ref[0, :, 4 * D_MODEL:5 * D_MODEL]
    hn = ((_rms(x1) * fg_ref[...]) * (1.0 + scale) + shift).astype(BF16)
    lg_ref[0] = lax.dot_general(wr_ref[...], hn, (((1,), (1,)), ((), ())), preferred_element_type=F32)


def _merge(o_f, o_b, g, y_hy, x0, modrows, gla_norm_g, w_out, norm_ffn_g, w_router_t):
    B, L, D = x0.shape
    tm = 512
    tmap = lambda b, i: (b, i, 0)
    full2 = lambda a: pl.BlockSpec(a.shape, lambda b, i: (0, 0))
    return pl.pallas_call(
        _merge_kernel,
        grid=(B, L // tm),
        in_specs=[pl.BlockSpec((1, tm, GLA_V_W), tmap), pl.BlockSpec((1, tm, GLA_V_W), tmap),
                  pl.BlockSpec((1, tm, GLA_V_W), tmap), pl.BlockSpec((1, tm, HY_WIDTH), tmap),
                  pl.BlockSpec((1, tm, D), tmap),
                  pl.BlockSpec((1, 1, modrows.shape[-1]), lambda b, i: (b, 0, 0)),
                  full2(gla_norm_g), full2(w_out), full2(norm_ffn_g), full2(w_router_t)],
        out_specs=[pl.BlockSpec((1, tm, D), tmap), pl.BlockSpec((1, N_EXPERTS, tm), lambda b, i: (b, 0, i))],
        out_shape=[jax.ShapeDtypeStruct((B, L, D), F32), jax.ShapeDtypeStruct((B, N_EXPERTS, L), F32)],
        compiler_params=_cparams(2),
        name="merge_router",
    )(o_f, o_b, g, y_hy, x0, modrows, gla_norm_g, w_out, norm_ffn_g, w_router_t)


def _topk_kernel(lg_ref, idx_ref, wt_ref, pos_scr, *, cap):
    E, L = lg_ref.shape[1], lg_ref.shape[2]
    lg = lg_ref[0]
    mx = jnp.max(lg, axis=0, keepdims=True)
    ex = jnp.exp(lg - mx)
    aff = ex / jnp.sum(ex, axis=0, keepdims=True)

    def count(mask):
        return jnp.sum(mask.astype(jnp.int32), axis=1, keepdims=True)

    def tbody(n, cur):
        cand = cur | lax.shift_left(jnp.int32(1), 30 - n)
        return jnp.where(count(aff >= lax.bitcast_convert_type(cand, F32)) >= cap, cand, cur)

    thr = lax.bitcast_convert_type(lax.fori_loop(0, 31, tbody, jnp.zeros((E, 1), jnp.int32)), F32)
    gt = aff > thr
    eq = aff == thr
    need = cap - count(gt)
    tok = lax.broadcasted_iota(jnp.int32, (E, L), 1)

    def ibody(n, cur):
        cand = cur + lax.shift_left(jnp.int32(1), 12 - n)
        ok = (cand <= L) & (count(eq & (tok < cand)) <= need)
        return jnp.where(ok, cand, cur)

    bound = lax.fori_loop(0, 13, ibody, jnp.zeros((E, 1), jnp.int32))
    sel = gt | (eq & (tok < bound))

    li = lax.broadcasted_iota(jnp.int32, (LANES, LANES), 0)
    lj = lax.broadcasted_iota(jnp.int32, (LANES, LANES), 1)
    ustrict = (li < lj).astype(BF16)
    off = jnp.zeros((E, 1), F32)
    for r in range(L // LANES):
        s_r = sel[:, r * LANES:(r + 1) * LANES]
        sf = s_r.astype(F32)
        pre = jnp.dot(sf.astype(BF16), ustrict, preferred_element_type=F32)
        pos_scr[:, r * LANES:(r + 1) * LANES] = jnp.where(s_r, pre + off, -1.0)
        off = off + jnp.sum(sf, axis=1, keepdims=True)

    slot = lax.broadcasted_iota(jnp.int32, (cap, L), 0).astype(F32)
    tokf = lax.broadcasted_iota(jnp.int32, (8, L), 1)
    row = lax.broadcasted_iota(jnp.int32, (8, L), 0)
    t_hi = (tokf // 64).astype(F32)
    t_lo = (tokf % 64).astype(F32)
    for e in range(E):
        a = aff[e:e + 1, :]
        a1 = a.astype(BF16).astype(F32)
        a2 = (a - a1).astype(BF16).astype(F32)
        a3 = a - a1 - a2
        vals = jnp.where(row == 0, t_hi, jnp.where(row == 1, t_lo, jnp.where(row == 2, a1, jnp.where(
            row == 3, a2, jnp.where(row == 4, a3, 0.0)))))
        onehot = jnp.where(pos_scr[e:e + 1, :] == slot, 1.0, 0.0).astype(BF16)
        res = lax.dot_general(vals.astype(BF16), onehot, (((1,), (1,)), ((), ())), preferred_element_type=F32)
        idx_ref[0, e:e + 1, :] = (res[0:1, :] * 64.0 + res[1:2, :]).astype(jnp.int32)
        wt_ref[0, e:e + 1, :] = res[2:3, :] + res[3:4, :] + res[4:5, :]


def _topk(logits_t, cap):
    B, E, L = logits_t.shape
    return pl.pallas_call(
        functools.partial(_topk_kernel, cap=cap),
        grid=(B,),
        in_specs=[pl.BlockSpec((1, E, L), lambda b: (b, 0, 0))],
        out_specs=[pl.BlockSpec((1, E, cap), lambda b: (b, 0, 0)), pl.BlockSpec((1, E, cap), lambda b: (b, 0, 0))],
        out_shape=[jax.ShapeDtypeStruct((B, E, cap), jnp.int32), jax.ShapeDtypeStruct((B, E, cap), F32)],
        scratch_shapes=[pltpu.VMEM((E, L), F32)],
        compiler_params=_cparams(1),
        name="ec_topk",
    )(logits_t)


def _moe_kernel(idxc_ref, wtc_ref, idxn_ref, idxp_ref, wtp_ref, x1_hbm, mod_ref, ng_ref, wg_ref, wu_ref, wd_ref,
                fg_ref, out_hbm, acc, tab, xsp, xs_scr, y_part, y_fin, sem, *, cap, n_exp):
    b = pl.program_id(0)
    e = pl.program_id(1)
    hf = pl.program_id(2)
    L = acc.shape[0]
    rows = 512
    group = 8
    half = D_MODEL // 2

    def gather_row(idx_ref, i):
        xsp[pl.ds(i, 1), :] = tab[pl.ds(idx_ref[0, 0, i], 1), :]

    def scatter_group(idx_ref, wt_ref, i0):
        ts = [idx_ref[0, 0, i0 + r] for r in range(group)]
        new = [acc[pl.ds(ts[r], 1), :] + y_fin[pl.ds(i0 + r, 1), :] * wt_ref[0, 0, i0 + r] for r in range(group)]
        for r in range(group):
            acc[pl.ds(ts[r], 1), :] = new[r]

    def ffn_part():
        xs = xs_scr[...]
        gte = _bdot(xs, wg_ref[0])
        up = _bdot(xs, wu_ref[0])
        return _bdot(gte * jax.nn.sigmoid(gte) * up, wd_ref[0])

    def unpack_rows():
        xp = xsp[...]
        lo = pltpu.unpack_elementwise(xp, index=0, packed_dtype=BF16, unpacked_dtype=F32)
        hi = pltpu.unpack_elementwise(xp, index=1, packed_dtype=BF16, unpacked_dtype=F32)
        xs_scr[...] = jnp.concatenate([lo, hi], axis=-1).astype(BF16)

    @pl.when((e == 0) & (hf == 0))
    def _():
        cp = pltpu.make_async_copy(x1_hbm.at[b], acc, sem.at[0])
        cp.start()
        cp.wait()
        shift = mod_ref[0, :, 3 * D_MODEL:4 * D_MODEL]
        scale = mod_ref[0, :, 4 * D_MODEL:5 * D_MODEL]

        def norm(n, carry):
            r0 = pl.multiple_of(n * rows, rows)
            hn = (_rms(acc[pl.ds(r0, rows), :]) * ng_ref[...]) * (1.0 + scale) + shift
            tab[pl.ds(r0, rows), :] = pltpu.pack_elementwise([hn[:, :half], hn[:, half:]], packed_dtype=BF16)
            return carry

        lax.fori_loop(0, L // rows, norm, 0)

        def gather0(gi, carry):
            i0 = pl.multiple_of(gi * group, group)
            for r in range(group):
                gather_row(idxc_ref, i0 + r)
            return carry

        lax.fori_loop(0, cap // group, gather0, 0)
        unpack_rows()
        y_part[...] = ffn_part()

    @pl.when((e > 0) & (hf == 0))
    def _():
        unpack_rows()
        y_part[...] = ffn_part()
        for i0 in range(0, cap, group):
            scatter_group(idxp_ref, wtp_ref, i0)

    @pl.when(hf == 1)
    def _():
        y_fin[...] = (y_part[...] + ffn_part()) * mod_ref[0, :, 5 * D_MODEL:6 * D_MODEL]
        for i in range(cap):
            gather_row(idxn_ref, i)

    @pl.when((e == n_exp - 1) & (hf == 1))
    def _():
        def scatter_last(gi, carry):
            scatter_group(idxc_ref, wtc_ref, pl.multiple_of(gi * group, group))
            return carry

        lax.fori_loop(0, cap // group, scatter_last, 0)

        def fin(n, carry):
            r0 = pl.multiple_of(n * rows, rows)
            acc[pl.ds(r0, rows), :] = _rms(acc[pl.ds(r0, rows), :]) * fg_ref[...]
            return carry

        lax.fori_loop(0, L // rows, fin, 0)
        cp = pltpu.make_async_copy(acc, out_hbm.at[b], sem.at[1])
        cp.start()
        cp.wait()


def _moe(idx, wts, x1, modrows, norm_ffn_g, wg, wu, wd, norm_final_g):
    B, L, D = x1.shape
    E, cap = idx.shape[1], idx.shape[2]
    idx3 = idx.reshape(B * E, 1, cap)
    wts3 = wts.reshape(B * E, 1, cap)

    def smem_spec(shift):
        return pl.BlockSpec((1, 1, cap), lambda b, e, h: (b * E + jnp.clip(e + shift, 0, E - 1), 0, 0),
                            memory_space=pltpu.SMEM)

    n_half = 2
    fc = D_EXPERT // n_half
    wspec = lambda: pl.BlockSpec((1, D, fc), lambda b, e, h: (e, 0, h))
    return pl.pallas_call(
        functools.partial(_moe_kernel, cap=cap, n_exp=E),
        grid=(B, E, n_half),
        in_specs=[smem_spec(0), smem_spec(0), smem_spec(1), smem_spec(-1), smem_spec(-1),
                  pl.BlockSpec(memory_space=pl.ANY),
                  pl.BlockSpec((1, 1, modrows.shape[-1]), lambda b, e, h: (b, 0, 0)),
                  pl.BlockSpec((1, D), lambda b, e, h: (0, 0)),
                  wspec(), wspec(),
                  pl.BlockSpec((1, fc, D), lambda b, e, h: (e, h, 0)),
                  pl.BlockSpec((1, D), lambda b, e, h: (0, 0))],
        out_specs=pl.BlockSpec(memory_space=pl.ANY),
        out_shape=jax.ShapeDtypeStruct((B, L, D), F32),
        scratch_shapes=[pltpu.VMEM((L, D), F32), pltpu.VMEM((L, D // 2), jnp.uint32),
                        pltpu.VMEM((cap, D // 2), jnp.uint32), pltpu.VMEM((cap, D), BF16),
                        pltpu.VMEM((cap, D), F32), pltpu.VMEM((cap, D), F32), pltpu.SemaphoreType.DMA((2,))],
        compiler_params=_cparams(3, VMEM_LIMIT_BIG),
        name="ec_moe",
    )(idx3, wts3, idx3, idx3, wts3, x1, modrows, norm_ffn_g.reshape(1, D), wg, wu, wd, norm_final_g.reshape(1, D))


def _pos_tables():
    quarter = D_MODEL // 4
    omega = 1.0 / (POS_BASE ** (np.arange(quarter, dtype=np.float64) / quarter))
    n = np.arange(GRID_W, dtype=np.float64)[:, None] * omega[None, :]
    tab = np.concatenate([np.sin(n), np.cos(n)], axis=-1).astype(np.float32)
    return tab


def _filter_features(L):
    t = np.linspace(0.0, 1.0, L, dtype=np.float32).astype(np.float64)[:, None]
    w = 2.0 * math.pi * np.arange(L, dtype=np.float64)[:, None] / L
    f = np.linspace(1e-4, HY_BANDS - 1, HY_BANDS, dtype=np.float32).astype(np.float64)[None, :]
    z = np.concatenate([t, np.cos(f * w), -np.sin(f * w)], axis=-1)
    zp = np.zeros((L, LANES), np.float32)
    zp[:, :HY_EMB] = z.astype(np.float32)
    return zp


def kernel(x, c, ctx, c_ctx, w_ada, b_ada, norm_mix_g, w_in, gla_wa_f, gla_ba_f, gla_wa_b, gla_ba_b, gla_norm_g,
           hy_conv_w, hy_conv_b, hy_w1, hy_b1, hy_w2, hy_b2, hy_w3, hy_freq, hy_bias, w_out, norm_ffn_g, w_router,
           w_gate, w_up, w_down, norm_final_g):
    B, L, D = x.shape
    assert w_ada.shape[0] == 1 and D == D_MODEL and L % (GRID_W * 8) == 0
    l = 0
    cap = EC_FACTOR * L // N_EXPERTS

    cc = jnp.zeros((8, D), F32).at[:B].set(c).at[B].set(c_ctx)
    modrows = _adaln(cc, w_ada[l], b_ada[l]).reshape(8, 1, N_MOD * D)

    s = (GLA_QK_W, 2 * GLA_QK_W, 2 * GLA_QK_W + GLA_V_W, 2 * GLA_QK_W + 2 * GLA_V_W,
         2 * GLA_QK_W + 2 * GLA_V_W + GLA_RANK, 2 * GLA_QK_W + 2 * GLA_V_W + 2 * GLA_RANK)
    w = w_in[l]
    w_q, w_k, w_v, w_g, w_af, w_ab, w_hy = (w[:, :s[0]], w[:, s[0]:s[1]], w[:, s[1]:s[2]], w[:, s[2]:s[3]],
                                            w[:, s[3]:s[4]], w[:, s[4]:s[5]], w[:, s[5]:])
    w_a = jnp.concatenate([w_af, w_ab, jnp.zeros((D, LANES - 2 * GLA_RANK), F32)], axis=1)
    w_x = jnp.concatenate([w_q, w_k, w_v, w_g, w_hy, w_a], axis=1).astype(BF16)
    w_c = jnp.concatenate([w_k, w_v, w_a], axis=1).astype(BF16)
    wa = jnp.zeros((LANES, 2 * GLA_QK_W), F32)
    wa = wa.at[:GLA_RANK, :GLA_QK_W].set(gla_wa_f[l]).at[GLA_RANK:2 * GLA_RANK, GLA_QK_W:].set(gla_wa_b[l])
    wa = wa.astype(BF16)
    ba = jnp.concatenate([gla_ba_f[l], gla_ba_b[l]]).reshape(1, 2 * GLA_QK_W)

    tab = jnp.asarray(_pos_tables())
    x0, q, k, v, g, u, la_f, la_b = _project(
        x, modrows, None, norm_mix_g[l], w_x, wa, ba, (tab, tab),
        (GLA_QK_W, GLA_QK_W, GLA_V_W, GLA_V_W, 3 * HY_WIDTH), True, 512,
        conv=(hy_conv_w[l], hy_conv_b[l].reshape(1, -1)))
    k_c, v_c, laf_c, lab_c = _project(ctx, modrows, B, norm_mix_g[l], w_c, wa, ba, None,
                                      (GLA_QK_W, GLA_V_W), False, ctx.shape[1])

    s_zero = jnp.zeros((B, GLA_V_W, GLA_QK_W), F32)
    nb_c = ctx.shape[1] // GLA_CHUNK
    _, s_f = _gla_pass(None, k_c, v_c, laf_c, s_zero, False, nb_c)
    _, s_b = _gla_pass(None, k_c, v_c, lab_c, s_zero, True, nb_c)
    o_f, _ = _gla_pass(q, k, v, la_f, s_f, False, 8)
    o_b, _ = _gla_pass(q, k, v, la_b, s_b, True, 8)

    w1p = jnp.zeros((LANES, HY_HIDDEN), F32).at[:HY_EMB].set(hy_w1[l])
    deltas = np.abs(np.linspace(HY_MIN_DECAY, HY_MAX_DECAY, HY_WIDTH, dtype=np.float32))
    deltas4 = jnp.asarray(np.tile(deltas, HY_ORDER * 2).reshape(1, -1))
    h = _hyena_filters(jnp.asarray(_filter_features(L)), w1p, hy_b1[l].reshape(1, -1), hy_w2[l],
                       hy_b2[l].reshape(1, -1), hy_w3[l], hy_freq[l].reshape(1, -1), deltas4)
    fwd, inv, mats = _dft_tables()
    hspec = _filter_spectra(h, mats)

    nct = HY_WIDTH // 256
    z1 = _long_conv(u, 0, u, nct, hy_bias[l][0:1], fwd, inv, hspec, 0)
    y_hy = _long_conv(z1, 0, u, 2 * nct, hy_bias[l][1:2], fwd, inv, hspec, 1)

    x1, logits_t = _merge(o_f, o_b, g, y_hy, x0, modrows, gla_norm_g[l].reshape(1, GLA_DV),
                                   w_out[l].astype(BF16), norm_ffn_g[l].reshape(1, D),
                                   w_router[l].T.astype(BF16))
    idx, wts = _topk(logits_t, cap)
    return _moe(idx, wts, x1, modrows, norm_ffn_g[l], w_gate[l], w_up[l], w_down[l], norm_final_g)
```

```python
import functools
import math

import numpy as np
import jax
import jax.numpy as jnp
from jax import lax
from jax.experimental import pallas as pl
from jax.experimental.pallas import tpu as pltpu

F32 = jnp.float32
BF16 = jnp.bfloat16

D_MODEL = 1024
GRID_W = 64
EPS = 1e-6
POS_BASE = 10000.0
N_MOD = 6

GLA_HEADS = 4
GLA_DK = 64
GLA_DV = 128
GLA_RANK = 16
GLA_GATE_NORM = 16.0
GLA_CHUNK = 64
GLA_QK_W = GLA_HEADS * GLA_DK
GLA_V_W = GLA_HEADS * GLA_DV

HY_WIDTH = D_MODEL - GLA_V_W
HY_ORDER = 2
HY_BANDS = 16
HY_EMB = 2 * HY_BANDS + 1
HY_HIDDEN = 64
HY_SHORT = 3
HY_MIN_DECAY = math.log(1e-2) / 1.5
HY_MAX_DECAY = math.log(1e-2) / 0.3

N_EXPERTS = 16
EC_FACTOR = 2
D_EXPERT = 1024

VMEM_LIMIT_BIG = 56 * 1024 * 1024
VMEM_LIMIT_MID = 40 * 1024 * 1024
LANES = 128

CONV_T = 512
MOE_ROWS = 512


def _cparams(n_axes, vmem=VMEM_LIMIT_MID):
    return pltpu.CompilerParams(dimension_semantics=("arbitrary",) * n_axes, vmem_limit_bytes=vmem)


def _bdot(a, b):
    return jnp.dot(a.astype(BF16), b.astype(BF16), preferred_element_type=F32)


def _rms(x):
    return x * lax.rsqrt(jnp.mean(x * x, axis=-1, keepdims=True) + EPS)


def _mod_kernel(c_ref, w_ref, b_ref, o_ref):
    c = c_ref[...]
    s = c * jax.nn.sigmoid(c)
    o_ref[...] = _bdot(s, w_ref[...]) + b_ref[...]


def _adaln(cc, w_ada, b_ada):
    rows, d = cc.shape
    n = w_ada.shape[1]
    tn = 1536
    return pl.pallas_call(
        _mod_kernel,
        grid=(n // tn,),
        in_specs=[pl.BlockSpec((rows, d), lambda j: (0, 0)),
                  pl.BlockSpec((d, tn), lambda j: (0, j)),
                  pl.BlockSpec((1, tn), lambda j: (0, j))],
        out_specs=pl.BlockSpec((rows, tn), lambda j: (0, j)),
        out_shape=jax.ShapeDtypeStruct((rows, n), F32),
        compiler_params=_cparams(1),
        name="adaln",
    )(cc, w_ada, b_ada.reshape(1, n))


def _proj_kernel(*refs, with_pos, tm, widths, q_scale_first):
    if with_pos:
        x_ref, xp_ref, xn_ref, rt_ref, ct_ref, cw_ref, cb_ref, mod_ref, g_ref, w_ref, wa_ref, ba_ref = refs[:12]
        outs = refs[12:]
    else:
        x_ref, mod_ref, g_ref, w_ref, wa_ref, ba_ref = refs[:6]
        outs = refs[6:]
    x = x_ref[0]
    if with_pos:
        i = pl.program_id(0)
        n_tiles = pl.num_programs(0)
        rows_per_tile = tm // GRID_W
        n_rows = rt_ref.shape[0]
        halo = xp_ref.shape[1]
        r0 = pl.multiple_of(i * rows_per_tile, rows_per_tile)
        rt = rt_ref[pl.ds(r0, rows_per_tile), :]
        half = D_MODEL // 2
        x3 = x.reshape(rows_per_tile, GRID_W, D_MODEL)
        x3 = jnp.concatenate([x3[:, :, :half] + rt[:, None, :], x3[:, :, half:] + ct_ref[...][None]], axis=-1)
        x = x3.reshape(tm, D_MODEL)
        outs[0][0] = x
        outs = outs[1:]
        rt_prev = rt_ref[pl.ds(jnp.maximum(r0 - 1, 0), 1), :]
        rt_next = rt_ref[pl.ds(jnp.minimum(r0 + rows_per_tile, n_rows - 1), 1), :]
        x_prev = xp_ref[0] + jnp.concatenate(
            [jnp.broadcast_to(rt_prev, (halo, half)), ct_ref[GRID_W - halo:GRID_W, :]], axis=-1)
        x_next = xn_ref[0] + jnp.concatenate([jnp.broadcast_to(rt_next, (halo, half)), ct_ref[0:halo, :]], axis=-1)
        x = jnp.concatenate([x, x_prev, x_next], axis=0)
    shift = mod_ref[0, :, 0:D_MODEL]
    scale = mod_ref[0, :, D_MODEL:2 * D_MODEL]
    hn = (_rms(x) * g_ref[...]) * (1.0 + scale) + shift
    p_all = _bdot(hn, w_ref[...])
    p = p_all[:tm]
    off = 0
    for n, (o_ref, w) in enumerate(zip(outs[:-2], widths)):
        blk = p[:, off:off + w]
        if q_scale_first and n == 0:
            blk = blk * (GLA_DK ** -0.5)
        if with_pos and n == len(widths) - 1:
            before = jnp.where(i == 0, 0.0, p_all[tm + halo - 1:tm + halo, off:off + w])
            after = jnp.where(i == n_tiles - 1, 0.0, p_all[tm + halo:tm + halo + 1, off:off + w])
            t = lax.broadcasted_iota(jnp.int32, blk.shape, 0)
            up = jnp.where(t == 0, before, pltpu.roll(blk, 1, 0))
            dn = jnp.where(t == tm - 1, after, pltpu.roll(blk, tm - 1, 0))
            blk = up * cw_ref[0:1, :] + blk * cw_ref[1:2, :] + dn * cw_ref[2:3, :] + cb_ref[...]
        o_ref[0] = blk
        off += w
    a = p[:, off:off + LANES]
    z = _bdot(a, wa_ref[...]) + ba_ref[...]
    la = jax.nn.log_sigmoid(z) / GLA_GATE_NORM
    outs[-2][0] = la[:, :GLA_QK_W]
    outs[-1][0] = la[:, GLA_QK_W:]


def _project(x, modrows, mod_row0, norm_g, w_cols, wa, ba, tabs, widths, q_scale_first, tm, conv=None):
    B, L, D = x.shape
    with_pos = tabs is not None
    ncols = w_cols.shape[1]
    in_specs = [pl.BlockSpec((1, tm, D), lambda i, b: (b, i, 0))]
    args = [x]
    if with_pos:
        rt, ct = tabs
        cw, cb = conv
        halo = 8
        per = tm // halo
        in_specs += [pl.BlockSpec((1, halo, D), lambda i, b: (b, jnp.maximum(i * per - 1, 0), 0)),
                     pl.BlockSpec((1, halo, D), lambda i, b: (b, jnp.minimum((i + 1) * per, L // halo - 1), 0)),
                     pl.BlockSpec(rt.shape, lambda i, b: (0, 0)), pl.BlockSpec(ct.shape, lambda i, b: (0, 0)),
                     pl.BlockSpec(cw.shape, lambda i, b: (0, 0)), pl.BlockSpec(cb.shape, lambda i, b: (0, 0))]
        args += [x, x, rt, ct, cw, cb]
    if mod_row0 is None:
        mod_map = lambda i, b: (b, 0, 0)
    else:
        mod_map = lambda i, b: (mod_row0, 0, 0)
    in_specs += [pl.BlockSpec((1, 1, modrows.shape[-1]), mod_map),
                 pl.BlockSpec((1, D), lambda i, b: (0, 0)),
                 pl.BlockSpec((D, ncols), lambda i, b: (0, 0)),
                 pl.BlockSpec(wa.shape, lambda i, b: (0, 0)),
                 pl.BlockSpec(ba.shape, lambda i, b: (0, 0))]
    args += [modrows, norm_g.reshape(1, D), w_cols, wa, ba]
    out_w = ([D] if with_pos else []) + list(widths) + [GLA_QK_W, GLA_QK_W]
    out_specs = [pl.BlockSpec((1, tm, w), lambda i, b: (b, i, 0)) for w in out_w]
    out_shape = [jax.ShapeDtypeStruct((B, L, w), F32) for w in out_w]
    return pl.pallas_call(
        functools.partial(_proj_kernel, with_pos=with_pos, tm=tm, widths=tuple(widths), q_scale_first=q_scale_first),
        grid=(L // tm, B),
        in_specs=in_specs, out_specs=out_specs, out_shape=out_shape,
        compiler_params=_cparams(2, VMEM_LIMIT_BIG),
        name="project_x" if with_pos else "project_ctx",
    )(*args)


def _gla_kernel(*refs, reverse, with_output, nb):
    if with_output:
        q_ref, k_ref, v_ref, la_ref, s0_ref, o_ref, sfin_ref, s_scr = refs
    else:
        k_ref, v_ref, la_ref, s0_ref, sfin_ref, s_scr = refs
    C = GLA_CHUNK

    @pl.when(pl.program_id(1) == 0)
    def _():
        s_scr[...] = s0_ref[0]

    H = GLA_HEADS
    nt = (((1,), (1,)), ((), ()))
    tn = (((0,), (0,)), ((), ()))
    row = lax.broadcasted_iota(jnp.int32, (C, GLA_QK_W), 0)
    sc_r = lax.broadcasted_iota(jnp.int32, (C, H * C), 0)
    sc_s = lax.broadcasted_iota(jnp.int32, (C, H * C), 1) % C
    keep = (sc_r <= sc_s) if reverse else (sc_r >= sc_s)
    def head_mask(shape, rdiv, cdiv):
        return (lax.broadcasted_iota(jnp.int32, shape, 0) // rdiv) == (lax.broadcasted_iota(jnp.int32, shape, 1) // cdiv)
    m_k = head_mask((H * C, GLA_QK_W), C, GLA_DK)
    m_v = head_mask((H * C, GLA_V_W), C, GLA_DV)
    m_s = head_mask((GLA_V_W, GLA_QK_W), GLA_DV, GLA_DK)

    def prefix_sum(x):
        s = 1
        while s < C:
            if reverse:
                x = x + jnp.where(row < C - s, pltpu.roll(x, C - s, 0), 0.0)
            else:
                x = x + jnp.where(row >= s, pltpu.roll(x, s, 0), 0.0)
            s *= 2
        return x

    order = range(nb - 1, -1, -1) if reverse else range(nb)
    for ci in order:
        sl = slice(ci * C, (ci + 1) * C)
        k = k_ref[0, sl, :]
        v = v_ref[0, sl, :]
        b = prefix_sum(la_ref[0, sl, :])
        last = 0 if reverse else C - 1
        b_tot = b[last:last + 1, :]
        kdec = (k * jnp.exp(b_tot - b)).astype(BF16)
        vb = v.astype(BF16)
        upd_t = jnp.where(m_s, lax.dot_general(vb, kdec, tn, preferred_element_type=F32), 0.0)
        st = s_scr[...]
        if with_output:
            q = q_ref[0, sl, :]
            mid = C // 2 - 1 if reverse else C // 2
            ref = b[mid:mid + 1, :]
            qs = (q * jnp.exp(b - ref)).astype(BF16)
            ks = (k * jnp.exp(ref - b)).astype(BF16)
            qb = (q * jnp.exp(b)).astype(BF16)
            o_inter = lax.dot_general(qb, st.astype(BF16), nt, preferred_element_type=F32)
            ks4 = jnp.where(m_k, jnp.concatenate([ks] * H, axis=0), jnp.zeros((), BF16))
            sc = lax.dot_general(qs, ks4, nt, preferred_element_type=F32)
            sc = jnp.where(keep, sc, 0.0).astype(BF16)
            v4 = jnp.where(m_v, jnp.concatenate([vb] * H, axis=0), jnp.zeros((), BF16))
            o_ref[0, sl, :] = o_inter + jnp.dot(sc, v4, preferred_element_type=F32)
        s_scr[...] = st * jnp.exp(b_tot) + upd_t
    sfin_ref[0] = s_scr[...]


def _gla_pass(q, k, v, la, s0, reverse, nb):
    B, L, _ = k.shape
    with_output = q is not None
    tm = nb * GLA_CHUNK
    nblk = L // tm
    if reverse:
        tmap = lambda b, i: (b, nblk - 1 - i, 0)
    else:
        tmap = lambda b, i: (b, i, 0)
    smap = lambda b, i: (b, 0, 0)
    in_specs, args = [], []
    if with_output:
        in_specs.append(pl.BlockSpec((1, tm, GLA_QK_W), tmap))
        args.append(q)
    in_specs += [pl.BlockSpec((1, tm, GLA_QK_W), tmap), pl.BlockSpec((1, tm, GLA_V_W), tmap),
                 pl.BlockSpec((1, tm, GLA_QK_W), tmap), pl.BlockSpec((1, GLA_V_W, GLA_QK_W), smap)]
    args += [k, v, la, s0]
    out_specs, out_shape = [], []
    if with_output:
        out_specs.append(pl.BlockSpec((1, tm, GLA_V_W), tmap))
        out_shape.append(jax.ShapeDtypeStruct((B, L, GLA_V_W), F32))
    out_specs.append(pl.BlockSpec((1, GLA_V_W, GLA_QK_W), smap))
    out_shape.append(jax.ShapeDtypeStruct((B, GLA_V_W, GLA_QK_W), F32))
    res = pl.pallas_call(
        functools.partial(_gla_kernel, reverse=reverse, with_output=with_output, nb=nb),
        grid=(B, nblk),
        in_specs=in_specs, out_specs=out_specs, out_shape=out_shape,
        scratch_shapes=[pltpu.VMEM((GLA_V_W, GLA_QK_W), F32)],
        compiler_params=_cparams(2),
        name=("gla_" + ("bwd" if reverse else "fwd") + ("" if with_output else "_state")),
    )(*args)
    return (res[0], res[1]) if with_output else (None, res[0])


def _filter_kernel(z_ref, w1_ref, b1_ref, w2_ref, b2_ref, w3_ref, fr_ref, dl_ref, o_ref, *, tm, L):
    hi = lax.Precision.HIGHEST
    fr = fr_ref[...]
    h1 = jnp.sin(fr * (jnp.dot(z_ref[...], w1_ref[...], precision=hi, preferred_element_type=F32) + b1_ref[...]))
    h2 = jnp.sin(fr * (jnp.dot(h1, w2_ref[...], precision=hi, preferred_element_type=F32) + b2_ref[...]))
    h = _bdot(h2, w3_ref[...])
    t = z_ref[:, 0:1]
    o_ref[...] = h * jnp.exp(-t * dl_ref[...])


def _hyena_filters(zfeat, w1p, b1, w2, b2, w3, freq, deltas4):
    L = zfeat.shape[0]
    tm = 512
    n = w3.shape[1]
    full = lambda a: pl.BlockSpec(a.shape, lambda i: (0,) * a.ndim)
    return pl.pallas_call(
        functools.partial(_filter_kernel, tm=tm, L=L),
        grid=(L // tm,),
        in_specs=[pl.BlockSpec((tm, zfeat.shape[1]), lambda i: (i, 0)), full(w1p), full(b1), full(w2), full(b2),
                  full(w3), full(freq), full(deltas4)],
        out_specs=pl.BlockSpec((tm, n), lambda i: (i, 0)),
        out_shape=jax.ShapeDtypeStruct((L, n), F32),
        compiler_params=_cparams(1),
        name="hyena_filters",
    )(zfeat, w1p, b1, w2, b2, w3, freq, deltas4)


def _dft_tables():
    T = CONV_T
    f = np.arange(T, dtype=np.float64)[:, None] + 0.5
    n = np.arange(T, dtype=np.float64)[None, :]
    th = 2.0 * np.pi * f * n / (2 * T)
    thn = 2.0 * np.pi * f * (n - T) / (2 * T)
    fwd = np.concatenate([np.cos(th), -np.sin(th)], axis=0)
    inv = np.concatenate([np.cos(th).T, -np.sin(th).T], axis=1) / T
    nz = np.ones((1, T)); nz[0, 0] = 0.0
    P = np.concatenate([np.cos(th), -np.sin(th)], axis=0)
    Pc = np.concatenate([np.cos(th), np.sin(th)], axis=0)
    N = np.concatenate([np.cos(thn), -np.sin(thn)], axis=0) * nz
    Nc = np.concatenate([np.cos(thn), np.sin(thn)], axis=0) * nz
    m_pos = np.concatenate([N, P], axis=1)
    m_zero = np.concatenate([Pc * nz, P], axis=1)
    m_neg = np.concatenate([Nc, Pc], axis=1)
    mats = np.stack([m_pos, m_zero, m_neg], axis=0)
    as_bf16 = lambda a: jnp.asarray(a, dtype=F32).astype(BF16)
    return as_bf16(fwd), as_bf16(inv), as_bf16(mats)


def _spec_kernel(m_ref, a_ref, b_ref, o_ref):
    T = CONV_T
    o_ref[0, 0] = _bdot(m_ref[0, :, :T], a_ref[...]) + _bdot(m_ref[0, :, T:], b_ref[...])


def _filter_spectra(h, mats):
    L = h.shape[0]
    T = CONV_T
    J = L // T
    W = HY_WIDTH
    ct = W
    nct = W // ct
    nlag = 2 * J - 1

    def sel(dl):
        d = dl - (J - 1)
        return jnp.where(d > 0, 0, jnp.where(d == 0, 1, 2))

    def a_map(o, dl, c):
        d = dl - (J - 1)
        row = jnp.where(d > 0, d - 1, jnp.where(d == 0, 0, -d - 1))
        dirn = jnp.where(d > 0, 0, 1)
        return (row, (o * 2 + dirn) * nct + c)

    def b_map(o, dl, c):
        d = dl - (J - 1)
        row = jnp.where(d >= 0, d, -d)
        dirn = jnp.where(d >= 0, 0, 1)
        return (row, (o * 2 + dirn) * nct + c)

    return pl.pallas_call(
        _spec_kernel,
        grid=(HY_ORDER, nlag, nct),
        in_specs=[pl.BlockSpec((1, 2 * T, 2 * T), lambda o, dl, c: (sel(dl), 0, 0)),
                  pl.BlockSpec((T, ct), a_map),
                  pl.BlockSpec((T, ct), b_map)],
        out_specs=pl.BlockSpec((1, 1, 2 * T, ct), lambda o, dl, c: (o, dl, 0, c)),
        out_shape=jax.ShapeDtypeStruct((HY_ORDER, nlag, 2 * T, W), F32),
        compiler_params=_cparams(3),
        name="filter_spectra",
    )(mats, h, h)


def _lconv_kernel(u_ref, gate_ref, bias_ref, fwd_ref, inv_ref, h_ref, o_ref, u_scr, y_scr, *, J, ct, GI):
    T = CONV_T
    g = pl.program_id(2)

    @pl.when(g == 0)
    def _():
        for j in range(J):
            u_scr[j] = _bdot(fwd_ref[...], u_ref[0, j * T:(j + 1) * T, :])

    def run_group(i0):
        def tile_body(rt, carry):
            r0 = pl.multiple_of(rt * 8, 8)
            for lt in range(ct // LANES):
                cs = slice(lt * LANES, (lt + 1) * LANES)
                ure = [u_scr[j, pl.ds(r0, 8), cs] for j in range(J)]
                uim = [u_scr[j, pl.ds(T + r0, 8), cs] for j in range(J)]
                for ii in range(GI):
                    yre = jnp.zeros((8, LANES), F32)
                    yim = jnp.zeros((8, LANES), F32)
                    for j in range(J):
                        lag = i0 + ii - j + (J - 1)
                        hre = h_ref[0, lag, pl.ds(r0, 8), cs]
                        him = h_ref[0, lag, pl.ds(T + r0, 8), cs]
                        yre = yre + hre * ure[j] - him * uim[j]
                        yim = yim + hre * uim[j] + him * ure[j]
                    y_scr[ii, pl.ds(r0, 8), cs] = yre
                    y_scr[ii, pl.ds(T + r0, 8), cs] = yim
            return carry

        lax.fori_loop(0, T // 8, tile_body, 0)
        for ii in range(GI):
            y = _bdot(inv_ref[...], y_scr[ii])
            ui = u_ref[0, (i0 + ii) * T:(i0 + ii + 1) * T, :]
            o_ref[0, ii * T:(ii + 1) * T, :] = gate_ref[0, ii * T:(ii + 1) * T, :] * (y + ui * bias_ref[...])

    for gg in range(J // GI):
        pl.when(g == gg)(functools.partial(run_group, gg * GI))


def _long_conv(u_arr, u_blk0, gate_arr, gate_blk0, bias_row, fwd, inv, hspec, order):
    B, L, _ = u_arr.shape
    T = CONV_T
    J = L // T
    ct = 256
    nct = HY_WIDTH // ct
    nlag = 2 * J - 1
    GI = 4
    return pl.pallas_call(
        functools.partial(_lconv_kernel, J=J, ct=ct, GI=GI),
        grid=(nct, B, J // GI),
        in_specs=[pl.BlockSpec((1, L, ct), lambda c, b, i: (b, 0, u_blk0 + c)),
                  pl.BlockSpec((1, GI * T, ct), lambda c, b, i: (b, i, gate_blk0 + c)),
                  pl.BlockSpec((1, ct), lambda c, b, i: (0, c)),
                  pl.BlockSpec(fwd.shape, lambda c, b, i: (0, 0)),
                  pl.BlockSpec(inv.shape, lambda c, b, i: (0, 0)),
                  pl.BlockSpec((1, nlag, 2 * T, ct), lambda c, b, i: (order, 0, 0, c),
                               pipeline_mode=pl.Buffered(1))],
        out_specs=pl.BlockSpec((1, GI * T, ct), lambda c, b, i: (b, i, c)),
        out_shape=jax.ShapeDtypeStruct((B, L, HY_WIDTH), F32),
        scratch_shapes=[pltpu.VMEM((J, 2 * T, ct), F32), pltpu.VMEM((GI, 2 * T, ct), F32)],
        compiler_params=_cparams(3, VMEM_LIMIT_BIG),
        name="hyena_long_conv%d" % order,
    )(u_arr, gate_arr, bias_row, fwd, inv, hspec)


def _merge_kernel(of_ref, ob_ref, g_ref, hy_ref, x0_ref, mod_ref, gn_ref, wout_ref, fg_ref, wr_ref,
                  x1_ref, lg_ref):
    o = of_ref[0] + ob_ref[0]
    g = g_ref[0]
    parts = []
    for h in range(GLA_HEADS):
        oh = o[:, h * GLA_DV:(h + 1) * GLA_DV]
        parts.append(_rms(oh) * gn_ref[...])
    y_gla = jnp.concatenate(parts, axis=-1) * (g * jax.nn.sigmoid(g))
    ycat = jnp.concatenate([y_gla, hy_ref[0]], axis=-1)
    m = _bdot(ycat, wout_ref[...])
    gate = mod_ref[0, :, 2 * D_MODEL:3 * D_MODEL]
    x1 = x0_ref[0] + gate * m
    x1_ref[0] = x1
    shift = mod_ref[0, :, 3 * D_MODEL:4 * D_MODEL]
    scale = mod_ref[0, :, 4 * D_MODEL:5 * D_MODEL]
    hn = ((_rms(x1) * fg_ref[...]) * (1.0 + scale) + shift).astype(BF16)
    lg_ref[0] = lax.dot_general(wr_ref[...], hn, (((1,), (1,)), ((), ())), preferred_element_type=F32)


def _merge(o_f, o_b, g, y_hy, x0, modrows, gla_norm_g, w_out, norm_ffn_g, w_router_t):
    B, L, D = x0.shape
    tm = 512
    tmap = lambda b, i: (b, i, 0)
    full2 = lambda a: pl.BlockSpec(a.shape, lambda b, i: (0, 0))
    return pl.pallas_call(
        _merge_kernel,
        grid=(B, L // tm),
        in_specs=[pl.BlockSpec((1, tm, GLA_V_W), tmap), pl.BlockSpec((1, tm, GLA_V_W), tmap),
                  pl.BlockSpec((1, tm, GLA_V_W), tmap), pl.BlockSpec((1, tm, HY_WIDTH), tmap),
                  pl.BlockSpec((1, tm, D), tmap),
                  pl.BlockSpec((1, 1, modrows.shape[-1]), lambda b, i: (b, 0, 0)),
                  full2(gla_norm_g), full2(w_out), full2(norm_ffn_g), full2(w_router_t)],
        out_specs=[pl.BlockSpec((1, tm, D), tmap), pl.BlockSpec((1, N_EXPERTS, tm), lambda b, i: (b, 0, i))],
        out_shape=[jax.ShapeDtypeStruct((B, L, D), F32), jax.ShapeDtypeStruct((B, N_EXPERTS, L), F32)],
        compiler_params=_cparams(2),
        name="merge_router",
    )(o_f, o_b, g, y_hy, x0, modrows, gla_norm_g, w_out, norm_ffn_g, w_router_t)


def _topk_kernel(lg_ref, idx_ref, wt_ref, pos_scr, *, cap):
    E, L = lg_ref.shape[1], lg_ref.shape[2]
    lg = lg_ref[0]
    mx = jnp.max(lg, axis=0, keepdims=True)
    ex = jnp.exp(lg - mx)
    aff = ex / jnp.sum(ex, axis=0, keepdims=True)

    def count(mask):
        return jnp.sum(mask.astype(jnp.int32), axis=1, keepdims=True)

    def tbody(n, cur):
        cand = cur | lax.shift_left(jnp.int32(1), 30 - n)
        return jnp.where(count(aff >= lax.bitcast_convert_type(cand, F32)) >= cap, cand, cur)

    thr = lax.bitcast_convert_type(lax.fori_loop(0, 31, tbody, jnp.zeros((E, 1), jnp.int32)), F32)
    gt = aff > thr
    eq = aff == thr
    need = cap - count(gt)
    tok = lax.broadcasted_iota(jnp.int32, (E, L), 1)

    def ibody(n, cur):
        cand = cur + lax.shift_left(jnp.int32(1), 12 - n)
        ok = (cand <= L) & (count(eq & (tok < cand)) <= need)
        return jnp.where(ok, cand, cur)

    bound = lax.fori_loop(0, 13, ibody, jnp.zeros((E, 1), jnp.int32))
    sel = gt | (eq & (tok < bound))

    li = lax.broadcasted_iota(jnp.int32, (LANES, LANES), 0)
    lj = lax.broadcasted_iota(jnp.int32, (LANES, LANES), 1)
    ustrict = (li < lj).astype(BF16)
    off = jnp.zeros((E, 1), F32)
    for r in range(L // LANES):
        s_r = sel[:, r * LANES:(r + 1) * LANES]
        sf = s_r.astype(F32)
        pre = jnp.dot(sf.astype(BF16), ustrict, preferred_element_type=F32)
        pos_scr[:, r * LANES:(r + 1) * LANES] = jnp.where(s_r, pre + off, -1.0)
        off = off + jnp.sum(sf, axis=1, keepdims=True)

    slot = lax.broadcasted_iota(jnp.int32, (cap, L), 0).astype(F32)
    tokf = lax.broadcasted_iota(jnp.int32, (8, L), 1)
    row = lax.broadcasted_iota(jnp.int32, (8, L), 0)
    t_hi = (tokf // 64).astype(F32)
    t_lo = (tokf % 64).astype(F32)
    for e in range(E):
        a = aff[e:e + 1, :]
        a1 = a.astype(BF16).astype(F32)
        a2 = (a - a1).astype(BF16).astype(F32)
        a3 = a - a1 - a2
        vals = jnp.where(row == 0, t_hi, jnp.where(row == 1, t_lo, jnp.where(row == 2, a1, jnp.where(
            row == 3, a2, jnp.where(row == 4, a3, 0.0)))))
        onehot = jnp.where(pos_scr[e:e + 1, :] == slot, 1.0, 0.0).astype(BF16)
        res = lax.dot_general(vals.astype(BF16), onehot, (((1,), (1,)), ((), ())), preferred_element_type=F32)
        idx_ref[0, e:e + 1, :] = (res[0:1, :] * 64.0 + res[1:2, :]).astype(jnp.int32)
        wt_ref[0, e:e + 1, :] = res[2:3, :] + res[3:4, :] + res[4:5, :]


def _topk(logits_t, cap):
    B, E, L = logits_t.shape
    return pl.pallas_call(
        functools.partial(_topk_kernel, cap=cap),
        grid=(B,),
        in_specs=[pl.BlockSpec((1, E, L), lambda b: (b, 0, 0))],
        out_specs=[pl.BlockSpec((1, E, cap), lambda b: (b, 0, 0)), pl.BlockSpec((1, E, cap), lambda b: (b, 0, 0))],
        out_shape=[jax.ShapeDtypeStruct((B, E, cap), jnp.int32), jax.ShapeDtypeStruct((B, E, cap), F32)],
        scratch_shapes=[pltpu.VMEM((E, L), F32)],
        compiler_params=_cparams(1),
        name="ec_topk",
    )(logits_t)


def _moe_kernel(*refs, cap, n_exp, b0, first, n_half, hid_chunk):
    idxc_ref, wtc_ref, idxn_ref, idxp_ref, wtp_ref, x1_hbm, mod_ref, ng_ref, wg_ref, wu_ref, wd_ref, fg_ref = refs[:12]
    if first:
        out_hbm, wgo_ref, wuo_ref, wdo_ref = refs[12:16]
        acc, tab, xsp, xs_scr, y_part, y_fin, sem = refs[16:]
    else:
        prev_hbm, out_hbm = refs[12:14]
        acc, tab, xsp, xs_scr, y_part, y_fin, sem = refs[14:]
    bi = pl.program_id(0)
    b = bi + b0
    ob = bi if first else b
    e = pl.program_id(1)
    hf = pl.program_id(2)
    L = acc.shape[0]
    rows = MOE_ROWS
    n_ch = L // rows
    group = 8
    half = D_MODEL // 2

    def chunk(n):
        return pl.ds(n * rows if isinstance(n, int) else pl.multiple_of(n * rows, rows), rows)

    def in_copy(n):
        return pltpu.make_async_copy(x1_hbm.at[b, chunk(n)], acc.at[chunk(n)], sem.at[2 + n])

    def out_copy(n):
        return pltpu.make_async_copy(acc.at[chunk(n)], out_hbm.at[ob, chunk(n)], sem.at[0])

    if not first:
        prev_copy = pltpu.make_async_copy(prev_hbm, out_hbm.at[pl.ds(0, b0)], sem.at[1])

        @pl.when((bi == 0) & (e == 0) & (hf == 0))
        def _():
            prev_copy.start()

    def gather_row(idx_ref, i):
        xsp[pl.ds(i, 1), :] = tab[pl.ds(idx_ref[0, 0, i], 1), :]

    def scatter_group(idx_ref, wt_ref, i0):
        ts = [idx_ref[0, 0, i0 + r] for r in range(group)]
        new = [acc[pl.ds(ts[r], 1), :] + y_fin[pl.ds(i0 + r, 1), :] * wt_ref[0, 0, i0 + r] for r in range(group)]
        for r in range(group):
            acc[pl.ds(ts[r], 1), :] = new[r]

    def ffn_part():
        xs = xs_scr[...]
        y = None
        for f0 in range(0, wg_ref.shape[2], hid_chunk):
            fs = slice(f0, f0 + hid_chunk)
            wg, wu, wd = wg_ref[0, :, fs].astype(BF16), wu_ref[0, :, fs].astype(BF16), wd_ref[0, fs, :].astype(BF16)
            if first:
                wgo_ref[0, :, fs], wuo_ref[0, :, fs], wdo_ref[0, fs, :] = wg, wu, wd
            gte = _bdot(xs, wg)
            up = _bdot(xs, wu)
            part = _bdot(gte * jax.nn.sigmoid(gte) * up, wd)
            y = part if y is None else y + part
        return y

    def unpack_rows():
        xp = xsp[...]
        lo = pltpu.unpack_elementwise(xp, index=0, packed_dtype=BF16, unpacked_dtype=F32)
        hi = pltpu.unpack_elementwise(xp, index=1, packed_dtype=BF16, unpacked_dtype=F32)
        xs_scr[...] = jnp.concatenate([lo, hi], axis=-1).astype(BF16)

    def scatter_prev():
        for i0 in range(0, cap, group):
            scatter_group(idxp_ref, wtp_ref, i0)

    def gather_next():
        for i in range(cap):
            gather_row(idxn_ref, i)

    mod_gate = lambda: mod_ref[0, :, 5 * D_MODEL:6 * D_MODEL]

    @pl.when((e == 0) & (hf == 0))
    def _():
        for n in range(n_ch):
            in_copy(n).start()
        shift = mod_ref[0, :, 3 * D_MODEL:4 * D_MODEL]
        scale = mod_ref[0, :, 4 * D_MODEL:5 * D_MODEL]

        def norm(n, carry):
            in_copy(n).wait()
            r0 = pl.multiple_of(n * rows, rows)
            hn = (_rms(acc[pl.ds(r0, rows), :]) * ng_ref[...]) * (1.0 + scale) + shift
            tab[pl.ds(r0, rows), :] = pltpu.pack_elementwise([hn[:, :half], hn[:, half:]], packed_dtype=BF16)
            return carry

        lax.fori_loop(0, n_ch, norm, 0)

        def gather0(gi, carry):
            i0 = pl.multiple_of(gi * group, group)
            for r in range(group):
                gather_row(idxc_ref, i0 + r)
            return carry

        lax.fori_loop(0, cap // group, gather0, 0)
        unpack_rows()
        y_part[...] = ffn_part()
        if n_half == 1:
            y_fin[...] = y_part[...] * mod_gate()
            gather_next()

    if n_half == 1:
        @pl.when(e > 0)
        def _():
            unpack_rows()
            y_part[...] = ffn_part()
            scatter_prev()
            y_fin[...] = y_part[...] * mod_gate()
            gather_next()
    else:
        @pl.when((e > 0) & (hf == 0))
        def _():
            unpack_rows()
            y_part[...] = ffn_part()
            scatter_prev()

        @pl.when(hf == 1)
        def _():
            y_fin[...] = (y_part[...] + ffn_part()) * mod_gate()
            gather_next()

    @pl.when((e == n_exp - 1) & (hf == n_half - 1))
    def _():
        def scatter_last(gi, carry):
            scatter_group(idxc_ref, wtc_ref, pl.multiple_of(gi * group, group))
            return carry

        lax.fori_loop(0, cap // group, scatter_last, 0)

        def fin(n, carry):
            r0 = pl.multiple_of(n * rows, rows)
            acc[pl.ds(r0, rows), :] = _rms(acc[pl.ds(r0, rows), :]) * fg_ref[...]
            out_copy(n).start()
            return carry

        lax.fori_loop(0, n_ch, fin, 0)
        for n in range(n_ch):
            out_copy(n).wait()
        if not first:
            @pl.when(bi == pl.num_programs(0) - 1)
            def _():
                prev_copy.wait()


def _moe_call(idx3, wts3, x1, modrows, norm_ffn_g, wg, wu, wd, norm_final_g, b0, nb, prev):
    B, L, D = x1.shape
    cap = idx3.shape[2]
    E = idx3.shape[0] // B
    first = prev is None

    def smem_spec(shift):
        return pl.BlockSpec((1, 1, cap), lambda b, e, h: ((b + b0) * E + jnp.clip(e + shift, 0, E - 1), 0, 0),
                            memory_space=pltpu.SMEM)

    n_half = 2 if first else 1
    fc = D_EXPERT // n_half
    w_in = lambda: pl.BlockSpec((1, D, fc), lambda b, e, h: (e, 0, h))
    w_dn = lambda: pl.BlockSpec((1, fc, D), lambda b, e, h: (e, h, 0))
    hbm = lambda: pl.BlockSpec(memory_space=pl.ANY)
    in_specs = [smem_spec(0), smem_spec(0), smem_spec(1), smem_spec(-1), smem_spec(-1), hbm(),
                pl.BlockSpec((1, 1, modrows.shape[-1]), lambda b, e, h: (b + b0, 0, 0)),
                pl.BlockSpec((1, D), lambda b, e, h: (0, 0)),
                w_in(), w_in(), w_dn(),
                pl.BlockSpec((1, D), lambda b, e, h: (0, 0))]
    args = [idx3, wts3, idx3, idx3, wts3, x1, modrows, norm_ffn_g.reshape(1, D), wg, wu, wd,
            norm_final_g.reshape(1, D)]
    if first:
        out_specs = [hbm(), w_in(), w_in(), w_dn()]
        out_shape = [jax.ShapeDtypeStruct((nb, L, D), F32)] + [jax.ShapeDtypeStruct(w.shape, BF16) for w in (wg, wu, wd)]
    else:
        in_specs.append(hbm())
        args.append(prev)
        out_specs = hbm()
        out_shape = jax.ShapeDtypeStruct((b0 + nb, L, D), F32)
    return pl.pallas_call(
        functools.partial(_moe_kernel, cap=cap, n_exp=E, b0=b0, first=first, n_half=n_half, hid_chunk=512),
        grid=(nb, E, n_half),
        in_specs=in_specs, out_specs=out_specs, out_shape=out_shape,
        scratch_shapes=[pltpu.VMEM((L, D), F32), pltpu.VMEM((L, D // 2), jnp.uint32),
                        pltpu.VMEM((cap, D // 2), jnp.uint32), pltpu.VMEM((cap, D), BF16),
                        pltpu.VMEM((cap, D), F32), pltpu.VMEM((cap, D), F32),
                        pltpu.SemaphoreType.DMA((2 + L // MOE_ROWS,))],
        compiler_params=_cparams(3, VMEM_LIMIT_BIG),
        name="ec_moe_first" if first else "ec_moe_rest",
    )(*args)


def _moe(idx, wts, x1, modrows, norm_ffn_g, wg, wu, wd, norm_final_g):
    B = x1.shape[0]
    E, cap = idx.shape[1], idx.shape[2]
    idx3 = idx.reshape(B * E, 1, cap)
    wts3 = wts.reshape(B * E, 1, cap)
    out0, wg_b, wu_b, wd_b = _moe_call(idx3, wts3, x1, modrows, norm_ffn_g, wg, wu, wd, norm_final_g, 0, 1, None)
    if B == 1:
        return out0
    return _moe_call(idx3, wts3, x1, modrows, norm_ffn_g, wg_b, wu_b, wd_b, norm_final_g, 1, B - 1, out0)


def _pos_tables():
    quarter = D_MODEL // 4
    omega = 1.0 / (POS_BASE ** (np.arange(quarter, dtype=np.float64) / quarter))
    n = np.arange(GRID_W, dtype=np.float64)[:, None] * omega[None, :]
    tab = np.concatenate([np.sin(n), np.cos(n)], axis=-1).astype(np.float32)
    return tab


def _filter_features(L):
    t = np.linspace(0.0, 1.0, L, dtype=np.float32).astype(np.float64)[:, None]
    w = 2.0 * math.pi * np.arange(L, dtype=np.float64)[:, None] / L
    f = np.linspace(1e-4, HY_BANDS - 1, HY_BANDS, dtype=np.float32).astype(np.float64)[None, :]
    z = np.concatenate([t, np.cos(f * w), -np.sin(f * w)], axis=-1)
    zp = np.zeros((L, LANES), np.float32)
    zp[:, :HY_EMB] = z.astype(np.float32)
    return zp


def kernel(x, c, ctx, c_ctx, w_ada, b_ada, norm_mix_g, w_in, gla_wa_f, gla_ba_f, gla_wa_b, gla_ba_b, gla_norm_g,
           hy_conv_w, hy_conv_b, hy_w1, hy_b1, hy_w2, hy_b2, hy_w3, hy_freq, hy_bias, w_out, norm_ffn_g, w_router,
           w_gate, w_up, w_down, norm_final_g):
    B, L, D = x.shape
    assert w_ada.shape[0] == 1 and D == D_MODEL and L % (GRID_W * 8) == 0
    l = 0
    cap = EC_FACTOR * L // N_EXPERTS

    cc = jnp.zeros((8, D), F32).at[:B].set(c).at[B].set(c_ctx)
    modrows = _adaln(cc, w_ada[l], b_ada[l]).reshape(8, 1, N_MOD * D)

    s = (GLA_QK_W, 2 * GLA_QK_W, 2 * GLA_QK_W + GLA_V_W, 2 * GLA_QK_W + 2 * GLA_V_W,
         2 * GLA_QK_W + 2 * GLA_V_W + GLA_RANK, 2 * GLA_QK_W + 2 * GLA_V_W + 2 * GLA_RANK)
    w = w_in[l]
    w_q, w_k, w_v, w_g, w_af, w_ab, w_hy = (w[:, :s[0]], w[:, s[0]:s[1]], w[:, s[1]:s[2]], w[:, s[2]:s[3]],
                                            w[:, s[3]:s[4]], w[:, s[4]:s[5]], w[:, s[5]:])
    w_a = jnp.concatenate([w_af, w_ab, jnp.zeros((D, LANES - 2 * GLA_RANK), F32)], axis=1)
    w_x = jnp.concatenate([w_q, w_k, w_v, w_g, w_hy, w_a], axis=1).astype(BF16)
    w_c = jnp.concatenate([w_k, w_v, w_a], axis=1).astype(BF16)
    wa = jnp.zeros((LANES, 2 * GLA_QK_W), F32)
    wa = wa.at[:GLA_RANK, :GLA_QK_W].set(gla_wa_f[l]).at[GLA_RANK:2 * GLA_RANK, GLA_QK_W:].set(gla_wa_b[l])
    wa = wa.astype(BF16)
    ba = jnp.concatenate([gla_ba_f[l], gla_ba_b[l]]).reshape(1, 2 * GLA_QK_W)

    tab = jnp.asarray(_pos_tables())
    x0, q, k, v, g, u, la_f, la_b = _project(
        x, modrows, None, norm_mix_g[l], w_x, wa, ba, (tab, tab),
        (GLA_QK_W, GLA_QK_W, GLA_V_W, GLA_V_W, 3 * HY_WIDTH), True, 512,
        conv=(hy_conv_w[l], hy_conv_b[l].reshape(1, -1)))
    k_c, v_c, laf_c, lab_c = _project(ctx, modrows, B, norm_mix_g[l], w_c, wa, ba, None,
                                      (GLA_QK_W, GLA_V_W), False, ctx.shape[1])

    s_zero = jnp.zeros((B, GLA_V_W, GLA_QK_W), F32)
    nb_c = ctx.shape[1] // GLA_CHUNK
    _, s_f = _gla_pass(None, k_c, v_c, laf_c, s_zero, False, nb_c)
    _, s_b = _gla_pass(None, k_c, v_c, lab_c, s_zero, True, nb_c)
    o_f, _ = _gla_pass(q, k, v, la_f, s_f, False, 16)
    o_b, _ = _gla_pass(q, k, v, la_b, s_b, True, 16)

    w1p = jnp.zeros((LANES, HY_HIDDEN), F32).at[:HY_EMB].set(hy_w1[l])
    deltas = np.abs(np.linspace(HY_MIN_DECAY, HY_MAX_DECAY, HY_WIDTH, dtype=np.float32))
    deltas4 = jnp.asarray(np.tile(deltas, HY_ORDER * 2).reshape(1, -1))
    h = _hyena_filters(jnp.asarray(_filter_features(L)), w1p, hy_b1[l].reshape(1, -1), hy_w2[l],
                       hy_b2[l].reshape(1, -1), hy_w3[l], hy_freq[l].reshape(1, -1), deltas4)
    fwd, inv, mats = _dft_tables()
    hspec = _filter_spectra(h, mats)

    nct = HY_WIDTH // 256
    z1 = _long_conv(u, 0, u, nct, hy_bias[l][0:1], fwd, inv, hspec, 0)
    y_hy = _long_conv(z1, 0, u, 2 * nct, hy_bias[l][1:2], fwd, inv, hspec, 1)

    x1, logits_t = _merge(o_f, o_b, g, y_hy, x0, modrows, gla_norm_g[l].reshape(1, GLA_DV),
                                   w_out[l].astype(BF16), norm_ffn_g[l].reshape(1, D),
                                   w_router[l].T.astype(BF16))
    idx, wts = _topk(logits_t, cap)
    return _moe(idx, wts, x1, modrows, norm_ffn_g[l], w_gate[l], w_up[l], w_down[l], norm_final_g)
```

```python
import functools
import math

import numpy as np
import jax
import jax.numpy as jnp
from jax import lax
from jax.experimental import pallas as pl
from jax.experimental.pallas import tpu as pltpu

F32 = jnp.float32
BF16 = jnp.bfloat16

D_MODEL = 1024
GRID_W = 64
EPS = 1e-6
POS_BASE = 10000.0
N_MOD = 6

GLA_HEADS = 4
GLA_DK = 64
GLA_DV = 128
GLA_RANK = 16
GLA_GATE_NORM = 16.0
GLA_CHUNK = 64
GLA_QK_W = GLA_HEADS * GLA_DK
GLA_V_W = GLA_HEADS * GLA_DV

HY_WIDTH = D_MODEL - GLA_V_W
HY_ORDER = 2
HY_BANDS = 16
HY_EMB = 2 * HY_BANDS + 1
HY_HIDDEN = 64
HY_SHORT = 3
HY_MIN_DECAY = math.log(1e-2) / 1.5
HY_MAX_DECAY = math.log(1e-2) / 0.3

N_EXPERTS = 16
EC_FACTOR = 2
D_EXPERT = 1024

VMEM_LIMIT_BIG = 56 * 1024 * 1024
VMEM_LIMIT_MID = 40 * 1024 * 1024
LANES = 128

CONV_T = 512
MOE_ROWS = 512


def _cparams(n_axes, vmem=VMEM_LIMIT_MID):
    return pltpu.CompilerParams(dimension_semantics=("arbitrary",) * n_axes, vmem_limit_bytes=vmem)


def _bdot(a, b):
    return jnp.dot(a.astype(BF16), b.astype(BF16), preferred_element_type=F32)


def _tree_sum(terms):
    while len(terms) > 1:
        terms = [a + b for a, b in zip(terms[::2], terms[1::2])] + ([terms[-1]] if len(terms) % 2 else [])
    return terms[0]


def _rms(x):
    return x * lax.rsqrt(jnp.mean(x * x, axis=-1, keepdims=True) + EPS)


def _mod_kernel(c_ref, w_ref, b_ref, o_ref):
    c = c_ref[...]
    s = c * jax.nn.sigmoid(c)
    o_ref[...] = _bdot(s, w_ref[...]) + b_ref[...]


def _adaln(cc, w_ada, b_ada):
    rows, d = cc.shape
    n = w_ada.shape[1]
    tn = 1536
    return pl.pallas_call(
        _mod_kernel,
        grid=(n // tn,),
        in_specs=[pl.BlockSpec((rows, d), lambda j: (0, 0)),
                  pl.BlockSpec((d, tn), lambda j: (0, j)),
                  pl.BlockSpec((1, tn), lambda j: (0, j))],
        out_specs=pl.BlockSpec((rows, tn), lambda j: (0, j)),
        out_shape=jax.ShapeDtypeStruct((rows, n), F32),
        compiler_params=_cparams(1),
        name="adaln",
    )(cc, w_ada, b_ada.reshape(1, n))


def _proj_kernel(*refs, with_pos, tm, widths, q_scale_first):
    if with_pos:
        x_ref, xp_ref, xn_ref, rt_ref, ct_ref, cw_ref, cb_ref, mod_ref, g_ref, w_ref, wa_ref, ba_ref = refs[:12]
        outs = refs[12:]
    else:
        x_ref, mod_ref, g_ref, w_ref, wa_ref, ba_ref = refs[:6]
        outs = refs[6:]
    x = x_ref[0]
    if with_pos:
        i = pl.program_id(0)
        n_tiles = pl.num_programs(0)
        rows_per_tile = tm // GRID_W
        n_rows = rt_ref.shape[0]
        halo = xp_ref.shape[1]
        r0 = pl.multiple_of(i * rows_per_tile, rows_per_tile)
        rt = rt_ref[pl.ds(r0, rows_per_tile), :]
        half = D_MODEL // 2
        x3 = x.reshape(rows_per_tile, GRID_W, D_MODEL)
        x3 = jnp.concatenate([x3[:, :, :half] + rt[:, None, :], x3[:, :, half:] + ct_ref[...][None]], axis=-1)
        x = x3.reshape(tm, D_MODEL)
        outs[0][0] = x
        outs = outs[1:]
        rt_prev = rt_ref[pl.ds(jnp.maximum(r0 - 1, 0), 1), :]
        rt_next = rt_ref[pl.ds(jnp.minimum(r0 + rows_per_tile, n_rows - 1), 1), :]
        x_prev = xp_ref[0] + jnp.concatenate(
            [jnp.broadcast_to(rt_prev, (halo, half)), ct_ref[GRID_W - halo:GRID_W, :]], axis=-1)
        x_next = xn_ref[0] + jnp.concatenate([jnp.broadcast_to(rt_next, (halo, half)), ct_ref[0:halo, :]], axis=-1)
        x = jnp.concatenate([x, x_prev, x_next], axis=0)
    shift = mod_ref[0, :, 0:D_MODEL]
    scale = mod_ref[0, :, D_MODEL:2 * D_MODEL]
    hn = (_rms(x) * g_ref[...]) * (1.0 + scale) + shift
    p_all = _bdot(hn, w_ref[...])
    p = p_all[:tm]
    off = 0
    for n, (o_ref, w) in enumerate(zip(outs[:-2], widths)):
        blk = p[:, off:off + w]
        if q_scale_first and n == 0:
            blk = blk * (GLA_DK ** -0.5)
        if with_pos and n == len(widths) - 1:
            before = jnp.where(i == 0, 0.0, p_all[tm + halo - 1:tm + halo, off:off + w])
            after = jnp.where(i == n_tiles - 1, 0.0, p_all[tm + halo:tm + halo + 1, off:off + w])
            t = lax.broadcasted_iota(jnp.int32, blk.shape, 0)
            up = jnp.where(t == 0, before, pltpu.roll(blk, 1, 0))
            dn = jnp.where(t == tm - 1, after, pltpu.roll(blk, tm - 1, 0))
            blk = up * cw_ref[0:1, :] + blk * cw_ref[1:2, :] + dn * cw_ref[2:3, :] + cb_ref[...]
        o_ref[0] = blk
        off += w
    a = p[:, off:off + LANES]
    z = _bdot(a, wa_ref[...]) + ba_ref[...]
    la = jax.nn.log_sigmoid(z) / GLA_GATE_NORM
    outs[-2][0] = la[:, :GLA_QK_W]
    outs[-1][0] = la[:, GLA_QK_W:]


def _project(x, modrows, mod_row0, norm_g, w_cols, wa, ba, tabs, widths, q_scale_first, tm, conv=None):
    B, L, D = x.shape
    with_pos = tabs is not None
    ncols = w_cols.shape[1]
    in_specs = [pl.BlockSpec((1, tm, D), lambda i, b: (b, i, 0))]
    args = [x]
    if with_pos:
        rt, ct = tabs
        cw, cb = conv
        halo = 8
        per = tm // halo
        in_specs += [pl.BlockSpec((1, halo, D), lambda i, b: (b, jnp.maximum(i * per - 1, 0), 0)),
                     pl.BlockSpec((1, halo, D), lambda i, b: (b, jnp.minimum((i + 1) * per, L // halo - 1), 0)),
                     pl.BlockSpec(rt.shape, lambda i, b: (0, 0)), pl.BlockSpec(ct.shape, lambda i, b: (0, 0)),
                     pl.BlockSpec(cw.shape, lambda i, b: (0, 0)), pl.BlockSpec(cb.shape, lambda i, b: (0, 0))]
        args += [x, x, rt, ct, cw, cb]
    if mod_row0 is None:
        mod_map = lambda i, b: (b, 0, 0)
    else:
        mod_map = lambda i, b: (mod_row0, 0, 0)
    in_specs += [pl.BlockSpec((1, 1, modrows.shape[-1]), mod_map),
                 pl.BlockSpec((1, D), lambda i, b: (0, 0)),
                 pl.BlockSpec((D, ncols), lambda i, b: (0, 0)),
                 pl.BlockSpec(wa.shape, lambda i, b: (0, 0)),
                 pl.BlockSpec(ba.shape, lambda i, b: (0, 0))]
    args += [modrows, norm_g.reshape(1, D), w_cols, wa, ba]
    out_w = ([D] if with_pos else []) + list(widths) + [GLA_QK_W, GLA_QK_W]
    out_specs = [pl.BlockSpec((1, tm, w), lambda i, b: (b, i, 0)) for w in out_w]
    out_shape = [jax.ShapeDtypeStruct((B, L, w), F32) for w in out_w]
    return pl.pallas_call(
        functools.partial(_proj_kernel, with_pos=with_pos, tm=tm, widths=tuple(widths), q_scale_first=q_scale_first),
        grid=(L // tm, B),
        in_specs=in_specs, out_specs=out_specs, out_shape=out_shape,
        compiler_params=_cparams(2, VMEM_LIMIT_BIG),
        name="project_x" if with_pos else "project_ctx",
    )(*args)


def _gla_kernel(*refs, reverse, with_output, nb):
    if with_output:
        q_ref, k_ref, v_ref, la_ref, s0_ref, o_ref, sfin_ref, s_scr = refs
    else:
        k_ref, v_ref, la_ref, s0_ref, sfin_ref, s_scr = refs
    C = GLA_CHUNK

    @pl.when(pl.program_id(1) == 0)
    def _():
        s_scr[...] = s0_ref[0]

    H = GLA_HEADS
    nt = (((1,), (1,)), ((), ()))
    tn = (((0,), (0,)), ((), ()))
    row = lax.broadcasted_iota(jnp.int32, (C, GLA_QK_W), 0)
    sc_r = lax.broadcasted_iota(jnp.int32, (C, H * C), 0)
    sc_s = lax.broadcasted_iota(jnp.int32, (C, H * C), 1) % C
    keep = (sc_r <= sc_s) if reverse else (sc_r >= sc_s)
    def head_mask(shape, rdiv, cdiv):
        return (lax.broadcasted_iota(jnp.int32, shape, 0) // rdiv) == (lax.broadcasted_iota(jnp.int32, shape, 1) // cdiv)
    m_k = head_mask((H * C, GLA_QK_W), C, GLA_DK)
    m_v = head_mask((H * C, GLA_V_W), C, GLA_DV)
    m_s = head_mask((GLA_V_W, GLA_QK_W), GLA_DV, GLA_DK)

    def prefix_sum(x):
        s = 1
        while s < C:
            if reverse:
                x = x + jnp.where(row < C - s, pltpu.roll(x, C - s, 0), 0.0)
            else:
                x = x + jnp.where(row >= s, pltpu.roll(x, s, 0), 0.0)
            s *= 2
        return x

    order = range(nb - 1, -1, -1) if reverse else range(nb)
    for ci in order:
        sl = slice(ci * C, (ci + 1) * C)
        k = k_ref[0, sl, :]
        v = v_ref[0, sl, :]
        b = prefix_sum(la_ref[0, sl, :])
        last = 0 if reverse else C - 1
        b_tot = b[last:last + 1, :]
        kdec = (k * jnp.exp(b_tot - b)).astype(BF16)
        vb = v.astype(BF16)
        upd_t = jnp.where(m_s, lax.dot_general(vb, kdec, tn, preferred_element_type=F32), 0.0)
        st = s_scr[...]
        if with_output:
            q = q_ref[0, sl, :]
            mid = C // 2 - 1 if reverse else C // 2
            ref = b[mid:mid + 1, :]
            qs = (q * jnp.exp(b - ref)).astype(BF16)
            ks = (k * jnp.exp(ref - b)).astype(BF16)
            qb = (q * jnp.exp(b)).astype(BF16)
            o_inter = lax.dot_general(qb, st.astype(BF16), nt, preferred_element_type=F32)
            ks4 = jnp.where(m_k, jnp.concatenate([ks] * H, axis=0), jnp.zeros((), BF16))
            sc = lax.dot_general(qs, ks4, nt, preferred_element_type=F32)
            sc = jnp.where(keep, sc, 0.0).astype(BF16)
            v4 = jnp.where(m_v, jnp.concatenate([vb] * H, axis=0), jnp.zeros((), BF16))
            o_ref[0, sl, :] = o_inter + jnp.dot(sc, v4, preferred_element_type=F32)
        s_scr[...] = st * jnp.exp(b_tot) + upd_t
    sfin_ref[0] = s_scr[...]


def _gla_pass(q, k, v, la, s0, reverse, nb):
    B, L, _ = k.shape
    with_output = q is not None
    tm = nb * GLA_CHUNK
    nblk = L // tm
    if reverse:
        tmap = lambda b, i: (b, nblk - 1 - i, 0)
    else:
        tmap = lambda b, i: (b, i, 0)
    smap = lambda b, i: (b, 0, 0)
    in_specs, args = [], []
    if with_output:
        in_specs.append(pl.BlockSpec((1, tm, GLA_QK_W), tmap))
        args.append(q)
    in_specs += [pl.BlockSpec((1, tm, GLA_QK_W), tmap), pl.BlockSpec((1, tm, GLA_V_W), tmap),
                 pl.BlockSpec((1, tm, GLA_QK_W), tmap), pl.BlockSpec((1, GLA_V_W, GLA_QK_W), smap)]
    args += [k, v, la, s0]
    out_specs, out_shape = [], []
    if with_output:
        out_specs.append(pl.BlockSpec((1, tm, GLA_V_W), tmap))
        out_shape.append(jax.ShapeDtypeStruct((B, L, GLA_V_W), F32))
    out_specs.append(pl.BlockSpec((1, GLA_V_W, GLA_QK_W), smap))
    out_shape.append(jax.ShapeDtypeStruct((B, GLA_V_W, GLA_QK_W), F32))
    res = pl.pallas_call(
        functools.partial(_gla_kernel, reverse=reverse, with_output=with_output, nb=nb),
        grid=(B, nblk),
        in_specs=in_specs, out_specs=out_specs, out_shape=out_shape,
        scratch_shapes=[pltpu.VMEM((GLA_V_W, GLA_QK_W), F32)],
        compiler_params=_cparams(2),
        name=("gla_" + ("bwd" if reverse else "fwd") + ("" if with_output else "_state")),
    )(*args)
    return (res[0], res[1]) if with_output else (None, res[0])


def _filter_kernel(z_ref, w1_ref, b1_ref, w2_ref, b2_ref, w3_ref, fr_ref, dl_ref, o_ref, *, tm, L):
    hi = lax.Precision.HIGHEST
    fr = fr_ref[...]
    h1 = jnp.sin(fr * (jnp.dot(z_ref[...], w1_ref[...], precision=hi, preferred_element_type=F32) + b1_ref[...]))
    h2 = jnp.sin(fr * (jnp.dot(h1, w2_ref[...], precision=hi, preferred_element_type=F32) + b2_ref[...]))
    h = _bdot(h2, w3_ref[...])
    t = z_ref[:, 0:1]
    o_ref[...] = h * jnp.exp(-t * dl_ref[...])


def _hyena_filters(zfeat, w1p, b1, w2, b2, w3, freq, deltas4):
    L = zfeat.shape[0]
    tm = 512
    n = w3.shape[1]
    full = lambda a: pl.BlockSpec(a.shape, lambda i: (0,) * a.ndim)
    return pl.pallas_call(
        functools.partial(_filter_kernel, tm=tm, L=L),
        grid=(L // tm,),
        in_specs=[pl.BlockSpec((tm, zfeat.shape[1]), lambda i: (i, 0)), full(w1p), full(b1), full(w2), full(b2),
                  full(w3), full(freq), full(deltas4)],
        out_specs=pl.BlockSpec((tm, n), lambda i: (i, 0)),
        out_shape=jax.ShapeDtypeStruct((L, n), F32),
        compiler_params=_cparams(1),
        name="hyena_filters",
    )(zfeat, w1p, b1, w2, b2, w3, freq, deltas4)


def _dft_tables():
    T = CONV_T
    f = np.arange(T, dtype=np.float64)[:, None] + 0.5
    n = np.arange(T, dtype=np.float64)[None, :]
    th = 2.0 * np.pi * f * n / (2 * T)
    thn = 2.0 * np.pi * f * (n - T) / (2 * T)
    fwd = np.concatenate([np.cos(th), -np.sin(th)], axis=0)
    inv = np.concatenate([np.cos(th).T, -np.sin(th).T], axis=1) / T
    nz = np.ones((1, T)); nz[0, 0] = 0.0
    P = np.concatenate([np.cos(th), -np.sin(th)], axis=0)
    Pc = np.concatenate([np.cos(th), np.sin(th)], axis=0)
    N = np.concatenate([np.cos(thn), -np.sin(thn)], axis=0) * nz
    Nc = np.concatenate([np.cos(thn), np.sin(thn)], axis=0) * nz
    m_pos = np.concatenate([N, P], axis=1)
    m_zero = np.concatenate([Pc * nz, P], axis=1)
    m_neg = np.concatenate([Nc, Pc], axis=1)
    mats = np.stack([m_pos, m_zero, m_neg], axis=0)
    as_bf16 = lambda a: jnp.asarray(a, dtype=F32).astype(BF16)
    return as_bf16(fwd), as_bf16(inv), as_bf16(mats)


def _spec_kernel(m_ref, a_ref, b_ref, o_ref):
    T = CONV_T
    o_ref[0, 0] = _bdot(m_ref[0, :, :T], a_ref[...]) + _bdot(m_ref[0, :, T:], b_ref[...])


def _filter_spectra(h, mats):
    L = h.shape[0]
    T = CONV_T
    J = L // T
    W = HY_WIDTH
    ct = W
    nct = W // ct
    nlag = 2 * J - 1

    def sel(dl):
        d = dl - (J - 1)
        return jnp.where(d > 0, 0, jnp.where(d == 0, 1, 2))

    def a_map(o, dl, c):
        d = dl - (J - 1)
        row = jnp.where(d > 0, d - 1, jnp.where(d == 0, 0, -d - 1))
        dirn = jnp.where(d > 0, 0, 1)
        return (row, (o * 2 + dirn) * nct + c)

    def b_map(o, dl, c):
        d = dl - (J - 1)
        row = jnp.where(d >= 0, d, -d)
        dirn = jnp.where(d >= 0, 0, 1)
        return (row, (o * 2 + dirn) * nct + c)

    return pl.pallas_call(
        _spec_kernel,
        grid=(HY_ORDER, nlag, nct),
        in_specs=[pl.BlockSpec((1, 2 * T, 2 * T), lambda o, dl, c: (sel(dl), 0, 0)),
                  pl.BlockSpec((T, ct), a_map),
                  pl.BlockSpec((T, ct), b_map)],
        out_specs=pl.BlockSpec((1, 1, 2 * T, ct), lambda o, dl, c: (o, dl, 0, c)),
        out_shape=jax.ShapeDtypeStruct((HY_ORDER, nlag, 2 * T, W), F32),
        compiler_params=_cparams(3),
        name="filter_spectra",
    )(mats, h, h)


def _lconv_kernel(u_ref, gate_ref, bias_ref, fwd_ref, inv_ref, h_ref, o_ref, u_scr, y_scr, *, J, ct, GI):
    T = CONV_T
    g = pl.program_id(2)

    @pl.when(g == 0)
    def _():
        for j in range(J):
            u_scr[j] = _bdot(fwd_ref[...], u_ref[0, j * T:(j + 1) * T, :])

    def run_group(i0):
        def tile_body(rt, carry):
            r0 = pl.multiple_of(rt * 8, 8)
            for lt in range(ct // LANES):
                cs = slice(lt * LANES, (lt + 1) * LANES)
                ure = [u_scr[j, pl.ds(r0, 8), cs] for j in range(J)]
                uim = [u_scr[j, pl.ds(T + r0, 8), cs] for j in range(J)]
                for ii in range(GI):
                    re_terms, im_terms = [], []
                    for j in range(J):
                        lag = i0 + ii - j + (J - 1)
                        hre = h_ref[0, lag, pl.ds(r0, 8), cs]
                        him = h_ref[0, lag, pl.ds(T + r0, 8), cs]
                        re_terms.append(hre * ure[j] - him * uim[j])
                        im_terms.append(hre * uim[j] + him * ure[j])
                    y_scr[ii, pl.ds(r0, 8), cs] = _tree_sum(re_terms)
                    y_scr[ii, pl.ds(T + r0, 8), cs] = _tree_sum(im_terms)
            return carry

        lax.fori_loop(0, T // 8, tile_body, 0)
        for ii in range(GI):
            y = _bdot(inv_ref[...], y_scr[ii])
            ui = u_ref[0, (i0 + ii) * T:(i0 + ii + 1) * T, :]
            o_ref[0, ii * T:(ii + 1) * T, :] = gate_ref[0, ii * T:(ii + 1) * T, :] * (y + ui * bias_ref[...])

    for gg in range(J // GI):
        pl.when(g == gg)(functools.partial(run_group, gg * GI))


def _long_conv(u_arr, u_blk0, gate_arr, gate_blk0, bias_row, fwd, inv, hspec, order):
    B, L, _ = u_arr.shape
    T = CONV_T
    J = L // T
    ct = 256
    nct = HY_WIDTH // ct
    nlag = 2 * J - 1
    GI = 4
    return pl.pallas_call(
        functools.partial(_lconv_kernel, J=J, ct=ct, GI=GI),
        grid=(nct, B, J // GI),
        in_specs=[pl.BlockSpec((1, L, ct), lambda c, b, i: (b, 0, u_blk0 + c)),
                  pl.BlockSpec((1, GI * T, ct), lambda c, b, i: (b, i, gate_blk0 + c)),
                  pl.BlockSpec((1, ct), lambda c, b, i: (0, c)),
                  pl.BlockSpec(fwd.shape, lambda c, b, i: (0, 0)),
                  pl.BlockSpec(inv.shape, lambda c, b, i: (0, 0)),
                  pl.BlockSpec((1, nlag, 2 * T, ct), lambda c, b, i: (order, 0, 0, c),
                               pipeline_mode=pl.Buffered(1))],
        out_specs=pl.BlockSpec((1, GI * T, ct), lambda c, b, i: (b, i, c)),
        out_shape=jax.ShapeDtypeStruct((B, L, HY_WIDTH), F32),
        scratch_shapes=[pltpu.VMEM((J, 2 * T, ct), F32), pltpu.VMEM((GI, 2 * T, ct), F32)],
        compiler_params=_cparams(3, VMEM_LIMIT_BIG),
        name="hyena_long_conv%d" % order,
    )(u_arr, gate_arr, bias_row, fwd, inv, hspec)


def _merge_kernel(of_ref, ob_ref, g_ref, hy_ref, x0_ref, mod_ref, gn_ref, wout_ref, fg_ref, wr_ref,
                  x1_ref, lg_ref):
    o = of_ref[0] + ob_ref[0]
    g = g_ref[0]
    parts = []
    for h in range(GLA_HEADS):
        oh = o[:, h * GLA_DV:(h + 1) * GLA_DV]
        parts.append(_rms(oh) * gn_ref[...])
    y_gla = jnp.concatenate(parts, axis=-1) * (g * jax.nn.sigmoid(g))
    ycat = jnp.concatenate([y_gla, hy_ref[0]], axis=-1)
    m = _bdot(ycat, wout_ref[...])
    gate = mod_ref[0, :, 2 * D_MODEL:3 * D_MODEL]
    x1 = x0_ref[0] + gate * m
    x1_ref[0] = x1
    shift = mod_ref[0, :, 3 * D_MODEL:4 * D_MODEL]
    scale = mod_ref[0, :, 4 * D_MODEL:5 * D_MODEL]
    hn = ((_rms(x1) * fg_ref[...]) * (1.0 + scale) + shift).astype(BF16)
    lg_ref[0] = lax.dot_general(wr_ref[...], hn, (((1,), (1,)), ((), ())), preferred_element_type=F32)


def _merge(o_f, o_b, g, y_hy, x0, modrows, gla_norm_g, w_out, norm_ffn_g, w_router_t):
    B, L, D = x0.shape
    tm = 512
    tmap = lambda b, i: (b, i, 0)
    full2 = lambda a: pl.BlockSpec(a.shape, lambda b, i: (0, 0))
    return pl.pallas_call(
        _merge_kernel,
        grid=(B, L // tm),
        in_specs=[pl.BlockSpec((1, tm, GLA_V_W), tmap), pl.BlockSpec((1, tm, GLA_V_W), tmap),
                  pl.BlockSpec((1, tm, GLA_V_W), tmap), pl.BlockSpec((1, tm, HY_WIDTH), tmap),
                  pl.BlockSpec((1, tm, D), tmap),
                  pl.BlockSpec((1, 1, modrows.shape[-1]), lambda b, i: (b, 0, 0)),
                  full2(gla_norm_g), full2(w_out), full2(norm_ffn_g), full2(w_router_t)],
        out_specs=[pl.BlockSpec((1, tm, D), tmap), pl.BlockSpec((1, N_EXPERTS, tm), lambda b, i: (b, 0, i))],
        out_shape=[jax.ShapeDtypeStruct((B, L, D), F32), jax.ShapeDtypeStruct((B, N_EXPERTS, L), F32)],
        compiler_params=_cparams(2),
        name="merge_router",
    )(o_f, o_b, g, y_hy, x0, modrows, gla_norm_g, w_out, norm_ffn_g, w_router_t)


def _topk_kernel(lg_ref, idx_ref, wt_ref, pos_scr, *, cap):
    E, L = lg_ref.shape[1], lg_ref.shape[2]
    lg = lg_ref[0]
    mx = jnp.max(lg, axis=0, keepdims=True)
    ex = jnp.exp(lg - mx)
    aff = ex / jnp.sum(ex, axis=0, keepdims=True)

    def count(mask):
        return jnp.sum(mask.astype(jnp.int32), axis=1, keepdims=True)

    def tbody(n, cur):
        cand = cur | lax.shift_left(jnp.int32(1), 30 - n)
        return jnp.where(count(aff >= lax.bitcast_convert_type(cand, F32)) >= cap, cand, cur)

    thr = lax.bitcast_convert_type(lax.fori_loop(0, 31, tbody, jnp.zeros((E, 1), jnp.int32)), F32)
    gt = aff > thr
    eq = aff == thr
    need = cap - count(gt)
    tok = lax.broadcasted_iota(jnp.int32, (E, L), 1)

    def ibody(n, cur):
        cand = cur + lax.shift_left(jnp.int32(1), 12 - n)
        ok = (cand <= L) & (count(eq & (tok < cand)) <= need)
        return jnp.where(ok, cand, cur)

    bound = lax.fori_loop(0, 13, ibody, jnp.zeros((E, 1), jnp.int32))
    sel = gt | (eq & (tok < bound))

    li = lax.broadcasted_iota(jnp.int32, (LANES, LANES), 0)
    lj = lax.broadcasted_iota(jnp.int32, (LANES, LANES), 1)
    ustrict = (li < lj).astype(BF16)
    off = jnp.zeros((E, 1), F32)
    for r in range(L // LANES):
        s_r = sel[:, r * LANES:(r + 1) * LANES]
        sf = s_r.astype(F32)
        pre = jnp.dot(sf.astype(BF16), ustrict, preferred_element_type=F32)
        pos_scr[:, r * LANES:(r + 1) * LANES] = jnp.where(s_r, pre + off, -1.0)
        off = off + jnp.sum(sf, axis=1, keepdims=True)

    slot = lax.broadcasted_iota(jnp.int32, (cap, L), 0).astype(F32)
    tokf = lax.broadcasted_iota(jnp.int32, (8, L), 1)
    row = lax.broadcasted_iota(jnp.int32, (8, L), 0)
    t_hi = (tokf // 64).astype(F32)
    t_lo = (tokf % 64).astype(F32)
    for e in range(E):
        a = aff[e:e + 1, :]
        a1 = a.astype(BF16).astype(F32)
        a2 = (a - a1).astype(BF16).astype(F32)
        a3 = a - a1 - a2
        vals = jnp.where(row == 0, t_hi, jnp.where(row == 1, t_lo, jnp.where(row == 2, a1, jnp.where(
            row == 3, a2, jnp.where(row == 4, a3, 0.0)))))
        onehot = jnp.where(pos_scr[e:e + 1, :] == slot, 1.0, 0.0).astype(BF16)
        res = lax.dot_general(vals.astype(BF16), onehot, (((1,), (1,)), ((), ())), preferred_element_type=F32)
        idx_ref[0, e:e + 1, :] = (res[0:1, :] * 64.0 + res[1:2, :]).astype(jnp.int32)
        wt_ref[0, e:e + 1, :] = res[2:3, :] + res[3:4, :] + res[4:5, :]


def _topk(logits_t, cap):
    B, E, L = logits_t.shape
    return pl.pallas_call(
        functools.partial(_topk_kernel, cap=cap),
        grid=(B,),
        in_specs=[pl.BlockSpec((1, E, L), lambda b: (b, 0, 0))],
        out_specs=[pl.BlockSpec((1, E, cap), lambda b: (b, 0, 0)), pl.BlockSpec((1, E, cap), lambda b: (b, 0, 0))],
        out_shape=[jax.ShapeDtypeStruct((B, E, cap), jnp.int32), jax.ShapeDtypeStruct((B, E, cap), F32)],
        scratch_shapes=[pltpu.VMEM((E, L), F32)],
        compiler_params=_cparams(1),
        name="ec_topk",
    )(logits_t)


def _moe_kernel(idxc_ref, wtc_ref, idxn_ref, idxp_ref, wtp_ref, x1_hbm, mod_ref, ng_ref, wg_ref, wu_ref, wd_ref,
                fg_ref, out_hbm, acc, tab, xsp, xs_scr, y_part, y_fin, sem, *, cap, n_exp):
    b = pl.program_id(0)
    e = pl.program_id(1)
    hf = pl.program_id(2)
    L = acc.shape[0]
    rows = MOE_ROWS
    n_ch = L // rows
    group = 8
    half = D_MODEL // 2

    def chunk(n):
        return pl.ds(n * rows if isinstance(n, int) else pl.multiple_of(n * rows, rows), rows)

    def in_copy(n):
        return pltpu.make_async_copy(x1_hbm.at[b, chunk(n)], acc.at[chunk(n)], sem.at[1 + n])

    def out_copy(n):
        return pltpu.make_async_copy(acc.at[chunk(n)], out_hbm.at[b, chunk(n)], sem.at[0])

    def gather_row(idx_ref, i):
        xsp[pl.ds(i, 1), :] = tab[pl.ds(idx_ref[0, 0, i], 1), :]

    def scatter_group(idx_ref, wt_ref, i0):
        ts = [idx_ref[0, 0, i0 + r] for r in range(group)]
        new = [acc[pl.ds(ts[r], 1), :] + y_fin[pl.ds(i0 + r, 1), :] * wt_ref[0, 0, i0 + r] for r in range(group)]
        for r in range(group):
            acc[pl.ds(ts[r], 1), :] = new[r]

    def ffn_part():
        xs = xs_scr[...]
        y = None
        width = wg_ref.shape[2]
        for f0 in range(0, width, width // 2):
            fs = slice(f0, f0 + width // 2)
            gte = _bdot(xs, wg_ref[0, :, fs])
            up = _bdot(xs, wu_ref[0, :, fs])
            part = _bdot(gte * jax.nn.sigmoid(gte) * up, wd_ref[0, fs, :])
            y = part if y is None else y + part
        return y

    def unpack_rows():
        xp = xsp[...]
        lo = pltpu.unpack_elementwise(xp, index=0, packed_dtype=BF16, unpacked_dtype=F32)
        hi = pltpu.unpack_elementwise(xp, index=1, packed_dtype=BF16, unpacked_dtype=F32)
        xs_scr[...] = jnp.concatenate([lo, hi], axis=-1).astype(BF16)

    def scatter_prev():
        for i0 in range(0, cap, group):
            scatter_group(idxp_ref, wtp_ref, i0)

    def gather_next():
        for i in range(cap):
            gather_row(idxn_ref, i)

    mod_gate = lambda: mod_ref[0, :, 5 * D_MODEL:6 * D_MODEL]

    @pl.when((e == 0) & (hf == 0))
    def _():
        for n in range(n_ch):
            in_copy(n).start()
        shift = mod_ref[0, :, 3 * D_MODEL:4 * D_MODEL]
        scale = mod_ref[0, :, 4 * D_MODEL:5 * D_MODEL]

        def norm(n, carry):
            in_copy(n).wait()
            r0 = pl.multiple_of(n * rows, rows)
            hn = (_rms(acc[pl.ds(r0, rows), :]) * ng_ref[...]) * (1.0 + scale) + shift
            tab[pl.ds(r0, rows), :] = pltpu.pack_elementwise([hn[:, :half], hn[:, half:]], packed_dtype=BF16)
            return carry

        lax.fori_loop(0, n_ch, norm, 0)

        def gather0(gi, carry):
            i0 = pl.multiple_of(gi * group, group)
            for r in range(group):
                gather_row(idxc_ref, i0 + r)
            return carry

        lax.fori_loop(0, cap // group, gather0, 0)
        unpack_rows()
        y_part[...] = ffn_part()

    @pl.when((e > 0) & (hf == 0))
    def _():
        unpack_rows()
        y_part[...] = ffn_part()
        scatter_prev()

    @pl.when(hf == 1)
    def _():
        y_fin[...] = (y_part[...] + ffn_part()) * mod_gate()
        gather_next()

    @pl.when((e == n_exp - 1) & (hf == 1))
    def _():
        def scatter_last(gi, carry):
            scatter_group(idxc_ref, wtc_ref, pl.multiple_of(gi * group, group))
            return carry

        lax.fori_loop(0, cap // group, scatter_last, 0)

        def fin(n, carry):
            r0 = pl.multiple_of(n * rows, rows)
            acc[pl.ds(r0, rows), :] = _rms(acc[pl.ds(r0, rows), :]) * fg_ref[...]
            out_copy(n).start()
            return carry

        lax.fori_loop(0, n_ch, fin, 0)
        for n in range(n_ch):
            out_copy(n).wait()


def _moe(idx, wts, x1, modrows, norm_ffn_g, wg, wu, wd, norm_final_g):
    B, L, D = x1.shape
    E, cap = idx.shape[1], idx.shape[2]
    idx3 = idx.reshape(B * E, 1, cap)
    wts3 = wts.reshape(B * E, 1, cap)

    def smem_spec(shift):
        return pl.BlockSpec((1, 1, cap), lambda b, e, h: (b * E + jnp.clip(e + shift, 0, E - 1), 0, 0),
                            memory_space=pltpu.SMEM)

    n_half = 2
    fc = D_EXPERT // n_half
    w_in = lambda: pl.BlockSpec((1, D, fc), lambda b, e, h: (e, 0, h))
    hbm = lambda: pl.BlockSpec(memory_space=pl.ANY)
    return pl.pallas_call(
        functools.partial(_moe_kernel, cap=cap, n_exp=E),
        grid=(B, E, n_half),
        in_specs=[smem_spec(0), smem_spec(0), smem_spec(1), smem_spec(-1), smem_spec(-1), hbm(),
                  pl.BlockSpec((1, 1, modrows.shape[-1]), lambda b, e, h: (b, 0, 0)),
                  pl.BlockSpec((1, D), lambda b, e, h: (0, 0)),
                  w_in(), w_in(),
                  pl.BlockSpec((1, fc, D), lambda b, e, h: (e, h, 0)),
                  pl.BlockSpec((1, D), lambda b, e, h: (0, 0))],
        out_specs=hbm(),
        out_shape=jax.ShapeDtypeStruct((B, L, D), F32),
        scratch_shapes=[pltpu.VMEM((L, D), F32), pltpu.VMEM((L, D // 2), jnp.uint32),
                        pltpu.VMEM((cap, D // 2), jnp.uint32), pltpu.VMEM((cap, D), BF16),
                        pltpu.VMEM((cap, D), F32), pltpu.VMEM((cap, D), F32),
                        pltpu.SemaphoreType.DMA((1 + L // MOE_ROWS,))],
        compiler_params=_cparams(3, VMEM_LIMIT_BIG),
        name="ec_moe",
    )(idx3, wts3, idx3, idx3, wts3, x1, modrows, norm_ffn_g.reshape(1, D), wg, wu, wd, norm_final_g.reshape(1, D))


def _pos_tables():
    quarter = D_MODEL // 4
    omega = 1.0 / (POS_BASE ** (np.arange(quarter, dtype=np.float64) / quarter))
    n = np.arange(GRID_W, dtype=np.float64)[:, None] * omega[None, :]
    tab = np.concatenate([np.sin(n), np.cos(n)], axis=-1).astype(np.float32)
    return tab


def _filter_features(L):
    t = np.linspace(0.0, 1.0, L, dtype=np.float32).astype(np.float64)[:, None]
    w = 2.0 * math.pi * np.arange(L, dtype=np.float64)[:, None] / L
    f = np.linspace(1e-4, HY_BANDS - 1, HY_BANDS, dtype=np.float32).astype(np.float64)[None, :]
    z = np.concatenate([t, np.cos(f * w), -np.sin(f * w)], axis=-1)
    zp = np.zeros((L, LANES), np.float32)
    zp[:, :HY_EMB] = z.astype(np.float32)
    return zp


def kernel(x, c, ctx, c_ctx, w_ada, b_ada, norm_mix_g, w_in, gla_wa_f, gla_ba_f, gla_wa_b, gla_ba_b, gla_norm_g,
           hy_conv_w, hy_conv_b, hy_w1, hy_b1, hy_w2, hy_b2, hy_w3, hy_freq, hy_bias, w_out, norm_ffn_g, w_router,
           w_gate, w_up, w_down, norm_final_g):
    B, L, D = x.shape
    assert w_ada.shape[0] == 1 and D == D_MODEL and L % (GRID_W * 8) == 0
    l = 0
    cap = EC_FACTOR * L // N_EXPERTS

    cc = jnp.zeros((8, D), F32).at[:B].set(c).at[B].set(c_ctx)
    modrows = _adaln(cc, w_ada[l], b_ada[l]).reshape(8, 1, N_MOD * D)

    s = (GLA_QK_W, 2 * GLA_QK_W, 2 * GLA_QK_W + GLA_V_W, 2 * GLA_QK_W + 2 * GLA_V_W,
         2 * GLA_QK_W + 2 * GLA_V_W + GLA_RANK, 2 * GLA_QK_W + 2 * GLA_V_W + 2 * GLA_RANK)
    w = w_in[l]
    w_q, w_k, w_v, w_g, w_af, w_ab, w_hy = (w[:, :s[0]], w[:, s[0]:s[1]], w[:, s[1]:s[2]], w[:, s[2]:s[3]],
                                            w[:, s[3]:s[4]], w[:, s[4]:s[5]], w[:, s[5]:])
    w_a = jnp.concatenate([w_af, w_ab, jnp.zeros((D, LANES - 2 * GLA_RANK), F32)], axis=1)
    w_x = jnp.concatenate([w_q, w_k, w_v, w_g, w_hy, w_a], axis=1).astype(BF16)
    w_c = jnp.concatenate([w_k, w_v, w_a], axis=1).astype(BF16)
    wa = jnp.zeros((LANES, 2 * GLA_QK_W), F32)
    wa = wa.at[:GLA_RANK, :GLA_QK_W].set(gla_wa_f[l]).at[GLA_RANK:2 * GLA_RANK, GLA_QK_W:].set(gla_wa_b[l])
    wa = wa.astype(BF16)
    ba = jnp.concatenate([gla_ba_f[l], gla_ba_b[l]]).reshape(1, 2 * GLA_QK_W)

    tab = jnp.asarray(_pos_tables())
    x0, q, k, v, g, u, la_f, la_b = _project(
        x, modrows, None, norm_mix_g[l], w_x, wa, ba, (tab, tab),
        (GLA_QK_W, GLA_QK_W, GLA_V_W, GLA_V_W, 3 * HY_WIDTH), True, 512,
        conv=(hy_conv_w[l], hy_conv_b[l].reshape(1, -1)))
    k_c, v_c, laf_c, lab_c = _project(ctx, modrows, B, norm_mix_g[l], w_c, wa, ba, None,
                                      (GLA_QK_W, GLA_V_W), False, ctx.shape[1])

    s_zero = jnp.zeros((B, GLA_V_W, GLA_QK_W), F32)
    nb_c = ctx.shape[1] // GLA_CHUNK
    _, s_f = _gla_pass(None, k_c, v_c, laf_c, s_zero, False, nb_c)
    _, s_b = _gla_pass(None, k_c, v_c, lab_c, s_zero, True, nb_c)
    o_f, _ = _gla_pass(q, k, v, la_f, s_f, False, 16)
    o_b, _ = _gla_pass(q, k, v, la_b, s_b, True, 16)

    w1p = jnp.zeros((LANES, HY_HIDDEN), F32).at[:HY_EMB].set(hy_w1[l])
    deltas = np.abs(np.linspace(HY_MIN_DECAY, HY_MAX_DECAY, HY_WIDTH, dtype=np.float32))
    deltas4 = jnp.asarray(np.tile(deltas, HY_ORDER * 2).reshape(1, -1))
    h = _hyena_filters(jnp.asarray(_filter_features(L)), w1p, hy_b1[l].reshape(1, -1), hy_w2[l],
                       hy_b2[l].reshape(1, -1), hy_w3[l], hy_freq[l].reshape(1, -1), deltas4)
    fwd, inv, mats = _dft_tables()
    hspec = _filter_spectra(h, mats)

    nct = HY_WIDTH // 256
    z1 = _long_conv(u, 0, u, nct, hy_bias[l][0:1], fwd, inv, hspec, 0)
    y_hy = _long_conv(z1, 0, u, 2 * nct, hy_bias[l][1:2], fwd, inv, hspec, 1)

    x1, logits_t = _merge(o_f, o_b, g, y_hy, x0, modrows, gla_norm_g[l].reshape(1, GLA_DV),
                                   w_out[l].astype(BF16), norm_ffn_g[l].reshape(1, D),
                                   w_router[l].T.astype(BF16))
    idx, wts = _topk(logits_t, cap)
    return _moe(idx, wts, x1, modrows, norm_ffn_g[l], w_gate[l], w_up[l], w_down[l], norm_final_g)
```

```python
import functools
import math

import numpy as np
import jax
import jax.numpy as jnp
from jax import lax
from jax.experimental import pallas as pl
from jax.experimental.pallas import tpu as pltpu

F32 = jnp.float32
BF16 = jnp.bfloat16

D_MODEL = 1024
GRID_W = 64
EPS = 1e-6
POS_BASE = 10000.0
N_MOD = 6

GLA_HEADS = 4
GLA_DK = 64
GLA_DV = 128
GLA_RANK = 16
GLA_GATE_NORM = 16.0
GLA_CHUNK = 64
GLA_QK_W = GLA_HEADS * GLA_DK
GLA_V_W = GLA_HEADS * GLA_DV

HY_WIDTH = D_MODEL - GLA_V_W
HY_ORDER = 2
HY_BANDS = 16
HY_EMB = 2 * HY_BANDS + 1
HY_HIDDEN = 64
HY_SHORT = 3
HY_MIN_DECAY = math.log(1e-2) / 1.5
HY_MAX_DECAY = math.log(1e-2) / 0.3

N_EXPERTS = 16
EC_FACTOR = 2
D_EXPERT = 1024

VMEM_LIMIT_BIG = 56 * 1024 * 1024
VMEM_LIMIT_MID = 40 * 1024 * 1024
LANES = 128

CONV_T = 512
MOE_ROWS = 512


def _cparams(n_axes, vmem=VMEM_LIMIT_MID):
    return pltpu.CompilerParams(dimension_semantics=("arbitrary",) * n_axes, vmem_limit_bytes=vmem)


def _bdot(a, b):
    return jnp.dot(a.astype(BF16), b.astype(BF16), preferred_element_type=F32)


def _tree_sum(terms):
    while len(terms) > 1:
        terms = [a + b for a, b in zip(terms[::2], terms[1::2])] + ([terms[-1]] if len(terms) % 2 else [])
    return terms[0]


def _rms(x):
    return x * lax.rsqrt(jnp.mean(x * x, axis=-1, keepdims=True) + EPS)


def _mod_kernel(c_ref, w_ref, b_ref, o_ref):
    c = c_ref[...]
    s = c * jax.nn.sigmoid(c)
    o_ref[...] = _bdot(s, w_ref[...]) + b_ref[...]


def _adaln(cc, w_ada, b_ada):
    rows, d = cc.shape
    n = w_ada.shape[1]
    tn = 1536
    return pl.pallas_call(
        _mod_kernel,
        grid=(n // tn,),
        in_specs=[pl.BlockSpec((rows, d), lambda j: (0, 0)),
                  pl.BlockSpec((d, tn), lambda j: (0, j)),
                  pl.BlockSpec((1, tn), lambda j: (0, j))],
        out_specs=pl.BlockSpec((rows, tn), lambda j: (0, j)),
        out_shape=jax.ShapeDtypeStruct((rows, n), F32),
        compiler_params=_cparams(1),
        name="adaln",
    )(cc, w_ada, b_ada.reshape(1, n))


def _proj_kernel(*refs, with_pos, tm, widths, q_scale_first):
    if with_pos:
        x_ref, xp_ref, xn_ref, rt_ref, ct_ref, cw_ref, cb_ref, mod_ref, g_ref, w_ref, wa_ref, ba_ref = refs[:12]
        outs = refs[12:]
    else:
        x_ref, mod_ref, g_ref, w_ref, wa_ref, ba_ref = refs[:6]
        outs = refs[6:]
    x = x_ref[0]
    if with_pos:
        i = pl.program_id(0)
        n_tiles = pl.num_programs(0)
        rows_per_tile = tm // GRID_W
        n_rows = rt_ref.shape[0]
        halo = xp_ref.shape[1]
        r0 = pl.multiple_of(i * rows_per_tile, rows_per_tile)
        rt = rt_ref[pl.ds(r0, rows_per_tile), :]
        half = D_MODEL // 2
        x3 = x.reshape(rows_per_tile, GRID_W, D_MODEL)
        x3 = jnp.concatenate([x3[:, :, :half] + rt[:, None, :], x3[:, :, half:] + ct_ref[...][None]], axis=-1)
        x = x3.reshape(tm, D_MODEL)
        outs[0][0] = x
        outs = outs[1:]
        rt_prev = rt_ref[pl.ds(jnp.maximum(r0 - 1, 0), 1), :]
        rt_next = rt_ref[pl.ds(jnp.minimum(r0 + rows_per_tile, n_rows - 1), 1), :]
        x_prev = xp_ref[0] + jnp.concatenate(
            [jnp.broadcast_to(rt_prev, (halo, half)), ct_ref[GRID_W - halo:GRID_W, :]], axis=-1)
        x_next = xn_ref[0] + jnp.concatenate([jnp.broadcast_to(rt_next, (halo, half)), ct_ref[0:halo, :]], axis=-1)
        x = jnp.concatenate([x, x_prev, x_next], axis=0)
    shift = mod_ref[0, :, 0:D_MODEL]
    scale = mod_ref[0, :, D_MODEL:2 * D_MODEL]
    hn = (_rms(x) * g_ref[...]) * (1.0 + scale) + shift
    p_all = _bdot(hn, w_ref[...])
    p = p_all[:tm]
    off = 0
    for n, (o_ref, w) in enumerate(zip(outs[:-2], widths)):
        blk = p[:, off:off + w]
        if q_scale_first and n == 0:
            blk = blk * (GLA_DK ** -0.5)
        if with_pos and n == len(widths) - 1:
            before = jnp.where(i == 0, 0.0, p_all[tm + halo - 1:tm + halo, off:off + w])
            after = jnp.where(i == n_tiles - 1, 0.0, p_all[tm + halo:tm + halo + 1, off:off + w])
            t = lax.broadcasted_iota(jnp.int32, blk.shape, 0)
            up = jnp.where(t == 0, before, pltpu.roll(blk, 1, 0))
            dn = jnp.where(t == tm - 1, after, pltpu.roll(blk, tm - 1, 0))
            blk = up * cw_ref[0:1, :] + blk * cw_ref[1:2, :] + dn * cw_ref[2:3, :] + cb_ref[...]
        o_ref[0] = blk
        off += w
    a = p[:, off:off + LANES]
    z = _bdot(a, wa_ref[...]) + ba_ref[...]
    la = jax.nn.log_sigmoid(z) / GLA_GATE_NORM
    outs[-2][0] = la[:, :GLA_QK_W]
    outs[-1][0] = la[:, GLA_QK_W:]


def _project(x, modrows, mod_row0, norm_g, w_cols, wa, ba, tabs, widths, q_scale_first, tm, conv=None):
    B, L, D = x.shape
    with_pos = tabs is not None
    ncols = w_cols.shape[1]
    in_specs = [pl.BlockSpec((1, tm, D), lambda i, b: (b, i, 0))]
    args = [x]
    if with_pos:
        rt, ct = tabs
        cw, cb = conv
        halo = 8
        per = tm // halo
        in_specs += [pl.BlockSpec((1, halo, D), lambda i, b: (b, jnp.maximum(i * per - 1, 0), 0)),
                     pl.BlockSpec((1, halo, D), lambda i, b: (b, jnp.minimum((i + 1) * per, L // halo - 1), 0)),
                     pl.BlockSpec(rt.shape, lambda i, b: (0, 0)), pl.BlockSpec(ct.shape, lambda i, b: (0, 0)),
                     pl.BlockSpec(cw.shape, lambda i, b: (0, 0)), pl.BlockSpec(cb.shape, lambda i, b: (0, 0))]
        args += [x, x, rt, ct, cw, cb]
    if mod_row0 is None:
        mod_map = lambda i, b: (b, 0, 0)
    else:
        mod_map = lambda i, b: (mod_row0, 0, 0)
    in_specs += [pl.BlockSpec((1, 1, modrows.shape[-1]), mod_map),
                 pl.BlockSpec((1, D), lambda i, b: (0, 0)),
                 pl.BlockSpec((D, ncols), lambda i, b: (0, 0)),
                 pl.BlockSpec(wa.shape, lambda i, b: (0, 0)),
                 pl.BlockSpec(ba.shape, lambda i, b: (0, 0))]
    args += [modrows, norm_g.reshape(1, D), w_cols, wa, ba]
    out_w = ([D] if with_pos else []) + list(widths) + [GLA_QK_W, GLA_QK_W]
    out_specs = [pl.BlockSpec((1, tm, w), lambda i, b: (b, i, 0)) for w in out_w]
    out_shape = [jax.ShapeDtypeStruct((B, L, w), F32) for w in out_w]
    return pl.pallas_call(
        functools.partial(_proj_kernel, with_pos=with_pos, tm=tm, widths=tuple(widths), q_scale_first=q_scale_first),
        grid=(L // tm, B),
        in_specs=in_specs, out_specs=out_specs, out_shape=out_shape,
        compiler_params=_cparams(2, VMEM_LIMIT_BIG),
        name="project_x" if with_pos else "project_ctx",
    )(*args)


def _gla_kernel(*refs, reverse, with_output, nb):
    if with_output:
        q_ref, k_ref, v_ref, la_ref, s0_ref, o_ref, sfin_ref, s_scr = refs
    else:
        k_ref, v_ref, la_ref, s0_ref, sfin_ref, s_scr = refs
    C = GLA_CHUNK

    @pl.when(pl.program_id(1) == 0)
    def _():
        s_scr[...] = s0_ref[0]

    H = GLA_HEADS
    nt = (((1,), (1,)), ((), ()))
    tn = (((0,), (0,)), ((), ()))
    row = lax.broadcasted_iota(jnp.int32, (C, GLA_QK_W), 0)
    sc_r = lax.broadcasted_iota(jnp.int32, (C, H * C), 0)
    sc_s = lax.broadcasted_iota(jnp.int32, (C, H * C), 1) % C
    keep = (sc_r <= sc_s) if reverse else (sc_r >= sc_s)
    def head_mask(shape, rdiv, cdiv):
        return (lax.broadcasted_iota(jnp.int32, shape, 0) // rdiv) == (lax.broadcasted_iota(jnp.int32, shape, 1) // cdiv)
    m_k = head_mask((H * C, GLA_QK_W), C, GLA_DK)
    m_v = head_mask((H * C, GLA_V_W), C, GLA_DV)
    m_s = head_mask((GLA_V_W, GLA_QK_W), GLA_DV, GLA_DK)

    def prefix_sum(x):
        s = 1
        while s < C:
            if reverse:
                x = x + jnp.where(row < C - s, pltpu.roll(x, C - s, 0), 0.0)
            else:
                x = x + jnp.where(row >= s, pltpu.roll(x, s, 0), 0.0)
            s *= 2
        return x

    order = range(nb - 1, -1, -1) if reverse else range(nb)
    for ci in order:
        sl = slice(ci * C, (ci + 1) * C)
        k = k_ref[0, sl, :]
        v = v_ref[0, sl, :]
        b = prefix_sum(la_ref[0, sl, :])
        last = 0 if reverse else C - 1
        b_tot = b[last:last + 1, :]
        kdec = (k * jnp.exp(b_tot - b)).astype(BF16)
        vb = v.astype(BF16)
        upd_t = jnp.where(m_s, lax.dot_general(vb, kdec, tn, preferred_element_type=F32), 0.0)
        st = s_scr[...]
        if with_output:
            q = q_ref[0, sl, :]
            mid = C // 2 - 1 if reverse else C // 2
            ref = b[mid:mid + 1, :]
            qs = (q * jnp.exp(b - ref)).astype(BF16)
            ks = (k * jnp.exp(ref - b)).astype(BF16)
            qb = (q * jnp.exp(b)).astype(BF16)
            o_inter = lax.dot_general(qb, st.astype(BF16), nt, preferred_element_type=F32)
            ks4 = jnp.where(m_k, jnp.concatenate([ks] * H, axis=0), jnp.zeros((), BF16))
            sc = lax.dot_general(qs, ks4, nt, preferred_element_type=F32)
            sc = jnp.where(keep, sc, 0.0).astype(BF16)
            v4 = jnp.where(m_v, jnp.concatenate([vb] * H, axis=0), jnp.zeros((), BF16))
            o_ref[0, sl, :] = o_inter + jnp.dot(sc, v4, preferred_element_type=F32)
        s_scr[...] = st * jnp.exp(b_tot) + upd_t
    sfin_ref[0] = s_scr[...]


def _gla_pass(q, k, v, la, s0, reverse, nb):
    B, L, _ = k.shape
    with_output = q is not None
    tm = nb * GLA_CHUNK
    nblk = L // tm
    if reverse:
        tmap = lambda b, i: (b, nblk - 1 - i, 0)
    else:
        tmap = lambda b, i: (b, i, 0)
    smap = lambda b, i: (b, 0, 0)
    in_specs, args = [], []
    if with_output:
        in_specs.append(pl.BlockSpec((1, tm, GLA_QK_W), tmap))
        args.append(q)
    in_specs += [pl.BlockSpec((1, tm, GLA_QK_W), tmap), pl.BlockSpec((1, tm, GLA_V_W), tmap),
                 pl.BlockSpec((1, tm, GLA_QK_W), tmap), pl.BlockSpec((1, GLA_V_W, GLA_QK_W), smap)]
    args += [k, v, la, s0]
    out_specs, out_shape = [], []
    if with_output:
        out_specs.append(pl.BlockSpec((1, tm, GLA_V_W), tmap))
        out_shape.append(jax.ShapeDtypeStruct((B, L, GLA_V_W), F32))
    out_specs.append(pl.BlockSpec((1, GLA_V_W, GLA_QK_W), smap))
    out_shape.append(jax.ShapeDtypeStruct((B, GLA_V_W, GLA_QK_W), F32))
    res = pl.pallas_call(
        functools.partial(_gla_kernel, reverse=reverse, with_output=with_output, nb=nb),
        grid=(B, nblk),
        in_specs=in_specs, out_specs=out_specs, out_shape=out_shape,
        scratch_shapes=[pltpu.VMEM((GLA_V_W, GLA_QK_W), F32)],
        compiler_params=_cparams(2),
        name=("gla_" + ("bwd" if reverse else "fwd") + ("" if with_output else "_state")),
    )(*args)
    return (res[0], res[1]) if with_output else (None, res[0])


def _filter_kernel(z_ref, w1_ref, b1_ref, w2_ref, b2_ref, w3_ref, fr_ref, dl_ref, o_ref, *, tm, L):
    hi = lax.Precision.HIGHEST
    fr = fr_ref[...]
    h1 = jnp.sin(fr * (jnp.dot(z_ref[...], w1_ref[...], precision=hi, preferred_element_type=F32) + b1_ref[...]))
    h2 = jnp.sin(fr * (jnp.dot(h1, w2_ref[...], precision=hi, preferred_element_type=F32) + b2_ref[...]))
    h = _bdot(h2, w3_ref[...])
    t = z_ref[:, 0:1]
    o_ref[...] = h * jnp.exp(-t * dl_ref[...])


def _hyena_filters(zfeat, w1p, b1, w2, b2, w3, freq, deltas4):
    L = zfeat.shape[0]
    tm = 512
    n = w3.shape[1]
    full = lambda a: pl.BlockSpec(a.shape, lambda i: (0,) * a.ndim)
    return pl.pallas_call(
        functools.partial(_filter_kernel, tm=tm, L=L),
        grid=(L // tm,),
        in_specs=[pl.BlockSpec((tm, zfeat.shape[1]), lambda i: (i, 0)), full(w1p), full(b1), full(w2), full(b2),
                  full(w3), full(freq), full(deltas4)],
        out_specs=pl.BlockSpec((tm, n), lambda i: (i, 0)),
        out_shape=jax.ShapeDtypeStruct((L, n), F32),
        compiler_params=_cparams(1),
        name="hyena_filters",
    )(zfeat, w1p, b1, w2, b2, w3, freq, deltas4)


def _dft_tables():
    T = CONV_T
    f = np.arange(T, dtype=np.float64)[:, None] + 0.5
    n = np.arange(T, dtype=np.float64)[None, :]
    th = 2.0 * np.pi * f * n / (2 * T)
    thn = 2.0 * np.pi * f * (n - T) / (2 * T)
    fwd = np.concatenate([np.cos(th), -np.sin(th)], axis=0)
    inv = np.concatenate([np.cos(th).T, -np.sin(th).T], axis=1) / T
    nz = np.ones((1, T)); nz[0, 0] = 0.0
    P = np.concatenate([np.cos(th), -np.sin(th)], axis=0)
    Pc = np.concatenate([np.cos(th), np.sin(th)], axis=0)
    N = np.concatenate([np.cos(thn), -np.sin(thn)], axis=0) * nz
    Nc = np.concatenate([np.cos(thn), np.sin(thn)], axis=0) * nz
    m_pos = np.concatenate([N, P], axis=1)
    m_zero = np.concatenate([Pc * nz, P], axis=1)
    m_neg = np.concatenate([Nc, Pc], axis=1)
    mats = np.stack([m_pos, m_zero, m_neg], axis=0)
    as_bf16 = lambda a: jnp.asarray(a, dtype=F32).astype(BF16)
    return as_bf16(fwd), as_bf16(inv), as_bf16(mats)


def _spec_kernel(m_ref, a_ref, b_ref, o_ref):
    T = CONV_T
    o_ref[0, 0] = _bdot(m_ref[0, :, :T], a_ref[...]) + _bdot(m_ref[0, :, T:], b_ref[...])


def _filter_spectra(h, mats):
    L = h.shape[0]
    T = CONV_T
    J = L // T
    W = HY_WIDTH
    ct = W
    nct = W // ct
    nlag = 2 * J - 1

    def sel(dl):
        d = dl - (J - 1)
        return jnp.where(d > 0, 0, jnp.where(d == 0, 1, 2))

    def a_map(o, dl, c):
        d = dl - (J - 1)
        row = jnp.where(d > 0, d - 1, jnp.where(d == 0, 0, -d - 1))
        dirn = jnp.where(d > 0, 0, 1)
        return (row, (o * 2 + dirn) * nct + c)

    def b_map(o, dl, c):
        d = dl - (J - 1)
        row = jnp.where(d >= 0, d, -d)
        dirn = jnp.where(d >= 0, 0, 1)
        return (row, (o * 2 + dirn) * nct + c)

    return pl.pallas_call(
        _spec_kernel,
        grid=(HY_ORDER, nlag, nct),
        in_specs=[pl.BlockSpec((1, 2 * T, 2 * T), lambda o, dl, c: (sel(dl), 0, 0)),
                  pl.BlockSpec((T, ct), a_map),
                  pl.BlockSpec((T, ct), b_map)],
        out_specs=pl.BlockSpec((1, 1, 2 * T, ct), lambda o, dl, c: (o, dl, 0, c)),
        out_shape=jax.ShapeDtypeStruct((HY_ORDER, nlag, 2 * T, W), F32),
        compiler_params=_cparams(3),
        name="filter_spectra",
    )(mats, h, h)


def _lconv_kernel(u_ref, gate_ref, bias_ref, fwd_ref, inv_ref, h_ref, o_ref, u_scr, y_scr, *, J, ct, GI):
    T = CONV_T
    g = pl.program_id(2)

    @pl.when(g == 0)
    def _():
        for j in range(J):
            u_scr[j] = _bdot(fwd_ref[...], u_ref[0, j * T:(j + 1) * T, :])

    def run_group(i0):
        def tile_body(rt, carry):
            r0 = pl.multiple_of(rt * 8, 8)
            for lt in range(ct // LANES):
                cs = slice(lt * LANES, (lt + 1) * LANES)
                ure = [u_scr[j, pl.ds(r0, 8), cs] for j in range(J)]
                uim = [u_scr[j, pl.ds(T + r0, 8), cs] for j in range(J)]
                for ii in range(GI):
                    re_terms, im_terms = [], []
                    for j in range(J):
                        lag = i0 + ii - j + (J - 1)
                        hre = h_ref[0, lag, pl.ds(r0, 8), cs]
                        him = h_ref[0, lag, pl.ds(T + r0, 8), cs]
                        re_terms.append(hre * ure[j] - him * uim[j])
                        im_terms.append(hre * uim[j] + him * ure[j])
                    y_scr[ii, pl.ds(r0, 8), cs] = _tree_sum(re_terms)
                    y_scr[ii, pl.ds(T + r0, 8), cs] = _tree_sum(im_terms)
            return carry

        lax.fori_loop(0, T // 8, tile_body, 0)
        for ii in range(GI):
            y = _bdot(inv_ref[...], y_scr[ii])
            ui = u_ref[0, (i0 + ii) * T:(i0 + ii + 1) * T, :]
            o_ref[0, ii * T:(ii + 1) * T, :] = gate_ref[0, ii * T:(ii + 1) * T, :] * (y + ui * bias_ref[...])

    for gg in range(J // GI):
        pl.when(g == gg)(functools.partial(run_group, gg * GI))


def _long_conv(u_arr, u_blk0, gate_arr, gate_blk0, bias_row, fwd, inv, hspec, order):
    B, L, _ = u_arr.shape
    T = CONV_T
    J = L // T
    ct = 256
    nct = HY_WIDTH // ct
    nlag = 2 * J - 1
    GI = 4
    return pl.pallas_call(
        functools.partial(_lconv_kernel, J=J, ct=ct, GI=GI),
        grid=(nct, B, J // GI),
        in_specs=[pl.BlockSpec((1, L, ct), lambda c, b, i: (b, 0, u_blk0 + c)),
                  pl.BlockSpec((1, GI * T, ct), lambda c, b, i: (b, i, gate_blk0 + c)),
                  pl.BlockSpec((1, ct), lambda c, b, i: (0, c)),
                  pl.BlockSpec(fwd.shape, lambda c, b, i: (0, 0)),
                  pl.BlockSpec(inv.shape, lambda c, b, i: (0, 0)),
                  pl.BlockSpec((1, nlag, 2 * T, ct), lambda c, b, i: (order, 0, 0, c),
                               pipeline_mode=pl.Buffered(1))],
        out_specs=pl.BlockSpec((1, GI * T, ct), lambda c, b, i: (b, i, c)),
        out_shape=jax.ShapeDtypeStruct((B, L, HY_WIDTH), F32),
        scratch_shapes=[pltpu.VMEM((J, 2 * T, ct), F32), pltpu.VMEM((GI, 2 * T, ct), F32)],
        compiler_params=_cparams(3, VMEM_LIMIT_BIG),
        name="hyena_long_conv%d" % order,
    )(u_arr, gate_arr, bias_row, fwd, inv, hspec)


def _merge_kernel(of_ref, ob_ref, g_ref, hy_ref, x0_ref, mod_ref, gn_ref, wout_ref, fg_ref, wr_ref,
                  x1_ref, lg_ref):
    o = of_ref[0] + ob_ref[0]
    g = g_ref[0]
    parts = []
    for h in range(GLA_HEADS):
        oh = o[:, h * GLA_DV:(h + 1) * GLA_DV]
        parts.append(_rms(oh) * gn_ref[...])
    y_gla = jnp.concatenate(parts, axis=-1) * (g * jax.nn.sigmoid(g))
    ycat = jnp.concatenate([y_gla, hy_ref[0]], axis=-1)
    m = _bdot(ycat, wout_ref[...])
    gate = mod_ref[0, :, 2 * D_MODEL:3 * D_MODEL]
    x1 = x0_ref[0] + gate * m
    x1_ref[0] = x1
    shift = mod_ref[0, :, 3 * D_MODEL:4 * D_MODEL]
    scale = mod_ref[0, :, 4 * D_MODEL:5 * D_MODEL]
    hn = ((_rms(x1) * fg_ref[...]) * (1.0 + scale) + shift).astype(BF16)
    lg_ref[0] = lax.dot_general(wr_ref[...], hn, (((1,), (1,)), ((), ())), preferred_element_type=F32)


def _merge(o_f, o_b, g, y_hy, x0, modrows, gla_norm_g, w_out, norm_ffn_g, w_router_t):
    B, L, D = x0.shape
    tm = 1024
    tmap = lambda b, i: (b, i, 0)
    full2 = lambda a: pl.BlockSpec(a.shape, lambda b, i: (0, 0))
    return pl.pallas_call(
        _merge_kernel,
        grid=(B, L // tm),
        in_specs=[pl.BlockSpec((1, tm, GLA_V_W), tmap), pl.BlockSpec((1, tm, GLA_V_W), tmap),
                  pl.BlockSpec((1, tm, GLA_V_W), tmap), pl.BlockSpec((1, tm, HY_WIDTH), tmap),
                  pl.BlockSpec((1, tm, D), tmap),
                  pl.BlockSpec((1, 1, modrows.shape[-1]), lambda b, i: (b, 0, 0)),
                  full2(gla_norm_g), full2(w_out), full2(norm_ffn_g), full2(w_router_t)],
        out_specs=[pl.BlockSpec((1, tm, D), tmap), pl.BlockSpec((1, N_EXPERTS, tm), lambda b, i: (b, 0, i))],
        out_shape=[jax.ShapeDtypeStruct((B, L, D), F32), jax.ShapeDtypeStruct((B, N_EXPERTS, L), F32)],
        compiler_params=_cparams(2, VMEM_LIMIT_BIG),
        name="merge_router",
    )(o_f, o_b, g, y_hy, x0, modrows, gla_norm_g, w_out, norm_ffn_g, w_router_t)


def _topk_kernel(lg_ref, idx_ref, wt_ref, pos_scr, *, cap):
    E, L = lg_ref.shape[1], lg_ref.shape[2]
    lg = lg_ref[0]
    mx = jnp.max(lg, axis=0, keepdims=True)
    ex = jnp.exp(lg - mx)
    aff = ex / jnp.sum(ex, axis=0, keepdims=True)

    def count(mask):
        return jnp.sum(mask.astype(jnp.int32), axis=1, keepdims=True)

    def tbody(n, cur):
        cand = cur | lax.shift_left(jnp.int32(1), 30 - n)
        return jnp.where(count(aff >= lax.bitcast_convert_type(cand, F32)) >= cap, cand, cur)

    thr = lax.bitcast_convert_type(lax.fori_loop(0, 31, tbody, jnp.zeros((E, 1), jnp.int32)), F32)
    gt = aff > thr
    eq = aff == thr
    need = cap - count(gt)
    tok = lax.broadcasted_iota(jnp.int32, (E, L), 1)

    def ibody(n, cur):
        cand = cur + lax.shift_left(jnp.int32(1), 12 - n)
        ok = (cand <= L) & (count(eq & (tok < cand)) <= need)
        return jnp.where(ok, cand, cur)

    bound = lax.fori_loop(0, 13, ibody, jnp.zeros((E, 1), jnp.int32))
    sel = gt | (eq & (tok < bound))

    li = lax.broadcasted_iota(jnp.int32, (LANES, LANES), 0)
    lj = lax.broadcasted_iota(jnp.int32, (LANES, LANES), 1)
    ustrict = (li < lj).astype(BF16)
    off = jnp.zeros((E, 1), F32)
    for r in range(L // LANES):
        s_r = sel[:, r * LANES:(r + 1) * LANES]
        sf = s_r.astype(F32)
        pre = jnp.dot(sf.astype(BF16), ustrict, preferred_element_type=F32)
        pos_scr[:, r * LANES:(r + 1) * LANES] = jnp.where(s_r, pre + off, -1.0)
        off = off + jnp.sum(sf, axis=1, keepdims=True)

    slot = lax.broadcasted_iota(jnp.int32, (cap, L), 0).astype(F32)
    tokf = lax.broadcasted_iota(jnp.int32, (8, L), 1)
    row = lax.broadcasted_iota(jnp.int32, (8, L), 0)
    t_hi = (tokf // 64).astype(F32)
    t_lo = (tokf % 64).astype(F32)
    for e in range(E):
        a = aff[e:e + 1, :]
        a1 = a.astype(BF16).astype(F32)
        a2 = (a - a1).astype(BF16).astype(F32)
        a3 = a - a1 - a2
        vals = jnp.where(row == 0, t_hi, jnp.where(row == 1, t_lo, jnp.where(row == 2, a1, jnp.where(
            row == 3, a2, jnp.where(row == 4, a3, 0.0)))))
        onehot = jnp.where(pos_scr[e:e + 1, :] == slot, 1.0, 0.0).astype(BF16)
        res = lax.dot_general(vals.astype(BF16), onehot, (((1,), (1,)), ((), ())), preferred_element_type=F32)
        idx_ref[0, e:e + 1, :] = (res[0:1, :] * 64.0 + res[1:2, :]).astype(jnp.int32)
        wt_ref[0, e:e + 1, :] = res[2:3, :] + res[3:4, :] + res[4:5, :]


def _topk(logits_t, cap):
    B, E, L = logits_t.shape
    return pl.pallas_call(
        functools.partial(_topk_kernel, cap=cap),
        grid=(B,),
        in_specs=[pl.BlockSpec((1, E, L), lambda b: (b, 0, 0))],
        out_specs=[pl.BlockSpec((1, E, cap), lambda b: (b, 0, 0)), pl.BlockSpec((1, E, cap), lambda b: (b, 0, 0))],
        out_shape=[jax.ShapeDtypeStruct((B, E, cap), jnp.int32), jax.ShapeDtypeStruct((B, E, cap), F32)],
        scratch_shapes=[pltpu.VMEM((E, L), F32)],
        compiler_params=_cparams(1),
        name="ec_topk",
    )(logits_t)


def _moe_kernel(idxc_ref, wtc_ref, idxn_ref, idxp_ref, wtp_ref, x1_hbm, mod_ref, ng_ref, wg_ref, wu_ref, wd_ref,
                fg_ref, out_hbm, acc, tab, xsp, xs_scr, y_part, y_fin, sem, *, cap, n_exp):
    b = pl.program_id(0)
    e = pl.program_id(1)
    hf = pl.program_id(2)
    L = acc.shape[0]
    rows = MOE_ROWS
    n_ch = L // rows
    group = 8
    half = D_MODEL // 2

    def chunk(n):
        return pl.ds(n * rows if isinstance(n, int) else pl.multiple_of(n * rows, rows), rows)

    def in_copy(n):
        return pltpu.make_async_copy(x1_hbm.at[b, chunk(n)], acc.at[chunk(n)], sem.at[1 + n])

    def out_copy(n):
        return pltpu.make_async_copy(acc.at[chunk(n)], out_hbm.at[b, chunk(n)], sem.at[0])

    def gather_row(idx_ref, i):
        xsp[pl.ds(i, 1), :] = tab[pl.ds(idx_ref[0, 0, i], 1), :]

    def scatter_group(idx_ref, wt_ref, i0):
        ts = [idx_ref[0, 0, i0 + r] for r in range(group)]
        new = [acc[pl.ds(ts[r], 1), :] + y_fin[pl.ds(i0 + r, 1), :] * wt_ref[0, 0, i0 + r] for r in range(group)]
        for r in range(group):
            acc[pl.ds(ts[r], 1), :] = new[r]

    def ffn_part():
        xs = xs_scr[...]
        y = None
        width = wg_ref.shape[2]
        for f0 in range(0, width, width // 2):
            fs = slice(f0, f0 + width // 2)
            gte = _bdot(xs, wg_ref[0, :, fs])
            up = _bdot(xs, wu_ref[0, :, fs])
            part = _bdot(gte * jax.nn.sigmoid(gte) * up, wd_ref[0, fs, :])
            y = part if y is None else y + part
        return y

    def unpack_rows():
        xp = xsp[...]
        lo = pltpu.unpack_elementwise(xp, index=0, packed_dtype=BF16, unpacked_dtype=F32)
        hi = pltpu.unpack_elementwise(xp, index=1, packed_dtype=BF16, unpacked_dtype=F32)
        xs_scr[...] = jnp.concatenate([lo, hi], axis=-1).astype(BF16)

    def scatter_prev():
        for i0 in range(0, cap, group):
            scatter_group(idxp_ref, wtp_ref, i0)

    def gather_next():
        for i in range(cap):
            gather_row(idxn_ref, i)

    mod_gate = lambda: mod_ref[0, :, 5 * D_MODEL:6 * D_MODEL]

    @pl.when((e == 0) & (hf == 0))
    def _():
        for n in range(n_ch):
            in_copy(n).start()
        shift = mod_ref[0, :, 3 * D_MODEL:4 * D_MODEL]
        scale = mod_ref[0, :, 4 * D_MODEL:5 * D_MODEL]

        def norm(n, carry):
            in_copy(n).wait()
            r0 = pl.multiple_of(n * rows, rows)
            hn = (_rms(acc[pl.ds(r0, rows), :]) * ng_ref[...]) * (1.0 + scale) + shift
            tab[pl.ds(r0, rows), :] = pltpu.pack_elementwise([hn[:, :half], hn[:, half:]], packed_dtype=BF16)
            return carry

        lax.fori_loop(0, n_ch, norm, 0)

        def gather0(gi, carry):
            i0 = pl.multiple_of(gi * group, group)
            for r in range(group):
                gather_row(idxc_ref, i0 + r)
            return carry

        lax.fori_loop(0, cap // group, gather0, 0)
        unpack_rows()
        y_part[...] = ffn_part()

    @pl.when((e > 0) & (hf == 0))
    def _():
        scatter_prev()
        y_part[...] = ffn_part()

    @pl.when(hf == 1)
    def _():
        gather_next()
        y_fin[...] = (y_part[...] + ffn_part()) * mod_gate()
        unpack_rows()

    @pl.when((e == n_exp - 1) & (hf == 1))
    def _():
        def scatter_last(gi, carry):
            scatter_group(idxc_ref, wtc_ref, pl.multiple_of(gi * group, group))
            return carry

        lax.fori_loop(0, cap // group, scatter_last, 0)

        def fin(n, carry):
            r0 = pl.multiple_of(n * rows, rows)
            acc[pl.ds(r0, rows), :] = _rms(acc[pl.ds(r0, rows), :]) * fg_ref[...]
            out_copy(n).start()
            return carry

        lax.fori_loop(0, n_ch, fin, 0)
        for n in range(n_ch):
            out_copy(n).wait()


def _moe(idx, wts, x1, modrows, norm_ffn_g, wg, wu, wd, norm_final_g):
    B, L, D = x1.shape
    E, cap = idx.shape[1], idx.shape[2]
    idx3 = idx.reshape(B * E, 1, cap)
    wts3 = wts.reshape(B * E, 1, cap)

    def smem_spec(shift):
        return pl.BlockSpec((1, 1, cap), lambda b, e, h: (b * E + jnp.clip(e + shift, 0, E - 1), 0, 0),
                            memory_space=pltpu.SMEM)

    n_half = 2
    fc = D_EXPERT // n_half
    w_in = lambda: pl.BlockSpec((1, D, fc), lambda b, e, h: (e, 0, h))
    hbm = lambda: pl.BlockSpec(memory_space=pl.ANY)
    return pl.pallas_call(
        functools.partial(_moe_kernel, cap=cap, n_exp=E),
        grid=(B, E, n_half),
        in_specs=[smem_spec(0), smem_spec(0), smem_spec(1), smem_spec(-1), smem_spec(-1), hbm(),
                  pl.BlockSpec((1, 1, modrows.shape[-1]), lambda b, e, h: (b, 0, 0)),
                  pl.BlockSpec((1, D), lambda b, e, h: (0, 0)),
                  w_in(), w_in(),
                  pl.BlockSpec((1, fc, D), lambda b, e, h: (e, h, 0)),
                  pl.BlockSpec((1, D), lambda b, e, h: (0, 0))],
        out_specs=hbm(),
        out_shape=jax.ShapeDtypeStruct((B, L, D), F32),
        scratch_shapes=[pltpu.VMEM((L, D), F32), pltpu.VMEM((L, D // 2), jnp.uint32),
                        pltpu.VMEM((cap, D // 2), jnp.uint32), pltpu.VMEM((cap, D), BF16),
                        pltpu.VMEM((cap, D), F32), pltpu.VMEM((cap, D), F32),
                        pltpu.SemaphoreType.DMA((1 + L // MOE_ROWS,))],
        compiler_params=_cparams(3, VMEM_LIMIT_BIG),
        name="ec_moe",
    )(idx3, wts3, idx3, idx3, wts3, x1, modrows, norm_ffn_g.reshape(1, D), wg, wu, wd, norm_final_g.reshape(1, D))


def _pos_tables():
    quarter = D_MODEL // 4
    omega = 1.0 / (POS_BASE ** (np.arange(quarter, dtype=np.float64) / quarter))
    n = np.arange(GRID_W, dtype=np.float64)[:, None] * omega[None, :]
    tab = np.concatenate([np.sin(n), np.cos(n)], axis=-1).astype(np.float32)
    return tab


def _filter_features(L):
    t = np.linspace(0.0, 1.0, L, dtype=np.float32).astype(np.float64)[:, None]
    w = 2.0 * math.pi * np.arange(L, dtype=np.float64)[:, None] / L
    f = np.linspace(1e-4, HY_BANDS - 1, HY_BANDS, dtype=np.float32).astype(np.float64)[None, :]
    z = np.concatenate([t, np.cos(f * w), -np.sin(f * w)], axis=-1)
    zp = np.zeros((L, LANES), np.float32)
    zp[:, :HY_EMB] = z.astype(np.float32)
    return zp


def kernel(x, c, ctx, c_ctx, w_ada, b_ada, norm_mix_g, w_in, gla_wa_f, gla_ba_f, gla_wa_b, gla_ba_b, gla_norm_g,
           hy_conv_w, hy_conv_b, hy_w1, hy_b1, hy_w2, hy_b2, hy_w3, hy_freq, hy_bias, w_out, norm_ffn_g, w_router,
           w_gate, w_up, w_down, norm_final_g):
    B, L, D = x.shape
    assert w_ada.shape[0] == 1 and D == D_MODEL and L % (GRID_W * 8) == 0
    l = 0
    cap = EC_FACTOR * L // N_EXPERTS

    cc = jnp.zeros((8, D), F32).at[:B].set(c).at[B].set(c_ctx)
    modrows = _adaln(cc, w_ada[l], b_ada[l]).reshape(8, 1, N_MOD * D)

    s = (GLA_QK_W, 2 * GLA_QK_W, 2 * GLA_QK_W + GLA_V_W, 2 * GLA_QK_W + 2 * GLA_V_W,
         2 * GLA_QK_W + 2 * GLA_V_W + GLA_RANK, 2 * GLA_QK_W + 2 * GLA_V_W + 2 * GLA_RANK)
    w = w_in[l]
    w_q, w_k, w_v, w_g, w_af, w_ab, w_hy = (w[:, :s[0]], w[:, s[0]:s[1]], w[:, s[1]:s[2]], w[:, s[2]:s[3]],
                                            w[:, s[3]:s[4]], w[:, s[4]:s[5]], w[:, s[5]:])
    w_a = jnp.concatenate([w_af, w_ab, jnp.zeros((D, LANES - 2 * GLA_RANK), F32)], axis=1)
    w_x = jnp.concatenate([w_q, w_k, w_v, w_g, w_hy, w_a], axis=1).astype(BF16)
    w_c = jnp.concatenate([w_k, w_v, w_a], axis=1).astype(BF16)
    wa = jnp.zeros((LANES, 2 * GLA_QK_W), F32)
    wa = wa.at[:GLA_RANK, :GLA_QK_W].set(gla_wa_f[l]).at[GLA_RANK:2 * GLA_RANK, GLA_QK_W:].set(gla_wa_b[l])
    wa = wa.astype(BF16)
    ba = jnp.concatenate([gla_ba_f[l], gla_ba_b[l]]).reshape(1, 2 * GLA_QK_W)

    tab = jnp.asarray(_pos_tables())
    x0, q, k, v, g, u, la_f, la_b = _project(
        x, modrows, None, norm_mix_g[l], w_x, wa, ba, (tab, tab),
        (GLA_QK_W, GLA_QK_W, GLA_V_W, GLA_V_W, 3 * HY_WIDTH), True, 1024,
        conv=(hy_conv_w[l], hy_conv_b[l].reshape(1, -1)))
    k_c, v_c, laf_c, lab_c = _project(ctx, modrows, B, norm_mix_g[l], w_c, wa, ba, None,
                                      (GLA_QK_W, GLA_V_W), False, ctx.shape[1])

    s_zero = jnp.zeros((B, GLA_V_W, GLA_QK_W), F32)
    nb_c = ctx.shape[1] // GLA_CHUNK
    _, s_f = _gla_pass(None, k_c, v_c, laf_c, s_zero, False, nb_c)
    _, s_b = _gla_pass(None, k_c, v_c, lab_c, s_zero, True, nb_c)
    o_f, _ = _gla_pass(q, k, v, la_f, s_f, False, 16)
    o_b, _ = _gla_pass(q, k, v, la_b, s_b, True, 16)

    w1p = jnp.zeros((LANES, HY_HIDDEN), F32).at[:HY_EMB].set(hy_w1[l])
    deltas = np.abs(np.linspace(HY_MIN_DECAY, HY_MAX_DECAY, HY_WIDTH, dtype=np.float32))
    deltas4 = jnp.asarray(np.tile(deltas, HY_ORDER * 2).reshape(1, -1))
    h = _hyena_filters(jnp.asarray(_filter_features(L)), w1p, hy_b1[l].reshape(1, -1), hy_w2[l],
                       hy_b2[l].reshape(1, -1), hy_w3[l], hy_freq[l].reshape(1, -1), deltas4)
    fwd, inv, mats = _dft_tables()
    hspec = _filter_spectra(h, mats)

    nct = HY_WIDTH // 256
    z1 = _long_conv(u, 0, u, nct, hy_bias[l][0:1], fwd, inv, hspec, 0)
    y_hy = _long_conv(z1, 0, u, 2 * nct, hy_bias[l][1:2], fwd, inv, hspec, 1)

    x1, logits_t = _merge(o_f, o_b, g, y_hy, x0, modrows, gla_norm_g[l].reshape(1, GLA_DV),
                                   w_out[l].astype(BF16), norm_ffn_g[l].reshape(1, D),
                                   w_router[l].T.astype(BF16))
    idx, wts = _topk(logits_t, cap)
    return _moe(idx, wts, x1, modrows, norm_ffn_g[l], w_gate[l], w_up[l], w_down[l], norm_final_g)
```

```python
import functools
import math

import numpy as np
import jax
import jax.numpy as jnp
from jax import lax
from jax.experimental import pallas as pl
from jax.experimental.pallas import tpu as pltpu

F32 = jnp.float32
BF16 = jnp.bfloat16

D_MODEL = 1024
GRID_W = 64
EPS = 1e-6
POS_BASE = 10000.0
N_MOD = 6

GLA_HEADS = 4
GLA_DK = 64
GLA_DV = 128
GLA_RANK = 16
GLA_GATE_NORM = 16.0
GLA_CHUNK = 64
GLA_QK_W = GLA_HEADS * GLA_DK
GLA_V_W = GLA_HEADS * GLA_DV

HY_WIDTH = D_MODEL - GLA_V_W
HY_ORDER = 2
HY_BANDS = 16
HY_EMB = 2 * HY_BANDS + 1
HY_HIDDEN = 64
HY_SHORT = 3
HY_MIN_DECAY = math.log(1e-2) / 1.5
HY_MAX_DECAY = math.log(1e-2) / 0.3

N_EXPERTS = 16
EC_FACTOR = 2
D_EXPERT = 1024

VMEM_LIMIT_BIG = 56 * 1024 * 1024
VMEM_LIMIT_MID = 40 * 1024 * 1024
LANES = 128

CONV_T = 512
MOE_ROWS = 512


def _cparams(n_axes, vmem=VMEM_LIMIT_MID):
    return pltpu.CompilerParams(dimension_semantics=("arbitrary",) * n_axes, vmem_limit_bytes=vmem)


def _bdot(a, b):
    return jnp.dot(a.astype(BF16), b.astype(BF16), preferred_element_type=F32)


def _tree_sum(terms):
    while len(terms) > 1:
        terms = [a + b for a, b in zip(terms[::2], terms[1::2])] + ([terms[-1]] if len(terms) % 2 else [])
    return terms[0]


def _rms(x):
    return x * lax.rsqrt(jnp.mean(x * x, axis=-1, keepdims=True) + EPS)


def _mod_kernel(c_ref, w_ref, b_ref, o_ref):
    c = c_ref[...]
    s = c * jax.nn.sigmoid(c)
    o_ref[...] = _bdot(s, w_ref[...]) + b_ref[...]


def _adaln(cc, w_ada, b_ada):
    rows, d = cc.shape
    n = w_ada.shape[1]
    tn = 1536
    return pl.pallas_call(
        _mod_kernel,
        grid=(n // tn,),
        in_specs=[pl.BlockSpec((rows, d), lambda j: (0, 0)),
                  pl.BlockSpec((d, tn), lambda j: (0, j)),
                  pl.BlockSpec((1, tn), lambda j: (0, j))],
        out_specs=pl.BlockSpec((rows, tn), lambda j: (0, j)),
        out_shape=jax.ShapeDtypeStruct((rows, n), F32),
        compiler_params=_cparams(1),
        name="adaln",
    )(cc, w_ada, b_ada.reshape(1, n))


def _proj_kernel(*refs, with_pos, tm, widths, q_scale_first):
    if with_pos:
        x_ref, xp_ref, xn_ref, rt_ref, ct_ref, cw_ref, cb_ref, mod_ref, g_ref, w_ref, wa_ref, ba_ref = refs[:12]
        outs = refs[12:]
    else:
        x_ref, mod_ref, g_ref, w_ref, wa_ref, ba_ref = refs[:6]
        outs = refs[6:]
    x = x_ref[0]
    if with_pos:
        i = pl.program_id(0)
        n_tiles = pl.num_programs(0)
        rows_per_tile = tm // GRID_W
        n_rows = rt_ref.shape[0]
        halo = xp_ref.shape[1]
        r0 = pl.multiple_of(i * rows_per_tile, rows_per_tile)
        rt = rt_ref[pl.ds(r0, rows_per_tile), :]
        half = D_MODEL // 2
        x3 = x.reshape(rows_per_tile, GRID_W, D_MODEL)
        x3 = jnp.concatenate([x3[:, :, :half] + rt[:, None, :], x3[:, :, half:] + ct_ref[...][None]], axis=-1)
        x = x3.reshape(tm, D_MODEL)
        outs[0][0] = x
        outs = outs[1:]
        rt_prev = rt_ref[pl.ds(jnp.maximum(r0 - 1, 0), 1), :]
        rt_next = rt_ref[pl.ds(jnp.minimum(r0 + rows_per_tile, n_rows - 1), 1), :]
        x_prev = xp_ref[0] + jnp.concatenate(
            [jnp.broadcast_to(rt_prev, (halo, half)), ct_ref[GRID_W - halo:GRID_W, :]], axis=-1)
        x_next = xn_ref[0] + jnp.concatenate([jnp.broadcast_to(rt_next, (halo, half)), ct_ref[0:halo, :]], axis=-1)
        x = jnp.concatenate([x, x_prev, x_next], axis=0)
    shift = mod_ref[0, :, 0:D_MODEL]
    scale = mod_ref[0, :, D_MODEL:2 * D_MODEL]
    hn = (_rms(x) * g_ref[...]) * (1.0 + scale) + shift
    p_all = _bdot(hn, w_ref[...])
    p = p_all[:tm]
    off = 0
    for n, (o_ref, w) in enumerate(zip(outs[:-2], widths)):
        blk = p[:, off:off + w]
        if q_scale_first and n == 0:
            blk = blk * (GLA_DK ** -0.5)
        if with_pos and n == len(widths) - 1:
            before = jnp.where(i == 0, 0.0, p_all[tm + halo - 1:tm + halo, off:off + w])
            after = jnp.where(i == n_tiles - 1, 0.0, p_all[tm + halo:tm + halo + 1, off:off + w])
            t = lax.broadcasted_iota(jnp.int32, blk.shape, 0)
            up = jnp.where(t == 0, before, pltpu.roll(blk, 1, 0))
            dn = jnp.where(t == tm - 1, after, pltpu.roll(blk, tm - 1, 0))
            blk = up * cw_ref[0:1, :] + blk * cw_ref[1:2, :] + dn * cw_ref[2:3, :] + cb_ref[...]
        o_ref[0] = blk.astype(o_ref.dtype)
        off += w
    a = p[:, off:off + LANES]
    z = _bdot(a, wa_ref[...]) + ba_ref[...]
    la = jax.nn.log_sigmoid(z) / GLA_GATE_NORM
    outs[-2][0] = la[:, :GLA_QK_W]
    outs[-1][0] = la[:, GLA_QK_W:]


def _project(x, modrows, mod_row0, norm_g, w_cols, wa, ba, tabs, widths, q_scale_first, tm, conv=None):
    B, L, D = x.shape
    with_pos = tabs is not None
    ncols = w_cols.shape[1]
    in_specs = [pl.BlockSpec((1, tm, D), lambda i, b: (b, i, 0))]
    args = [x]
    if with_pos:
        rt, ct = tabs
        cw, cb = conv
        halo = 8
        per = tm // halo
        in_specs += [pl.BlockSpec((1, halo, D), lambda i, b: (b, jnp.maximum(i * per - 1, 0), 0)),
                     pl.BlockSpec((1, halo, D), lambda i, b: (b, jnp.minimum((i + 1) * per, L // halo - 1), 0)),
                     pl.BlockSpec(rt.shape, lambda i, b: (0, 0)), pl.BlockSpec(ct.shape, lambda i, b: (0, 0)),
                     pl.BlockSpec(cw.shape, lambda i, b: (0, 0)), pl.BlockSpec(cb.shape, lambda i, b: (0, 0))]
        args += [x, x, rt, ct, cw, cb]
    if mod_row0 is None:
        mod_map = lambda i, b: (b, 0, 0)
    else:
        mod_map = lambda i, b: (mod_row0, 0, 0)
    in_specs += [pl.BlockSpec((1, 1, modrows.shape[-1]), mod_map),
                 pl.BlockSpec((1, D), lambda i, b: (0, 0)),
                 pl.BlockSpec((D, ncols), lambda i, b: (0, 0)),
                 pl.BlockSpec(wa.shape, lambda i, b: (0, 0)),
                 pl.BlockSpec(ba.shape, lambda i, b: (0, 0))]
    args += [modrows, norm_g.reshape(1, D), w_cols, wa, ba]
    out_w = ([D] if with_pos else []) + list(widths) + [GLA_QK_W, GLA_QK_W]
    out_specs = [pl.BlockSpec((1, tm, w), lambda i, b: (b, i, 0)) for w in out_w]
    out_shape = [jax.ShapeDtypeStruct((B, L, w), F32) for w in out_w]
    v_pos = (1 if with_pos else 0) + list(widths).index(GLA_V_W)
    out_shape[v_pos] = jax.ShapeDtypeStruct((B, L, GLA_V_W), BF16)
    return pl.pallas_call(
        functools.partial(_proj_kernel, with_pos=with_pos, tm=tm, widths=tuple(widths), q_scale_first=q_scale_first),
        grid=(L // tm, B),
        in_specs=in_specs, out_specs=out_specs, out_shape=out_shape,
        compiler_params=_cparams(2, VMEM_LIMIT_BIG),
        name="project_x" if with_pos else "project_ctx",
    )(*args)


def _gla_kernel(*refs, reverse, with_output, nb):
    if with_output:
        q_ref, k_ref, v_ref, la_ref, s0_ref, o_ref, sfin_ref, s_scr = refs
    else:
        k_ref, v_ref, la_ref, s0_ref, sfin_ref, s_scr = refs
    C = GLA_CHUNK

    @pl.when(pl.program_id(1) == 0)
    def _():
        s_scr[...] = s0_ref[0]

    H = GLA_HEADS
    nt = (((1,), (1,)), ((), ()))
    tn = (((0,), (0,)), ((), ()))
    row = lax.broadcasted_iota(jnp.int32, (C, GLA_QK_W), 0)
    sc_r = lax.broadcasted_iota(jnp.int32, (C, H * C), 0)
    sc_s = lax.broadcasted_iota(jnp.int32, (C, H * C), 1) % C
    keep = (sc_r <= sc_s) if reverse else (sc_r >= sc_s)
    def head_mask(shape, rdiv, cdiv):
        return (lax.broadcasted_iota(jnp.int32, shape, 0) // rdiv) == (lax.broadcasted_iota(jnp.int32, shape, 1) // cdiv)
    m_k = head_mask((H * C, GLA_QK_W), C, GLA_DK)
    m_v = head_mask((H * C, GLA_V_W), C, GLA_DV)
    m_s = head_mask((GLA_V_W, GLA_QK_W), GLA_DV, GLA_DK)

    def prefix_sum(x):
        s = 1
        while s < C:
            if reverse:
                x = x + jnp.where(row < C - s, pltpu.roll(x, C - s, 0), 0.0)
            else:
                x = x + jnp.where(row >= s, pltpu.roll(x, s, 0), 0.0)
            s *= 2
        return x

    order = range(nb - 1, -1, -1) if reverse else range(nb)
    for ci in order:
        sl = slice(ci * C, (ci + 1) * C)
        k = k_ref[0, sl, :]
        v = v_ref[0, sl, :]
        b = prefix_sum(la_ref[0, sl, :])
        last = 0 if reverse else C - 1
        b_tot = b[last:last + 1, :]
        kdec = (k * jnp.exp(b_tot - b)).astype(BF16)
        vb = v.astype(BF16)
        upd_t = jnp.where(m_s, lax.dot_general(vb, kdec, tn, preferred_element_type=F32), 0.0)
        st = s_scr[...]
        if with_output:
            q = q_ref[0, sl, :]
            mid = C // 2 - 1 if reverse else C // 2
            ref = b[mid:mid + 1, :]
            qs = (q * jnp.exp(b - ref)).astype(BF16)
            ks = (k * jnp.exp(ref - b)).astype(BF16)
            qb = (q * jnp.exp(b)).astype(BF16)
            o_inter = lax.dot_general(qb, st.astype(BF16), nt, preferred_element_type=F32)
            ks4 = jnp.where(m_k, jnp.concatenate([ks] * H, axis=0), jnp.zeros((), BF16))
            sc = lax.dot_general(qs, ks4, nt, preferred_element_type=F32)
            sc = jnp.where(keep, sc, 0.0).astype(BF16)
            v4 = jnp.where(m_v, jnp.concatenate([vb] * H, axis=0), jnp.zeros((), BF16))
            o_ref[0, sl, :] = o_inter + jnp.dot(sc, v4, preferred_element_type=F32)
        s_scr[...] = st * jnp.exp(b_tot) + upd_t
    sfin_ref[0] = s_scr[...]


def _gla_pass(q, k, v, la, s0, reverse, nb):
    B, L, _ = k.shape
    with_output = q is not None
    tm = nb * GLA_CHUNK
    nblk = L // tm
    if reverse:
        tmap = lambda b, i: (b, nblk - 1 - i, 0)
    else:
        tmap = lambda b, i: (b, i, 0)
    smap = lambda b, i: (b, 0, 0)
    in_specs, args = [], []
    if with_output:
        in_specs.append(pl.BlockSpec((1, tm, GLA_QK_W), tmap))
        args.append(q)
    in_specs += [pl.BlockSpec((1, tm, GLA_QK_W), tmap), pl.BlockSpec((1, tm, GLA_V_W), tmap),
                 pl.BlockSpec((1, tm, GLA_QK_W), tmap), pl.BlockSpec((1, GLA_V_W, GLA_QK_W), smap)]
    args += [k, v, la, s0]
    out_specs, out_shape = [], []
    if with_output:
        out_specs.append(pl.BlockSpec((1, tm, GLA_V_W), tmap))
        out_shape.append(jax.ShapeDtypeStruct((B, L, GLA_V_W), F32))
    out_specs.append(pl.BlockSpec((1, GLA_V_W, GLA_QK_W), smap))
    out_shape.append(jax.ShapeDtypeStruct((B, GLA_V_W, GLA_QK_W), F32))
    res = pl.pallas_call(
        functools.partial(_gla_kernel, reverse=reverse, with_output=with_output, nb=nb),
        grid=(B, nblk),
        in_specs=in_specs, out_specs=out_specs, out_shape=out_shape,
        scratch_shapes=[pltpu.VMEM((GLA_V_W, GLA_QK_W), F32)],
        compiler_params=_cparams(2),
        name=("gla_" + ("bwd" if reverse else "fwd") + ("" if with_output else "_state")),
    )(*args)
    return (res[0], res[1]) if with_output else (None, res[0])


def _filter_kernel(z_ref, w1_ref, b1_ref, w2_ref, b2_ref, w3_ref, fr_ref, dl_ref, o_ref, *, tm, L):
    hi = lax.Precision.HIGHEST
    fr = fr_ref[...]
    h1 = jnp.sin(fr * (jnp.dot(z_ref[...], w1_ref[...], precision=hi, preferred_element_type=F32) + b1_ref[...]))
    h2 = jnp.sin(fr * (jnp.dot(h1, w2_ref[...], precision=hi, preferred_element_type=F32) + b2_ref[...]))
    h = _bdot(h2, w3_ref[...])
    t = z_ref[:, 0:1]
    decay = jnp.exp(-t * dl_ref[...])
    o_ref[...] = (h * jnp.concatenate([decay] * (h.shape[1] // decay.shape[1]), axis=-1)).astype(o_ref.dtype)


def _hyena_filters(zfeat, w1p, b1, w2, b2, w3, freq, decay_rates):
    L = zfeat.shape[0]
    tm = 512
    n = w3.shape[1]
    full = lambda a: pl.BlockSpec(a.shape, lambda i: (0,) * a.ndim)
    return pl.pallas_call(
        functools.partial(_filter_kernel, tm=tm, L=L),
        grid=(L // tm,),
        in_specs=[pl.BlockSpec((tm, zfeat.shape[1]), lambda i: (i, 0)), full(w1p), full(b1), full(w2), full(b2),
                  full(w3), full(freq), full(decay_rates)],
        out_specs=pl.BlockSpec((tm, n), lambda i: (i, 0)),
        out_shape=jax.ShapeDtypeStruct((L, n), BF16),
        compiler_params=_cparams(1),
        name="hyena_filters",
    )(zfeat, w1p, b1, w2, b2, w3, freq, decay_rates)


def _dft_tables():
    T = CONV_T
    f = np.arange(T, dtype=np.float64)[:, None] + 0.5
    n = np.arange(T, dtype=np.float64)[None, :]
    th = 2.0 * np.pi * f * n / (2 * T)
    thn = 2.0 * np.pi * f * (n - T) / (2 * T)
    fwd = np.concatenate([np.cos(th), -np.sin(th)], axis=0)
    inv = np.concatenate([np.cos(th).T, -np.sin(th).T], axis=1) / T
    nz = np.ones((1, T)); nz[0, 0] = 0.0
    P = np.concatenate([np.cos(th), -np.sin(th)], axis=0)
    Pc = np.concatenate([np.cos(th), np.sin(th)], axis=0)
    N = np.concatenate([np.cos(thn), -np.sin(thn)], axis=0) * nz
    Nc = np.concatenate([np.cos(thn), np.sin(thn)], axis=0) * nz
    m_pos = np.concatenate([N, P], axis=1)
    m_zero = np.concatenate([Pc * nz, P], axis=1)
    m_neg = np.concatenate([Nc, Pc], axis=1)
    mats = np.stack([m_pos, m_zero, m_neg], axis=0)
    as_bf16 = lambda a: jnp.asarray(a, dtype=F32).astype(BF16)
    return as_bf16(fwd), as_bf16(inv), as_bf16(mats)


def _spec_kernel(m_ref, a_ref, b_ref, o_ref):
    T = CONV_T
    o_ref[0, 0] = _bdot(m_ref[0, :, :T], a_ref[...]) + _bdot(m_ref[0, :, T:], b_ref[...])


def _filter_spectra(h, mats):
    L = h.shape[0]
    T = CONV_T
    J = L // T
    W = HY_WIDTH
    ct = W
    nct = W // ct
    nlag = 2 * J - 1

    def sel(dl):
        d = dl - (J - 1)
        return jnp.where(d > 0, 0, jnp.where(d == 0, 1, 2))

    def a_map(o, dl, c):
        d = dl - (J - 1)
        row = jnp.where(d > 0, d - 1, jnp.where(d == 0, 0, -d - 1))
        dirn = jnp.where(d > 0, 0, 1)
        return (row, (o * 2 + dirn) * nct + c)

    def b_map(o, dl, c):
        d = dl - (J - 1)
        row = jnp.where(d >= 0, d, -d)
        dirn = jnp.where(d >= 0, 0, 1)
        return (row, (o * 2 + dirn) * nct + c)

    return pl.pallas_call(
        _spec_kernel,
        grid=(HY_ORDER, nlag, nct),
        in_specs=[pl.BlockSpec((1, 2 * T, 2 * T), lambda o, dl, c: (sel(dl), 0, 0)),
                  pl.BlockSpec((T, ct), a_map),
                  pl.BlockSpec((T, ct), b_map)],
        out_specs=pl.BlockSpec((1, 1, 2 * T, ct), lambda o, dl, c: (o, dl, 0, c)),
        out_shape=jax.ShapeDtypeStruct((HY_ORDER, nlag, 2 * T, W), F32),
        compiler_params=_cparams(3),
        name="filter_spectra",
    )(mats, h, h)


def _lconv_kernel(u_ref, gate_ref, bias_ref, fwd_ref, inv_ref, h_ref, o_ref, u_scr, y_scr, *, J, ct, GI):
    T = CONV_T
    g = pl.program_id(2)

    @pl.when(g == 0)
    def _():
        for j in range(J):
            u_scr[j] = _bdot(fwd_ref[...], u_ref[0, j * T:(j + 1) * T, :])

    def run_group(i0):
        def tile_body(rt, carry):
            r0 = pl.multiple_of(rt * 8, 8)
            for lt in range(ct // LANES):
                cs = slice(lt * LANES, (lt + 1) * LANES)
                ure = [u_scr[j, pl.ds(r0, 8), cs] for j in range(J)]
                uim = [u_scr[j, pl.ds(T + r0, 8), cs] for j in range(J)]
                for ii in range(GI):
                    re_terms, im_terms = [], []
                    for j in range(J):
                        lag = i0 + ii - j + (J - 1)
                        hre = h_ref[0, lag, pl.ds(r0, 8), cs]
                        him = h_ref[0, lag, pl.ds(T + r0, 8), cs]
                        re_terms.append(hre * ure[j] - him * uim[j])
                        im_terms.append(hre * uim[j] + him * ure[j])
                    y_scr[ii, pl.ds(r0, 8), cs] = _tree_sum(re_terms)
                    y_scr[ii, pl.ds(T + r0, 8), cs] = _tree_sum(im_terms)
            return carry

        lax.fori_loop(0, T // 8, tile_body, 0)
        for ii in range(GI):
            y = _bdot(inv_ref[...], y_scr[ii])
            ui = u_ref[0, (i0 + ii) * T:(i0 + ii + 1) * T, :]
            o_ref[0, ii * T:(ii + 1) * T, :] = gate_ref[0, ii * T:(ii + 1) * T, :] * (y + ui * bias_ref[...])

    for gg in range(J // GI):
        pl.when(g == gg)(functools.partial(run_group, gg * GI))


def _long_conv(u_arr, u_blk0, gate_arr, gate_blk0, bias_row, fwd, inv, hspec, order):
    B, L, _ = u_arr.shape
    T = CONV_T
    J = L // T
    ct = 256
    nct = HY_WIDTH // ct
    nlag = 2 * J - 1
    GI = 4
    return pl.pallas_call(
        functools.partial(_lconv_kernel, J=J, ct=ct, GI=GI),
        grid=(nct, B, J // GI),
        in_specs=[pl.BlockSpec((1, L, ct), lambda c, b, i: (b, 0, u_blk0 + c)),
                  pl.BlockSpec((1, GI * T, ct), lambda c, b, i: (b, i, gate_blk0 + c)),
                  pl.BlockSpec((1, ct), lambda c, b, i: (0, c)),
                  pl.BlockSpec(fwd.shape, lambda c, b, i: (0, 0)),
                  pl.BlockSpec(inv.shape, lambda c, b, i: (0, 0)),
                  pl.BlockSpec((1, nlag, 2 * T, ct), lambda c, b, i: (order, 0, 0, c),
                               pipeline_mode=pl.Buffered(1))],
        out_specs=pl.BlockSpec((1, GI * T, ct), lambda c, b, i: (b, i, c)),
        out_shape=jax.ShapeDtypeStruct((B, L, HY_WIDTH), F32),
        scratch_shapes=[pltpu.VMEM((J, 2 * T, ct), F32), pltpu.VMEM((GI, 2 * T, ct), F32)],
        compiler_params=_cparams(3, VMEM_LIMIT_BIG),
        name="hyena_long_conv%d" % order,
    )(u_arr, gate_arr, bias_row, fwd, inv, hspec)


def _merge_kernel(of_ref, ob_ref, g_ref, hy_ref, x0_ref, mod_ref, gn_ref, wout_ref, fg_ref, wr_ref,
                  x1_ref, lg_ref):
    o = of_ref[0] + ob_ref[0]
    g = g_ref[0]
    parts = []
    for h in range(GLA_HEADS):
        oh = o[:, h * GLA_DV:(h + 1) * GLA_DV]
        parts.append(_rms(oh) * gn_ref[...])
    y_gla = jnp.concatenate(parts, axis=-1) * (g * jax.nn.sigmoid(g))
    ycat = jnp.concatenate([y_gla, hy_ref[0]], axis=-1)
    m = _bdot(ycat, wout_ref[...])
    gate = mod_ref[0, :, 2 * D_MODEL:3 * D_MODEL]
    x1 = x0_ref[0] + gate * m
    x1_ref[0] = x1
    shift = mod_ref[0, :, 3 * D_MODEL:4 * D_MODEL]
    scale = mod_ref[0, :, 4 * D_MODEL:5 * D_MODEL]
    hn = ((_rms(x1) * fg_ref[...]) * (1.0 + scale) + shift).astype(BF16)
    lg_ref[0] = lax.dot_general(wr_ref[...], hn, (((1,), (1,)), ((), ())), preferred_element_type=F32)


def _merge(o_f, o_b, g, y_hy, x0, modrows, gla_norm_g, w_out, norm_ffn_g, w_router_t):
    B, L, D = x0.shape
    tm = 1024
    tmap = lambda b, i: (b, i, 0)
    full2 = lambda a: pl.BlockSpec(a.shape, lambda b, i: (0, 0))
    return pl.pallas_call(
        _merge_kernel,
        grid=(B, L // tm),
        in_specs=[pl.BlockSpec((1, tm, GLA_V_W), tmap), pl.BlockSpec((1, tm, GLA_V_W), tmap),
                  pl.BlockSpec((1, tm, GLA_V_W), tmap), pl.BlockSpec((1, tm, HY_WIDTH), tmap),
                  pl.BlockSpec((1, tm, D), tmap),
                  pl.BlockSpec((1, 1, modrows.shape[-1]), lambda b, i: (b, 0, 0)),
                  full2(gla_norm_g), full2(w_out), full2(norm_ffn_g), full2(w_router_t)],
        out_specs=[pl.BlockSpec((1, tm, D), tmap), pl.BlockSpec((1, N_EXPERTS, tm), lambda b, i: (b, 0, i))],
        out_shape=[jax.ShapeDtypeStruct((B, L, D), F32), jax.ShapeDtypeStruct((B, N_EXPERTS, L), F32)],
        compiler_params=_cparams(2, VMEM_LIMIT_BIG),
        name="merge_router",
    )(o_f, o_b, g, y_hy, x0, modrows, gla_norm_g, w_out, norm_ffn_g, w_router_t)


def _topk_kernel(lg_ref, idx_ref, wt_ref, pos_scr, *, cap):
    E, L = lg_ref.shape[1], lg_ref.shape[2]
    lg = lg_ref[0]
    mx = jnp.max(lg, axis=0, keepdims=True)
    ex = jnp.exp(lg - mx)
    aff = ex / jnp.sum(ex, axis=0, keepdims=True)

    def count(mask):
        return jnp.sum(mask.astype(jnp.int32), axis=1, keepdims=True)

    def tbody(n, cur):
        cand = cur | lax.shift_left(jnp.int32(1), 30 - n)
        return jnp.where(count(aff >= lax.bitcast_convert_type(cand, F32)) >= cap, cand, cur)

    thr = lax.bitcast_convert_type(lax.fori_loop(0, 31, tbody, jnp.zeros((E, 1), jnp.int32)), F32)
    gt = aff > thr
    eq = aff == thr
    need = cap - count(gt)
    tok = lax.broadcasted_iota(jnp.int32, (E, L), 1)

    def ibody(n, cur):
        cand = cur + lax.shift_left(jnp.int32(1), 12 - n)
        ok = (cand <= L) & (count(eq & (tok < cand)) <= need)
        return jnp.where(ok, cand, cur)

    bound = lax.fori_loop(0, 13, ibody, jnp.zeros((E, 1), jnp.int32))
    sel = gt | (eq & (tok < bound))

    li = lax.broadcasted_iota(jnp.int32, (LANES, LANES), 0)
    lj = lax.broadcasted_iota(jnp.int32, (LANES, LANES), 1)
    ustrict = (li < lj).astype(BF16)
    off = jnp.zeros((E, 1), F32)
    for r in range(L // LANES):
        s_r = sel[:, r * LANES:(r + 1) * LANES]
        sf = s_r.astype(F32)
        pre = jnp.dot(sf.astype(BF16), ustrict, preferred_element_type=F32)
        pos_scr[:, r * LANES:(r + 1) * LANES] = jnp.where(s_r, pre + off, -1.0)
        off = off + jnp.sum(sf, axis=1, keepdims=True)

    slot = lax.broadcasted_iota(jnp.int32, (cap, L), 0).astype(F32)
    tokf = lax.broadcasted_iota(jnp.int32, (8, L), 1)
    row = lax.broadcasted_iota(jnp.int32, (8, L), 0)
    t_hi = (tokf // 64).astype(F32)
    t_lo = (tokf % 64).astype(F32)
    for e in range(E):
        a = aff[e:e + 1, :]
        a1 = a.astype(BF16).astype(F32)
        a2 = (a - a1).astype(BF16).astype(F32)
        a3 = a - a1 - a2
        vals = jnp.where(row == 0, t_hi, jnp.where(row == 1, t_lo, jnp.where(row == 2, a1, jnp.where(
            row == 3, a2, jnp.where(row == 4, a3, 0.0)))))
        onehot = jnp.where(pos_scr[e:e + 1, :] == slot, 1.0, 0.0).astype(BF16)
        res = lax.dot_general(vals.astype(BF16), onehot, (((1,), (1,)), ((), ())), preferred_element_type=F32)
        idx_ref[0, e:e + 1, :] = (res[0:1, :] * 64.0 + res[1:2, :]).astype(jnp.int32)
        wt_ref[0, e:e + 1, :] = res[2:3, :] + res[3:4, :] + res[4:5, :]


def _topk(logits_t, cap):
    B, E, L = logits_t.shape
    return pl.pallas_call(
        functools.partial(_topk_kernel, cap=cap),
        grid=(B,),
        in_specs=[pl.BlockSpec((1, E, L), lambda b: (b, 0, 0))],
        out_specs=[pl.BlockSpec((1, E, cap), lambda b: (b, 0, 0)), pl.BlockSpec((1, E, cap), lambda b: (b, 0, 0))],
        out_shape=[jax.ShapeDtypeStruct((B, E, cap), jnp.int32), jax.ShapeDtypeStruct((B, E, cap), F32)],
        scratch_shapes=[pltpu.VMEM((E, L), F32)],
        compiler_params=_cparams(1),
        name="ec_topk",
    )(logits_t)


def _moe_kernel(idxc_ref, wtc_ref, idxn_ref, idxp_ref, wtp_ref, x1_hbm, mod_ref, ng_ref, wg_ref, wu_ref, wd_ref,
                fg_ref, out_hbm, acc, tab, xsp, xs_scr, y_part, y_fin, sem, *, cap, n_exp):
    b = pl.program_id(0)
    e = pl.program_id(1)
    hf = pl.program_id(2)
    L = acc.shape[0]
    rows = MOE_ROWS
    n_ch = L // rows
    group = 8
    half = D_MODEL // 2

    def chunk(n):
        return pl.ds(n * rows if isinstance(n, int) else pl.multiple_of(n * rows, rows), rows)

    def in_copy(n):
        return pltpu.make_async_copy(x1_hbm.at[b, chunk(n)], acc.at[chunk(n)], sem.at[1 + n])

    def out_copy(n):
        return pltpu.make_async_copy(acc.at[chunk(n)], out_hbm.at[b, chunk(n)], sem.at[0])

    def gather_row(idx_ref, i):
        xsp[pl.ds(i, 1), :] = tab[pl.ds(idx_ref[0, 0, i], 1), :]

    def scatter_group(idx_ref, wt_ref, i0):
        ts = [idx_ref[0, 0, i0 + r] for r in range(group)]
        new = [acc[pl.ds(ts[r], 1), :] + y_fin[pl.ds(i0 + r, 1), :] * wt_ref[0, 0, i0 + r] for r in range(group)]
        for r in range(group):
            acc[pl.ds(ts[r], 1), :] = new[r]

    def ffn_part():
        xs = xs_scr[...]
        y = None
        width = wg_ref.shape[2]
        for f0 in range(0, width, width // 2):
            fs = slice(f0, f0 + width // 2)
            gte = _bdot(xs, wg_ref[0, :, fs])
            up = _bdot(xs, wu_ref[0, :, fs])
            part = _bdot(gte * jax.nn.sigmoid(gte) * up, wd_ref[0, fs, :])
            y = part if y is None else y + part
        return y

    def unpack_rows():
        xp = xsp[...]
        lo = pltpu.unpack_elementwise(xp, index=0, packed_dtype=BF16, unpacked_dtype=F32)
        hi = pltpu.unpack_elementwise(xp, index=1, packed_dtype=BF16, unpacked_dtype=F32)
        xs_scr[...] = jnp.concatenate([lo, hi], axis=-1).astype(BF16)

    def scatter_prev():
        for i0 in range(0, cap, group):
            scatter_group(idxp_ref, wtp_ref, i0)

    def gather_next():
        for i in range(cap):
            gather_row(idxn_ref, i)

    mod_gate = lambda: mod_ref[0, :, 5 * D_MODEL:6 * D_MODEL]

    @pl.when((e == 0) & (hf == 0))
    def _():
        for n in range(n_ch):
            in_copy(n).start()
        shift = mod_ref[0, :, 3 * D_MODEL:4 * D_MODEL]
        scale = mod_ref[0, :, 4 * D_MODEL:5 * D_MODEL]

        def norm(n, carry):
            in_copy(n).wait()
            r0 = pl.multiple_of(n * rows, rows)
            hn = (_rms(acc[pl.ds(r0, rows), :]) * ng_ref[...]) * (1.0 + scale) + shift
            tab[pl.ds(r0, rows), :] = pltpu.pack_elementwise([hn[:, :half], hn[:, half:]], packed_dtype=BF16)
            return carry

        lax.fori_loop(0, n_ch, norm, 0)

        def gather0(gi, carry):
            i0 = pl.multiple_of(gi * group, group)
            for r in range(group):
                gather_row(idxc_ref, i0 + r)
            return carry

        lax.fori_loop(0, cap // group, gather0, 0)
        unpack_rows()
        y_part[...] = ffn_part()

    @pl.when((e > 0) & (hf == 0))
    def _():
        scatter_prev()
        y_part[...] = ffn_part()

    @pl.when(hf == 1)
    def _():
        gather_next()
        y_fin[...] = (y_part[...] + ffn_part()) * mod_gate()
        unpack_rows()

    @pl.when((e == n_exp - 1) & (hf == 1))
    def _():
        def scatter_last(gi, carry):
            scatter_group(idxc_ref, wtc_ref, pl.multiple_of(gi * group, group))
            return carry

        lax.fori_loop(0, cap // group, scatter_last, 0)

        def fin(n, carry):
            r0 = pl.multiple_of(n * rows, rows)
            acc[pl.ds(r0, rows), :] = _rms(acc[pl.ds(r0, rows), :]) * fg_ref[...]
            out_copy(n).start()
            return carry

        lax.fori_loop(0, n_ch, fin, 0)
        for n in range(n_ch):
            out_copy(n).wait()


def _moe(idx, wts, x1, modrows, norm_ffn_g, wg, wu, wd, norm_final_g):
    B, L, D = x1.shape
    E, cap = idx.shape[1], idx.shape[2]
    idx3 = idx.reshape(B * E, 1, cap)
    wts3 = wts.reshape(B * E, 1, cap)

    def smem_spec(shift):
        return pl.BlockSpec((1, 1, cap), lambda b, e, h: (b * E + jnp.clip(e + shift, 0, E - 1), 0, 0),
                            memory_space=pltpu.SMEM)

    n_half = 2
    fc = D_EXPERT // n_half
    w_in = lambda: pl.BlockSpec((1, D, fc), lambda b, e, h: (e, 0, h))
    hbm = lambda: pl.BlockSpec(memory_space=pl.ANY)
    return pl.pallas_call(
        functools.partial(_moe_kernel, cap=cap, n_exp=E),
        grid=(B, E, n_half),
        in_specs=[smem_spec(0), smem_spec(0), smem_spec(1), smem_spec(-1), smem_spec(-1), hbm(),
                  pl.BlockSpec((1, 1, modrows.shape[-1]), lambda b, e, h: (b, 0, 0)),
                  pl.BlockSpec((1, D), lambda b, e, h: (0, 0)),
                  w_in(), w_in(),
                  pl.BlockSpec((1, fc, D), lambda b, e, h: (e, h, 0)),
                  pl.BlockSpec((1, D), lambda b, e, h: (0, 0))],
        out_specs=hbm(),
        out_shape=jax.ShapeDtypeStruct((B, L, D), F32),
        scratch_shapes=[pltpu.VMEM((L, D), F32), pltpu.VMEM((L, D // 2), jnp.uint32),
                        pltpu.VMEM((cap, D // 2), jnp.uint32), pltpu.VMEM((cap, D), BF16),
                        pltpu.VMEM((cap, D), F32), pltpu.VMEM((cap, D), F32),
                        pltpu.SemaphoreType.DMA((1 + L // MOE_ROWS,))],
        compiler_params=_cparams(3, VMEM_LIMIT_BIG),
        name="ec_moe",
    )(idx3, wts3, idx3, idx3, wts3, x1, modrows, norm_ffn_g.reshape(1, D), wg, wu, wd, norm_final_g.reshape(1, D))


def _pos_tables():
    quarter = D_MODEL // 4
    omega = 1.0 / (POS_BASE ** (np.arange(quarter, dtype=np.float64) / quarter))
    n = np.arange(GRID_W, dtype=np.float64)[:, None] * omega[None, :]
    tab = np.concatenate([np.sin(n), np.cos(n)], axis=-1).astype(np.float32)
    return tab


def _filter_features(L):
    t = np.linspace(0.0, 1.0, L, dtype=np.float32).astype(np.float64)[:, None]
    w = 2.0 * math.pi * np.arange(L, dtype=np.float64)[:, None] / L
    f = np.linspace(1e-4, HY_BANDS - 1, HY_BANDS, dtype=np.float32).astype(np.float64)[None, :]
    z = np.concatenate([t, np.cos(f * w), -np.sin(f * w)], axis=-1)
    zp = np.zeros((L, LANES), np.float32)
    zp[:, :HY_EMB] = z.astype(np.float32)
    return zp


def kernel(x, c, ctx, c_ctx, w_ada, b_ada, norm_mix_g, w_in, gla_wa_f, gla_ba_f, gla_wa_b, gla_ba_b, gla_norm_g,
           hy_conv_w, hy_conv_b, hy_w1, hy_b1, hy_w2, hy_b2, hy_w3, hy_freq, hy_bias, w_out, norm_ffn_g, w_router,
           w_gate, w_up, w_down, norm_final_g):
    B, L, D = x.shape
    assert w_ada.shape[0] == 1 and D == D_MODEL and L % (GRID_W * 8) == 0
    l = 0
    cap = EC_FACTOR * L // N_EXPERTS

    cc = jnp.zeros((8, D), F32).at[:B].set(c).at[B].set(c_ctx)
    modrows = _adaln(cc, w_ada[l], b_ada[l]).reshape(8, 1, N_MOD * D)

    s = (GLA_QK_W, 2 * GLA_QK_W, 2 * GLA_QK_W + GLA_V_W, 2 * GLA_QK_W + 2 * GLA_V_W,
         2 * GLA_QK_W + 2 * GLA_V_W + GLA_RANK, 2 * GLA_QK_W + 2 * GLA_V_W + 2 * GLA_RANK)
    w = w_in[l]
    w_q, w_k, w_v, w_g, w_af, w_ab, w_hy = (w[:, :s[0]], w[:, s[0]:s[1]], w[:, s[1]:s[2]], w[:, s[2]:s[3]],
                                            w[:, s[3]:s[4]], w[:, s[4]:s[5]], w[:, s[5]:])
    w_a = jnp.concatenate([w_af, w_ab, jnp.zeros((D, LANES - 2 * GLA_RANK), F32)], axis=1)
    w_x = jnp.concatenate([w_q, w_k, w_v, w_g, w_hy, w_a], axis=1).astype(BF16)
    w_c = jnp.concatenate([w_k, w_v, w_a], axis=1).astype(BF16)
    wa = jnp.zeros((LANES, 2 * GLA_QK_W), F32)
    wa = wa.at[:GLA_RANK, :GLA_QK_W].set(gla_wa_f[l]).at[GLA_RANK:2 * GLA_RANK, GLA_QK_W:].set(gla_wa_b[l])
    wa = wa.astype(BF16)
    ba = jnp.concatenate([gla_ba_f[l], gla_ba_b[l]]).reshape(1, 2 * GLA_QK_W)

    tab = jnp.asarray(_pos_tables())
    x0, q, k, v, g, u, la_f, la_b = _project(
        x, modrows, None, norm_mix_g[l], w_x, wa, ba, (tab, tab),
        (GLA_QK_W, GLA_QK_W, GLA_V_W, GLA_V_W, 3 * HY_WIDTH), True, 1024,
        conv=(hy_conv_w[l], hy_conv_b[l].reshape(1, -1)))
    k_c, v_c, laf_c, lab_c = _project(ctx, modrows, B, norm_mix_g[l], w_c, wa, ba, None,
                                      (GLA_QK_W, GLA_V_W), False, ctx.shape[1])

    s_zero = jnp.zeros((B, GLA_V_W, GLA_QK_W), F32)
    nb_c = ctx.shape[1] // GLA_CHUNK
    _, s_f = _gla_pass(None, k_c, v_c, laf_c, s_zero, False, nb_c)
    _, s_b = _gla_pass(None, k_c, v_c, lab_c, s_zero, True, nb_c)
    o_f, _ = _gla_pass(q, k, v, la_f, s_f, False, 16)
    o_b, _ = _gla_pass(q, k, v, la_b, s_b, True, 16)

    w1p = jnp.zeros((LANES, HY_HIDDEN), F32).at[:HY_EMB].set(hy_w1[l])
    deltas = np.abs(np.linspace(HY_MIN_DECAY, HY_MAX_DECAY, HY_WIDTH, dtype=np.float32))
    decay_rates = jnp.asarray(deltas.reshape(1, -1))
    h = _hyena_filters(jnp.asarray(_filter_features(L)), w1p, hy_b1[l].reshape(1, -1), hy_w2[l],
                       hy_b2[l].reshape(1, -1), hy_w3[l], hy_freq[l].reshape(1, -1), decay_rates)
    fwd, inv, mats = _dft_tables()
    hspec = _filter_spectra(h, mats)

    nct = HY_WIDTH // 256
    z1 = _long_conv(u, 0, u, nct, hy_bias[l][0:1], fwd, inv, hspec, 0)
    y_hy = _long_conv(z1, 0, u, 2 * nct, hy_bias[l][1:2], fwd, inv, hspec, 1)

    x1, logits_t = _merge(o_f, o_b, g, y_hy, x0, modrows, gla_norm_g[l].reshape(1, GLA_DV),
                                   w_out[l].astype(BF16), norm_ffn_g[l].reshape(1, D),
                                   w_router[l].T.astype(BF16))
    idx, wts = _topk(logits_t, cap)
    return _moe(idx, wts, x1, modrows, norm_ffn_g[l], w_gate[l], w_up[l], w_down[l], norm_final_g)
```

```python
import functools
import math

import numpy as np
import jax
import jax.numpy as jnp
from jax import lax
from jax.experimental import pallas as pl
from jax.experimental.pallas import tpu as pltpu

F32 = jnp.float32
BF16 = jnp.bfloat16

D_MODEL = 1024
GRID_W = 64
EPS = 1e-6
POS_BASE = 10000.0
N_MOD = 6

GLA_HEADS = 4
GLA_DK = 64
GLA_DV = 128
GLA_RANK = 16
GLA_GATE_NORM = 16.0
GLA_CHUNK = 64
GLA_QK_W = GLA_HEADS * GLA_DK
GLA_V_W = GLA_HEADS * GLA_DV

HY_WIDTH = D_MODEL - GLA_V_W
HY_ORDER = 2
HY_BANDS = 16
HY_EMB = 2 * HY_BANDS + 1
HY_HIDDEN = 64
HY_SHORT = 3
HY_MIN_DECAY = math.log(1e-2) / 1.5
HY_MAX_DECAY = math.log(1e-2) / 0.3

N_EXPERTS = 16
EC_FACTOR = 2
D_EXPERT = 1024

VMEM_LIMIT_BIG = 56 * 1024 * 1024
VMEM_LIMIT_MID = 40 * 1024 * 1024
LANES = 128

CONV_T = 512
MOE_ROWS = 512


def _cparams(n_axes, vmem=VMEM_LIMIT_MID):
    return pltpu.CompilerParams(dimension_semantics=("arbitrary",) * n_axes, vmem_limit_bytes=vmem)


def _bdot(a, b):
    return jnp.dot(a.astype(BF16), b.astype(BF16), preferred_element_type=F32)


def _tree_sum(terms):
    while len(terms) > 1:
        terms = [a + b for a, b in zip(terms[::2], terms[1::2])] + ([terms[-1]] if len(terms) % 2 else [])
    return terms[0]


def _rms(x):
    return x * lax.rsqrt(jnp.mean(x * x, axis=-1, keepdims=True) + EPS)


def _mod_kernel(c_ref, w_ref, b_ref, o_ref):
    c = c_ref[...]
    s = c * jax.nn.sigmoid(c)
    o_ref[...] = _bdot(s, w_ref[...]) + b_ref[...]


def _adaln(cc, w_ada, b_ada):
    rows, d = cc.shape
    n = w_ada.shape[1]
    tn = 1536
    return pl.pallas_call(
        _mod_kernel,
        grid=(n // tn,),
        in_specs=[pl.BlockSpec((rows, d), lambda j: (0, 0)),
                  pl.BlockSpec((d, tn), lambda j: (0, j)),
                  pl.BlockSpec((1, tn), lambda j: (0, j))],
        out_specs=pl.BlockSpec((rows, tn), lambda j: (0, j)),
        out_shape=jax.ShapeDtypeStruct((rows, n), F32),
        compiler_params=_cparams(1),
        name="adaln",
    )(cc, w_ada, b_ada.reshape(1, n))


def _proj_kernel(*refs, with_pos, tm, widths, q_scale_first):
    if with_pos:
        x_ref, xp_ref, xn_ref, rt_ref, ct_ref, cw_ref, cb_ref, mod_ref, g_ref, w_ref, wa_ref, ba_ref = refs[:12]
        outs = refs[12:]
    else:
        x_ref, mod_ref, g_ref, w_ref, wa_ref, ba_ref = refs[:6]
        outs = refs[6:]
    x = x_ref[0]
    if with_pos:
        i = pl.program_id(0)
        n_tiles = pl.num_programs(0)
        rows_per_tile = tm // GRID_W
        n_rows = rt_ref.shape[0]
        halo = xp_ref.shape[1]
        r0 = pl.multiple_of(i * rows_per_tile, rows_per_tile)
        rt = rt_ref[pl.ds(r0, rows_per_tile), :]
        half = D_MODEL // 2
        x3 = x.reshape(rows_per_tile, GRID_W, D_MODEL)
        x3 = jnp.concatenate([x3[:, :, :half] + rt[:, None, :], x3[:, :, half:] + ct_ref[...][None]], axis=-1)
        x = x3.reshape(tm, D_MODEL)
        outs[0][0] = x
        outs = outs[1:]
        rt_prev = rt_ref[pl.ds(jnp.maximum(r0 - 1, 0), 1), :]
        rt_next = rt_ref[pl.ds(jnp.minimum(r0 + rows_per_tile, n_rows - 1), 1), :]
        x_prev = xp_ref[0] + jnp.concatenate(
            [jnp.broadcast_to(rt_prev, (halo, half)), ct_ref[GRID_W - halo:GRID_W, :]], axis=-1)
        x_next = xn_ref[0] + jnp.concatenate([jnp.broadcast_to(rt_next, (halo, half)), ct_ref[0:halo, :]], axis=-1)
        x = jnp.concatenate([x, x_prev, x_next], axis=0)
    shift = mod_ref[0, :, 0:D_MODEL]
    scale = mod_ref[0, :, D_MODEL:2 * D_MODEL]
    hn = (_rms(x) * g_ref[...]) * (1.0 + scale) + shift
    p_all = _bdot(hn, w_ref[...])
    p = p_all[:tm]
    off = 0
    for n, (o_ref, w) in enumerate(zip(outs[:-2], widths)):
        blk = p[:, off:off + w]
        if q_scale_first and n == 0:
            blk = blk * (GLA_DK ** -0.5)
        if with_pos and n == len(widths) - 1:
            before = jnp.where(i == 0, 0.0, p_all[tm + halo - 1:tm + halo, off:off + w])
            after = jnp.where(i == n_tiles - 1, 0.0, p_all[tm + halo:tm + halo + 1, off:off + w])
            t = lax.broadcasted_iota(jnp.int32, blk.shape, 0)
            up = jnp.where(t == 0, before, pltpu.roll(blk, 1, 0))
            dn = jnp.where(t == tm - 1, after, pltpu.roll(blk, tm - 1, 0))
            blk = up * cw_ref[0:1, :] + blk * cw_ref[1:2, :] + dn * cw_ref[2:3, :] + cb_ref[...]
        o_ref[0] = blk.astype(o_ref.dtype)
        off += w
    a = p[:, off:off + LANES]
    z = _bdot(a, wa_ref[...]) + ba_ref[...]
    la = jax.nn.log_sigmoid(z) / GLA_GATE_NORM
    outs[-2][0] = la[:, :GLA_QK_W]
    outs[-1][0] = la[:, GLA_QK_W:]


def _project(x, modrows, mod_row0, norm_g, w_cols, wa, ba, tabs, widths, q_scale_first, tm, conv=None):
    B, L, D = x.shape
    with_pos = tabs is not None
    ncols = w_cols.shape[1]
    in_specs = [pl.BlockSpec((1, tm, D), lambda i, b: (b, i, 0))]
    args = [x]
    if with_pos:
        rt, ct = tabs
        cw, cb = conv
        halo = 8
        per = tm // halo
        in_specs += [pl.BlockSpec((1, halo, D), lambda i, b: (b, jnp.maximum(i * per - 1, 0), 0)),
                     pl.BlockSpec((1, halo, D), lambda i, b: (b, jnp.minimum((i + 1) * per, L // halo - 1), 0)),
                     pl.BlockSpec(rt.shape, lambda i, b: (0, 0)), pl.BlockSpec(ct.shape, lambda i, b: (0, 0)),
                     pl.BlockSpec(cw.shape, lambda i, b: (0, 0)), pl.BlockSpec(cb.shape, lambda i, b: (0, 0))]
        args += [x, x, rt, ct, cw, cb]
    if mod_row0 is None:
        mod_map = lambda i, b: (b, 0, 0)
    else:
        mod_map = lambda i, b: (mod_row0, 0, 0)
    in_specs += [pl.BlockSpec((1, 1, modrows.shape[-1]), mod_map),
                 pl.BlockSpec((1, D), lambda i, b: (0, 0)),
                 pl.BlockSpec((D, ncols), lambda i, b: (0, 0)),
                 pl.BlockSpec(wa.shape, lambda i, b: (0, 0)),
                 pl.BlockSpec(ba.shape, lambda i, b: (0, 0))]
    args += [modrows, norm_g.reshape(1, D), w_cols, wa, ba]
    out_w = ([D] if with_pos else []) + list(widths) + [GLA_QK_W, GLA_QK_W]
    out_specs = [pl.BlockSpec((1, tm, w), lambda i, b: (b, i, 0)) for w in out_w]
    out_shape = [jax.ShapeDtypeStruct((B, L, w), F32) for w in out_w]
    v_pos = (1 if with_pos else 0) + list(widths).index(GLA_V_W)
    out_shape[v_pos] = jax.ShapeDtypeStruct((B, L, GLA_V_W), BF16)
    return pl.pallas_call(
        functools.partial(_proj_kernel, with_pos=with_pos, tm=tm, widths=tuple(widths), q_scale_first=q_scale_first),
        grid=(L // tm, B),
        in_specs=in_specs, out_specs=out_specs, out_shape=out_shape,
        compiler_params=_cparams(2, VMEM_LIMIT_BIG),
        name="project_x" if with_pos else "project_ctx",
    )(*args)


def _gla_kernel(*refs, reverse, with_output, nb):
    if with_output:
        q_ref, k_ref, v_ref, la_ref, s0_ref, o_ref, sfin_ref, s_scr = refs
    else:
        k_ref, v_ref, la_ref, s0_ref, sfin_ref, s_scr = refs
    C = GLA_CHUNK

    @pl.when(pl.program_id(1) == 0)
    def _():
        s_scr[...] = s0_ref[0]

    H = GLA_HEADS
    nt = (((1,), (1,)), ((), ()))
    tn = (((0,), (0,)), ((), ()))
    row = lax.broadcasted_iota(jnp.int32, (C, GLA_QK_W), 0)
    sc_r = lax.broadcasted_iota(jnp.int32, (C, H * C), 0)
    sc_s = lax.broadcasted_iota(jnp.int32, (C, H * C), 1) % C
    keep = (sc_r <= sc_s) if reverse else (sc_r >= sc_s)
    def head_mask(shape, rdiv, cdiv):
        return (lax.broadcasted_iota(jnp.int32, shape, 0) // rdiv) == (lax.broadcasted_iota(jnp.int32, shape, 1) // cdiv)
    m_k = head_mask((H * C, GLA_QK_W), C, GLA_DK)
    m_v = head_mask((H * C, GLA_V_W), C, GLA_DV)
    m_s = head_mask((GLA_V_W, GLA_QK_W), GLA_DV, GLA_DK)

    def prefix_sum(x):
        s = 1
        while s < C:
            if reverse:
                x = x + jnp.where(row < C - s, pltpu.roll(x, C - s, 0), 0.0)
            else:
                x = x + jnp.where(row >= s, pltpu.roll(x, s, 0), 0.0)
            s *= 2
        return x

    order = range(nb - 1, -1, -1) if reverse else range(nb)
    for ci in order:
        sl = slice(ci * C, (ci + 1) * C)
        k = k_ref[0, sl, :]
        v = v_ref[0, sl, :]
        b = prefix_sum(la_ref[0, sl, :])
        last = 0 if reverse else C - 1
        b_tot = b[last:last + 1, :]
        kdec = (k * jnp.exp(b_tot - b)).astype(BF16)
        vb = v.astype(BF16)
        upd_t = jnp.where(m_s, lax.dot_general(vb, kdec, tn, preferred_element_type=F32), 0.0)
        st = s_scr[...]
        if with_output:
            q = q_ref[0, sl, :]
            mid = C // 2 - 1 if reverse else C // 2
            ref = b[mid:mid + 1, :]
            qs = (q * jnp.exp(b - ref)).astype(BF16)
            ks = (k * jnp.exp(ref - b)).astype(BF16)
            qb = (q * jnp.exp(b)).astype(BF16)
            o_inter = lax.dot_general(qb, st.astype(BF16), nt, preferred_element_type=F32)
            ks4 = jnp.where(m_k, jnp.concatenate([ks] * H, axis=0), jnp.zeros((), BF16))
            sc = lax.dot_general(qs, ks4, nt, preferred_element_type=F32)
            sc = jnp.where(keep, sc, 0.0).astype(BF16)
            v4 = jnp.where(m_v, jnp.concatenate([vb] * H, axis=0), jnp.zeros((), BF16))
            o_ref[0, sl, :] = o_inter + jnp.dot(sc, v4, preferred_element_type=F32)
        s_scr[...] = st * jnp.exp(b_tot) + upd_t
    sfin_ref[0] = s_scr[...]


def _gla_pass(q, k, v, la, s0, reverse, nb):
    B, L, _ = k.shape
    with_output = q is not None
    tm = nb * GLA_CHUNK
    nblk = L // tm
    if reverse:
        tmap = lambda b, i: (b, nblk - 1 - i, 0)
    else:
        tmap = lambda b, i: (b, i, 0)
    smap = lambda b, i: (b, 0, 0)
    in_specs, args = [], []
    if with_output:
        in_specs.append(pl.BlockSpec((1, tm, GLA_QK_W), tmap))
        args.append(q)
    in_specs += [pl.BlockSpec((1, tm, GLA_QK_W), tmap), pl.BlockSpec((1, tm, GLA_V_W), tmap),
                 pl.BlockSpec((1, tm, GLA_QK_W), tmap), pl.BlockSpec((1, GLA_V_W, GLA_QK_W), smap)]
    args += [k, v, la, s0]
    out_specs, out_shape = [], []
    if with_output:
        out_specs.append(pl.BlockSpec((1, tm, GLA_V_W), tmap))
        out_shape.append(jax.ShapeDtypeStruct((B, L, GLA_V_W), F32))
    out_specs.append(pl.BlockSpec((1, GLA_V_W, GLA_QK_W), smap))
    out_shape.append(jax.ShapeDtypeStruct((B, GLA_V_W, GLA_QK_W), F32))
    res = pl.pallas_call(
        functools.partial(_gla_kernel, reverse=reverse, with_output=with_output, nb=nb),
        grid=(B, nblk),
        in_specs=in_specs, out_specs=out_specs, out_shape=out_shape,
        scratch_shapes=[pltpu.VMEM((GLA_V_W, GLA_QK_W), F32)],
        compiler_params=_cparams(2),
        name=("gla_" + ("bwd" if reverse else "fwd") + ("" if with_output else "_state")),
    )(*args)
    return (res[0], res[1]) if with_output else (None, res[0])


def _filter_kernel(z_ref, w1_ref, b1_ref, w2_ref, b2_ref, w3_ref, fr_ref, dl_ref, o_ref, *, tm, L):
    hi = lax.Precision.HIGHEST
    fr = fr_ref[...]
    h1 = jnp.sin(fr * (jnp.dot(z_ref[...], w1_ref[...], precision=hi, preferred_element_type=F32) + b1_ref[...]))
    h2 = jnp.sin(fr * (jnp.dot(h1, w2_ref[...], precision=hi, preferred_element_type=F32) + b2_ref[...]))
    h = _bdot(h2, w3_ref[...])
    t = z_ref[:, 0:1]
    decay = jnp.exp(-t * dl_ref[...])
    o_ref[...] = (h * jnp.concatenate([decay] * (h.shape[1] // decay.shape[1]), axis=-1)).astype(o_ref.dtype)


def _hyena_filters(zfeat, w1p, b1, w2, b2, w3, freq, decay_rates):
    L = zfeat.shape[0]
    tm = 512
    n = w3.shape[1]
    full = lambda a: pl.BlockSpec(a.shape, lambda i: (0,) * a.ndim)
    return pl.pallas_call(
        functools.partial(_filter_kernel, tm=tm, L=L),
        grid=(L // tm,),
        in_specs=[pl.BlockSpec((tm, zfeat.shape[1]), lambda i: (i, 0)), full(w1p), full(b1), full(w2), full(b2),
                  full(w3), full(freq), full(decay_rates)],
        out_specs=pl.BlockSpec((tm, n), lambda i: (i, 0)),
        out_shape=jax.ShapeDtypeStruct((L, n), BF16),
        compiler_params=_cparams(1),
        name="hyena_filters",
    )(zfeat, w1p, b1, w2, b2, w3, freq, decay_rates)


def _dft_tables():
    T = CONV_T
    f = np.arange(T, dtype=np.float64)[:, None] + 0.5
    n = np.arange(T, dtype=np.float64)[None, :]
    th = 2.0 * np.pi * f * n / (2 * T)
    thn = 2.0 * np.pi * f * (n - T) / (2 * T)
    fwd = np.concatenate([np.cos(th), -np.sin(th)], axis=0)
    inv = np.concatenate([np.cos(th).T, -np.sin(th).T], axis=1) / T
    nz = np.ones((1, T)); nz[0, 0] = 0.0
    P = np.concatenate([np.cos(th), -np.sin(th)], axis=0)
    Pc = np.concatenate([np.cos(th), np.sin(th)], axis=0)
    N = np.concatenate([np.cos(thn), -np.sin(thn)], axis=0) * nz
    Nc = np.concatenate([np.cos(thn), np.sin(thn)], axis=0) * nz
    m_pos = np.concatenate([N, P], axis=1)
    m_zero = np.concatenate([Pc * nz, P], axis=1)
    m_neg = np.concatenate([Nc, Pc], axis=1)
    mats = np.stack([m_pos, m_zero, m_neg], axis=0)
    as_bf16 = lambda a: jnp.asarray(a, dtype=F32).astype(BF16)
    return as_bf16(fwd), as_bf16(inv), as_bf16(mats)


def _spec_kernel(m_ref, a_ref, b_ref, o_ref):
    T = CONV_T
    o_ref[0, 0] = _bdot(m_ref[0, :, :T], a_ref[...]) + _bdot(m_ref[0, :, T:], b_ref[...])


def _filter_spectra(h, mats):
    L = h.shape[0]
    T = CONV_T
    J = L // T
    W = HY_WIDTH
    ct = W
    nct = W // ct
    nlag = 2 * J - 1

    def sel(dl):
        d = dl - (J - 1)
        return jnp.where(d > 0, 0, jnp.where(d == 0, 1, 2))

    def a_map(o, dl, c):
        d = dl - (J - 1)
        row = jnp.where(d > 0, d - 1, jnp.where(d == 0, 0, -d - 1))
        dirn = jnp.where(d > 0, 0, 1)
        return (row, (o * 2 + dirn) * nct + c)

    def b_map(o, dl, c):
        d = dl - (J - 1)
        row = jnp.where(d >= 0, d, -d)
        dirn = jnp.where(d >= 0, 0, 1)
        return (row, (o * 2 + dirn) * nct + c)

    return pl.pallas_call(
        _spec_kernel,
        grid=(HY_ORDER, nlag, nct),
        in_specs=[pl.BlockSpec((1, 2 * T, 2 * T), lambda o, dl, c: (sel(dl), 0, 0)),
                  pl.BlockSpec((T, ct), a_map),
                  pl.BlockSpec((T, ct), b_map)],
        out_specs=pl.BlockSpec((1, 1, 2 * T, ct), lambda o, dl, c: (o, dl, 0, c)),
        out_shape=jax.ShapeDtypeStruct((HY_ORDER, nlag, 2 * T, W), F32),
        compiler_params=_cparams(3),
        name="filter_spectra",
    )(mats, h, h)


def _lconv_kernel(u_ref, gate_ref, bias_ref, fwd_ref, inv_ref, h_ref, o_ref, u_scr, y_scr, *, J, ct, GI):
    T = CONV_T
    g = pl.program_id(2)

    @pl.when(g == 0)
    def _():
        for j in range(J):
            u_scr[j] = _bdot(fwd_ref[...], u_ref[0, j * T:(j + 1) * T, :])

    def run_group(i0):
        def tile_body(rt, carry):
            r0 = pl.multiple_of(rt * 8, 8)
            for lt in range(ct // LANES):
                cs = slice(lt * LANES, (lt + 1) * LANES)
                ure = [u_scr[j, pl.ds(r0, 8), cs] for j in range(J)]
                uim = [u_scr[j, pl.ds(T + r0, 8), cs] for j in range(J)]
                for ii in range(GI):
                    re_terms, im_terms = [], []
                    for j in range(J):
                        lag = i0 + ii - j + (J - 1)
                        hre = h_ref[0, lag, pl.ds(r0, 8), cs]
                        him = h_ref[0, lag, pl.ds(T + r0, 8), cs]
                        re_terms.append(hre * ure[j] - him * uim[j])
                        im_terms.append(hre * uim[j] + him * ure[j])
                    y_scr[ii, pl.ds(r0, 8), cs] = _tree_sum(re_terms)
                    y_scr[ii, pl.ds(T + r0, 8), cs] = _tree_sum(im_terms)
            return carry

        lax.fori_loop(0, T // 8, tile_body, 0)
        for ii in range(GI):
            y = _bdot(inv_ref[...], y_scr[ii])
            ui = u_ref[0, (i0 + ii) * T:(i0 + ii + 1) * T, :]
            o_ref[0, ii * T:(ii + 1) * T, :] = gate_ref[0, ii * T:(ii + 1) * T, :] * (y + ui * bias_ref[...])

    for gg in range(J // GI):
        pl.when(g == gg)(functools.partial(run_group, gg * GI))


def _long_conv(u_arr, u_blk0, gate_arr, gate_blk0, bias_row, fwd, inv, hspec, order):
    B, L, _ = u_arr.shape
    T = CONV_T
    J = L // T
    ct = 256
    nct = HY_WIDTH // ct
    nlag = 2 * J - 1
    GI = 4
    return pl.pallas_call(
        functools.partial(_lconv_kernel, J=J, ct=ct, GI=GI),
        grid=(nct, B, J // GI),
        in_specs=[pl.BlockSpec((1, L, ct), lambda c, b, i: (b, 0, u_blk0 + c)),
                  pl.BlockSpec((1, GI * T, ct), lambda c, b, i: (b, i, gate_blk0 + c)),
                  pl.BlockSpec((1, ct), lambda c, b, i: (0, c)),
                  pl.BlockSpec(fwd.shape, lambda c, b, i: (0, 0)),
                  pl.BlockSpec(inv.shape, lambda c, b, i: (0, 0)),
                  pl.BlockSpec((1, nlag, 2 * T, ct), lambda c, b, i: (order, 0, 0, c),
                               pipeline_mode=pl.Buffered(1))],
        out_specs=pl.BlockSpec((1, GI * T, ct), lambda c, b, i: (b, i, c)),
        out_shape=jax.ShapeDtypeStruct((B, L, HY_WIDTH), F32),
        scratch_shapes=[pltpu.VMEM((J, 2 * T, ct), F32), pltpu.VMEM((GI, 2 * T, ct), F32)],
        compiler_params=_cparams(3, VMEM_LIMIT_BIG),
        name="hyena_long_conv%d" % order,
    )(u_arr, gate_arr, bias_row, fwd, inv, hspec)


def _merge_kernel(of_ref, ob_ref, g_ref, hy_ref, x0_ref, mod_ref, gn_ref, wout_ref, fg_ref, wr_ref,
                  x1_ref, lg_ref):
    o = of_ref[0] + ob_ref[0]
    g = g_ref[0]
    parts = []
    for h in range(GLA_HEADS):
        oh = o[:, h * GLA_DV:(h + 1) * GLA_DV]
        parts.append(_rms(oh) * gn_ref[...])
    y_gla = jnp.concatenate(parts, axis=-1) * (g * jax.nn.sigmoid(g))
    ycat = jnp.concatenate([y_gla, hy_ref[0]], axis=-1)
    m = _bdot(ycat, wout_ref[...])
    gate = mod_ref[0, :, 2 * D_MODEL:3 * D_MODEL]
    x1 = x0_ref[0] + gate * m
    x1_ref[0] = x1
    shift = mod_ref[0, :, 3 * D_MODEL:4 * D_MODEL]
    scale = mod_ref[0, :, 4 * D_MODEL:5 * D_MODEL]
    hn = ((_rms(x1) * fg_ref[...]) * (1.0 + scale) + shift).astype(BF16)
    lg_ref[0] = lax.dot_general(wr_ref[...], hn, (((1,), (1,)), ((), ())), preferred_element_type=F32)


def _merge(o_f, o_b, g, y_hy, x0, modrows, gla_norm_g, w_out, norm_ffn_g, w_router_t):
    B, L, D = x0.shape
    tm = 1024
    tmap = lambda b, i: (b, i, 0)
    full2 = lambda a: pl.BlockSpec(a.shape, lambda b, i: (0, 0))
    return pl.pallas_call(
        _merge_kernel,
        grid=(B, L // tm),
        in_specs=[pl.BlockSpec((1, tm, GLA_V_W), tmap), pl.BlockSpec((1, tm, GLA_V_W), tmap),
                  pl.BlockSpec((1, tm, GLA_V_W), tmap), pl.BlockSpec((1, tm, HY_WIDTH), tmap),
                  pl.BlockSpec((1, tm, D), tmap),
                  pl.BlockSpec((1, 1, modrows.shape[-1]), lambda b, i: (b, 0, 0)),
                  full2(gla_norm_g), full2(w_out), full2(norm_ffn_g), full2(w_router_t)],
        out_specs=[pl.BlockSpec((1, tm, D), tmap), pl.BlockSpec((1, N_EXPERTS, tm), lambda b, i: (b, 0, i))],
        out_shape=[jax.ShapeDtypeStruct((B, L, D), F32), jax.ShapeDtypeStruct((B, N_EXPERTS, L), F32)],
        compiler_params=_cparams(2, VMEM_LIMIT_BIG),
        name="merge_router",
    )(o_f, o_b, g, y_hy, x0, modrows, gla_norm_g, w_out, norm_ffn_g, w_router_t)


def _topk_kernel(lg_ref, idx_ref, wt_ref, pos_scr, *, cap):
    E, L = lg_ref.shape[1], lg_ref.shape[2]
    lg = lg_ref[0]
    mx = jnp.max(lg, axis=0, keepdims=True)
    ex = jnp.exp(lg - mx)
    aff = ex / jnp.sum(ex, axis=0, keepdims=True)

    def count(mask):
        return jnp.sum(mask.astype(jnp.int32), axis=1, keepdims=True)

    def tbody(n, cur):
        cand = cur | lax.shift_left(jnp.int32(1), 30 - n)
        return jnp.where(count(aff >= lax.bitcast_convert_type(cand, F32)) >= cap, cand, cur)

    thr = lax.bitcast_convert_type(lax.fori_loop(0, 31, tbody, jnp.zeros((E, 1), jnp.int32)), F32)
    gt = aff > thr
    eq = aff == thr
    need = cap - count(gt)
    tok = lax.broadcasted_iota(jnp.int32, (E, L), 1)

    def ibody(n, cur):
        cand = cur + lax.shift_left(jnp.int32(1), 12 - n)
        ok = (cand <= L) & (count(eq & (tok < cand)) <= need)
        return jnp.where(ok, cand, cur)

    bound = lax.fori_loop(0, 13, ibody, jnp.zeros((E, 1), jnp.int32))
    sel = gt | (eq & (tok < bound))

    li = lax.broadcasted_iota(jnp.int32, (LANES, LANES), 0)
    lj = lax.broadcasted_iota(jnp.int32, (LANES, LANES), 1)
    ustrict = (li < lj).astype(BF16)
    off = jnp.zeros((E, 1), F32)
    for r in range(L // LANES):
        s_r = sel[:, r * LANES:(r + 1) * LANES]
        sf = s_r.astype(F32)
        pre = jnp.dot(sf.astype(BF16), ustrict, preferred_element_type=F32)
        pos_scr[:, r * LANES:(r + 1) * LANES] = jnp.where(s_r, pre + off, -1.0)
        off = off + jnp.sum(sf, axis=1, keepdims=True)

    slot = lax.broadcasted_iota(jnp.int32, (cap, L), 0).astype(F32)
    tokf = lax.broadcasted_iota(jnp.int32, (8, L), 1)
    row = lax.broadcasted_iota(jnp.int32, (8, L), 0)
    t_hi = (tokf // 64).astype(F32)
    t_lo = (tokf % 64).astype(F32)
    for e in range(E):
        a = aff[e:e + 1, :]
        a1 = a.astype(BF16).astype(F32)
        a2 = (a - a1).astype(BF16).astype(F32)
        a3 = a - a1 - a2
        vals = jnp.where(row == 0, t_hi, jnp.where(row == 1, t_lo, jnp.where(row == 2, a1, jnp.where(
            row == 3, a2, jnp.where(row == 4, a3, 0.0)))))
        onehot = jnp.where(pos_scr[e:e + 1, :] == slot, 1.0, 0.0).astype(BF16)
        res = lax.dot_general(vals.astype(BF16), onehot, (((1,), (1,)), ((), ())), preferred_element_type=F32)
        idx_ref[0, e:e + 1, :] = (res[0:1, :] * 64.0 + res[1:2, :]).astype(jnp.int32)
        wt_ref[0, e:e + 1, :] = res[2:3, :] + res[3:4, :] + res[4:5, :]


def _topk(logits_t, cap):
    B, E, L = logits_t.shape
    return pl.pallas_call(
        functools.partial(_topk_kernel, cap=cap),
        grid=(B,),
        in_specs=[pl.BlockSpec((1, E, L), lambda b: (b, 0, 0))],
        out_specs=[pl.BlockSpec((1, E, cap), lambda b: (b, 0, 0)), pl.BlockSpec((1, E, cap), lambda b: (b, 0, 0))],
        out_shape=[jax.ShapeDtypeStruct((B, E, cap), jnp.int32), jax.ShapeDtypeStruct((B, E, cap), F32)],
        scratch_shapes=[pltpu.VMEM((E, L), F32)],
        compiler_params=_cparams(1),
        name="ec_topk",
    )(logits_t)


def _moe_kernel(idxc_ref, wtc_ref, idxn_ref, idxp_ref, wtp_ref, x1_hbm, mod_ref, ng_ref, wg_hbm, wu_hbm, wd_hbm,
                fg_ref, out_hbm, acc, tab, xsp, xs_scr, y_part, y_fin, wg_buf, wu_buf, wd_buf, sem, wsem, *,
                cap, n_exp):
    b = pl.program_id(0)
    e = pl.program_id(1)
    hf = pl.program_id(2)
    n_slots = wg_buf.shape[0]
    fc = wg_buf.shape[2]
    step = (b * n_exp + e) * 2 + hf
    n_steps = pl.num_programs(0) * n_exp * 2
    slot = step % n_slots

    def weight_copies(t):
        te, th, ts = (t // 2) % n_exp, t % 2, t % n_slots
        cols = pl.ds(th * fc if isinstance(th, int) else pl.multiple_of(th * fc, fc), fc)
        return (pltpu.make_async_copy(wg_hbm.at[te, :, cols], wg_buf.at[ts], wsem.at[ts, 0]),
                pltpu.make_async_copy(wu_hbm.at[te, :, cols], wu_buf.at[ts], wsem.at[ts, 1]),
                pltpu.make_async_copy(wd_hbm.at[te, cols, :], wd_buf.at[ts], wsem.at[ts, 2]))

    @pl.when(step == 0)
    def _():
        for t in range(n_slots - 1):
            for cp in weight_copies(t):
                cp.start()

    @pl.when(step + n_slots - 1 < n_steps)
    def _():
        for cp in weight_copies(step + n_slots - 1):
            cp.start()

    for cp in weight_copies(step):
        cp.wait()
    L = acc.shape[0]
    rows = MOE_ROWS
    n_ch = L // rows
    group = 8
    half = D_MODEL // 2

    def chunk(n):
        return pl.ds(n * rows if isinstance(n, int) else pl.multiple_of(n * rows, rows), rows)

    def in_copy(n):
        return pltpu.make_async_copy(x1_hbm.at[b, chunk(n)], acc.at[chunk(n)], sem.at[1 + n])

    def out_copy(n):
        return pltpu.make_async_copy(acc.at[chunk(n)], out_hbm.at[b, chunk(n)], sem.at[0])

    def gather_row(idx_ref, i):
        xsp[pl.ds(i, 1), :] = tab[pl.ds(idx_ref[0, 0, i], 1), :]

    def scatter_group(idx_ref, wt_ref, i0):
        ts = [idx_ref[0, 0, i0 + r] for r in range(group)]
        new = [acc[pl.ds(ts[r], 1), :] + y_fin[pl.ds(i0 + r, 1), :] * wt_ref[0, 0, i0 + r] for r in range(group)]
        for r in range(group):
            acc[pl.ds(ts[r], 1), :] = new[r]

    def ffn_part():
        xs = xs_scr[...]
        y = None
        for f0 in range(0, fc, fc // 2):
            fs = slice(f0, f0 + fc // 2)
            gte = _bdot(xs, wg_buf[slot, :, fs])
            up = _bdot(xs, wu_buf[slot, :, fs])
            part = _bdot(gte * jax.nn.sigmoid(gte) * up, wd_buf[slot, fs, :])
            y = part if y is None else y + part
        return y

    def unpack_rows():
        xp = xsp[...]
        lo = pltpu.unpack_elementwise(xp, index=0, packed_dtype=BF16, unpacked_dtype=F32)
        hi = pltpu.unpack_elementwise(xp, index=1, packed_dtype=BF16, unpacked_dtype=F32)
        xs_scr[...] = jnp.concatenate([lo, hi], axis=-1).astype(BF16)

    def scatter_prev():
        for i0 in range(0, cap, group):
            scatter_group(idxp_ref, wtp_ref, i0)

    def gather_next():
        for i in range(cap):
            gather_row(idxn_ref, i)

    mod_gate = lambda: mod_ref[0, :, 5 * D_MODEL:6 * D_MODEL]

    @pl.when((e == 0) & (hf == 0))
    def _():
        for n in range(n_ch):
            in_copy(n).start()
        shift = mod_ref[0, :, 3 * D_MODEL:4 * D_MODEL]
        scale = mod_ref[0, :, 4 * D_MODEL:5 * D_MODEL]

        def norm(n, carry):
            in_copy(n).wait()
            r0 = pl.multiple_of(n * rows, rows)
            hn = (_rms(acc[pl.ds(r0, rows), :]) * ng_ref[...]) * (1.0 + scale) + shift
            tab[pl.ds(r0, rows), :] = pltpu.pack_elementwise([hn[:, :half], hn[:, half:]], packed_dtype=BF16)
            return carry

        lax.fori_loop(0, n_ch, norm, 0)

        def gather0(gi, carry):
            i0 = pl.multiple_of(gi * group, group)
            for r in range(group):
                gather_row(idxc_ref, i0 + r)
            return carry

        lax.fori_loop(0, cap // group, gather0, 0)
        unpack_rows()
        y_part[...] = ffn_part()

    @pl.when((e > 0) & (hf == 0))
    def _():
        scatter_prev()
        y_part[...] = ffn_part()

    @pl.when(hf == 1)
    def _():
        gather_next()
        y_fin[...] = (y_part[...] + ffn_part()) * mod_gate()
        unpack_rows()

    @pl.when((e == n_exp - 1) & (hf == 1))
    def _():
        def scatter_last(gi, carry):
            scatter_group(idxc_ref, wtc_ref, pl.multiple_of(gi * group, group))
            return carry

        lax.fori_loop(0, cap // group, scatter_last, 0)

        def fin(n, carry):
            r0 = pl.multiple_of(n * rows, rows)
            acc[pl.ds(r0, rows), :] = _rms(acc[pl.ds(r0, rows), :]) * fg_ref[...]
            out_copy(n).start()
            return carry

        lax.fori_loop(0, n_ch, fin, 0)
        for n in range(n_ch):
            out_copy(n).wait()


def _moe(idx, wts, x1, modrows, norm_ffn_g, wg, wu, wd, norm_final_g):
    B, L, D = x1.shape
    E, cap = idx.shape[1], idx.shape[2]
    idx3 = idx.reshape(B * E, 1, cap)
    wts3 = wts.reshape(B * E, 1, cap)

    def smem_spec(shift):
        return pl.BlockSpec((1, 1, cap), lambda b, e, h: (b * E + jnp.clip(e + shift, 0, E - 1), 0, 0),
                            memory_space=pltpu.SMEM)

    n_half = 2
    fc = D_EXPERT // n_half
    n_slots = 3
    hbm = lambda: pl.BlockSpec(memory_space=pl.ANY)
    return pl.pallas_call(
        functools.partial(_moe_kernel, cap=cap, n_exp=E),
        grid=(B, E, n_half),
        in_specs=[smem_spec(0), smem_spec(0), smem_spec(1), smem_spec(-1), smem_spec(-1), hbm(),
                  pl.BlockSpec((1, 1, modrows.shape[-1]), lambda b, e, h: (b, 0, 0)),
                  pl.BlockSpec((1, D), lambda b, e, h: (0, 0)),
                  hbm(), hbm(), hbm(),
                  pl.BlockSpec((1, D), lambda b, e, h: (0, 0))],
        out_specs=hbm(),
        out_shape=jax.ShapeDtypeStruct((B, L, D), F32),
        scratch_shapes=[pltpu.VMEM((L, D), F32), pltpu.VMEM((L, D // 2), jnp.uint32),
                        pltpu.VMEM((cap, D // 2), jnp.uint32), pltpu.VMEM((cap, D), BF16),
                        pltpu.VMEM((cap, D), F32), pltpu.VMEM((cap, D), F32),
                        pltpu.VMEM((n_slots, D, fc), F32), pltpu.VMEM((n_slots, D, fc), F32),
                        pltpu.VMEM((n_slots, fc, D), F32),
                        pltpu.SemaphoreType.DMA((1 + L // MOE_ROWS,)), pltpu.SemaphoreType.DMA((n_slots, 3))],
        compiler_params=_cparams(3, VMEM_LIMIT_BIG),
        name="ec_moe",
    )(idx3, wts3, idx3, idx3, wts3, x1, modrows, norm_ffn_g.reshape(1, D), wg, wu, wd, norm_final_g.reshape(1, D))


def _pos_tables():
    quarter = D_MODEL // 4
    omega = 1.0 / (POS_BASE ** (np.arange(quarter, dtype=np.float64) / quarter))
    n = np.arange(GRID_W, dtype=np.float64)[:, None] * omega[None, :]
    tab = np.concatenate([np.sin(n), np.cos(n)], axis=-1).astype(np.float32)
    return tab


def _filter_features(L):
    t = np.linspace(0.0, 1.0, L, dtype=np.float32).astype(np.float64)[:, None]
    w = 2.0 * math.pi * np.arange(L, dtype=np.float64)[:, None] / L
    f = np.linspace(1e-4, HY_BANDS - 1, HY_BANDS, dtype=np.float32).astype(np.float64)[None, :]
    z = np.concatenate([t, np.cos(f * w), -np.sin(f * w)], axis=-1)
    zp = np.zeros((L, LANES), np.float32)
    zp[:, :HY_EMB] = z.astype(np.float32)
    return zp


def kernel(x, c, ctx, c_ctx, w_ada, b_ada, norm_mix_g, w_in, gla_wa_f, gla_ba_f, gla_wa_b, gla_ba_b, gla_norm_g,
           hy_conv_w, hy_conv_b, hy_w1, hy_b1, hy_w2, hy_b2, hy_w3, hy_freq, hy_bias, w_out, norm_ffn_g, w_router,
           w_gate, w_up, w_down, norm_final_g):
    B, L, D = x.shape
    assert w_ada.shape[0] == 1 and D == D_MODEL and L % (GRID_W * 8) == 0
    l = 0
    cap = EC_FACTOR * L // N_EXPERTS

    cc = jnp.zeros((8, D), F32).at[:B].set(c).at[B].set(c_ctx)
    modrows = _adaln(cc, w_ada[l], b_ada[l]).reshape(8, 1, N_MOD * D)

    s = (GLA_QK_W, 2 * GLA_QK_W, 2 * GLA_QK_W + GLA_V_W, 2 * GLA_QK_W + 2 * GLA_V_W,
         2 * GLA_QK_W + 2 * GLA_V_W + GLA_RANK, 2 * GLA_QK_W + 2 * GLA_V_W + 2 * GLA_RANK)
    w = w_in[l]
    w_q, w_k, w_v, w_g, w_af, w_ab, w_hy = (w[:, :s[0]], w[:, s[0]:s[1]], w[:, s[1]:s[2]], w[:, s[2]:s[3]],
                                            w[:, s[3]:s[4]], w[:, s[4]:s[5]], w[:, s[5]:])
    w_a = jnp.concatenate([w_af, w_ab, jnp.zeros((D, LANES - 2 * GLA_RANK), F32)], axis=1)
    w_x = jnp.concatenate([w_q, w_k, w_v, w_g, w_hy, w_a], axis=1).astype(BF16)
    w_c = jnp.concatenate([w_k, w_v, w_a], axis=1).astype(BF16)
    wa = jnp.zeros((LANES, 2 * GLA_QK_W), F32)
    wa = wa.at[:GLA_RANK, :GLA_QK_W].set(gla_wa_f[l]).at[GLA_RANK:2 * GLA_RANK, GLA_QK_W:].set(gla_wa_b[l])
    wa = wa.astype(BF16)
    ba = jnp.concatenate([gla_ba_f[l], gla_ba_b[l]]).reshape(1, 2 * GLA_QK_W)

    tab = jnp.asarray(_pos_tables())
    x0, q, k, v, g, u, la_f, la_b = _project(
        x, modrows, None, norm_mix_g[l], w_x, wa, ba, (tab, tab),
        (GLA_QK_W, GLA_QK_W, GLA_V_W, GLA_V_W, 3 * HY_WIDTH), True, 1024,
        conv=(hy_conv_w[l], hy_conv_b[l].reshape(1, -1)))
    k_c, v_c, laf_c, lab_c = _project(ctx, modrows, B, norm_mix_g[l], w_c, wa, ba, None,
                                      (GLA_QK_W, GLA_V_W), False, ctx.shape[1])

    s_zero = jnp.zeros((B, GLA_V_W, GLA_QK_W), F32)
    nb_c = ctx.shape[1] // GLA_CHUNK
    _, s_f = _gla_pass(None, k_c, v_c, laf_c, s_zero, False, nb_c)
    _, s_b = _gla_pass(None, k_c, v_c, lab_c, s_zero, True, nb_c)
    o_f, _ = _gla_pass(q, k, v, la_f, s_f, False, 16)
    o_b, _ = _gla_pass(q, k, v, la_b, s_b, True, 16)

    w1p = jnp.zeros((LANES, HY_HIDDEN), F32).at[:HY_EMB].set(hy_w1[l])
    deltas = np.abs(np.linspace(HY_MIN_DECAY, HY_MAX_DECAY, HY_WIDTH, dtype=np.float32))
    decay_rates = jnp.asarray(deltas.reshape(1, -1))
    h = _hyena_filters(jnp.asarray(_filter_features(L)), w1p, hy_b1[l].reshape(1, -1), hy_w2[l],
                       hy_b2[l].reshape(1, -1), hy_w3[l], hy_freq[l].reshape(1, -1), decay_rates)
    fwd, inv, mats = _dft_tables()
    hspec = _filter_spectra(h, mats)

    nct = HY_WIDTH // 256
    z1 = _long_conv(u, 0, u, nct, hy_bias[l][0:1], fwd, inv, hspec, 0)
    y_hy = _long_conv(z1, 0, u, 2 * nct, hy_bias[l][1:2], fwd, inv, hspec, 1)

    x1, logits_t = _merge(o_f, o_b, g, y_hy, x0, modrows, gla_norm_g[l].reshape(1, GLA_DV),
                                   w_out[l].astype(BF16), norm_ffn_g[l].reshape(1, D),
                                   w_router[l].T.astype(BF16))
    idx, wts = _topk(logits_t, cap)
    return _moe(idx, wts, x1, modrows, norm_ffn_g[l], w_gate[l], w_up[l], w_down[l], norm_final_g)
```

```python
import functools
import math

import numpy as np
import jax
import jax.numpy as jnp
from jax import lax
from jax.experimental import pallas as pl
from jax.experimental.pallas import tpu as pltpu

F32 = jnp.float32
BF16 = jnp.bfloat16

D_MODEL = 1024
GRID_W = 64
EPS = 1e-6
POS_BASE = 10000.0
N_MOD = 6

GLA_HEADS = 4
GLA_DK = 64
GLA_DV = 128
GLA_RANK = 16
GLA_GATE_NORM = 16.0
GLA_CHUNK = 64
GLA_QK_W = GLA_HEADS * GLA_DK
GLA_V_W = GLA_HEADS * GLA_DV

HY_WIDTH = D_MODEL - GLA_V_W
HY_ORDER = 2
HY_BANDS = 16
HY_EMB = 2 * HY_BANDS + 1
HY_HIDDEN = 64
HY_SHORT = 3
HY_MIN_DECAY = math.log(1e-2) / 1.5
HY_MAX_DECAY = math.log(1e-2) / 0.3

N_EXPERTS = 16
EC_FACTOR = 2
D_EXPERT = 1024

VMEM_LIMIT_BIG = 56 * 1024 * 1024
VMEM_LIMIT_MID = 40 * 1024 * 1024
LANES = 128

CONV_T = 512
MOE_ROWS = 512


def _cparams(n_axes, vmem=VMEM_LIMIT_MID):
    return pltpu.CompilerParams(dimension_semantics=("arbitrary",) * n_axes, vmem_limit_bytes=vmem)


def _bdot(a, b):
    return jnp.dot(a.astype(BF16), b.astype(BF16), preferred_element_type=F32)


def _tree_sum(terms):
    while len(terms) > 1:
        terms = [a + b for a, b in zip(terms[::2], terms[1::2])] + ([terms[-1]] if len(terms) % 2 else [])
    return terms[0]


def _rms(x):
    return x * lax.rsqrt(jnp.mean(x * x, axis=-1, keepdims=True) + EPS)


def _mod_kernel(c_ref, w_ref, b_ref, o_ref):
    c = c_ref[...]
    s = c * jax.nn.sigmoid(c)
    o_ref[...] = _bdot(s, w_ref[...]) + b_ref[...]


def _adaln(cc, w_ada, b_ada):
    rows, d = cc.shape
    n = w_ada.shape[1]
    tn = 1536
    return pl.pallas_call(
        _mod_kernel,
        grid=(n // tn,),
        in_specs=[pl.BlockSpec((rows, d), lambda j: (0, 0)),
                  pl.BlockSpec((d, tn), lambda j: (0, j)),
                  pl.BlockSpec((1, tn), lambda j: (0, j))],
        out_specs=pl.BlockSpec((rows, tn), lambda j: (0, j)),
        out_shape=jax.ShapeDtypeStruct((rows, n), F32),
        compiler_params=_cparams(1),
        name="adaln",
    )(cc, w_ada, b_ada.reshape(1, n))


def _proj_kernel(*refs, with_pos, tm, widths, q_scale_first):
    if with_pos:
        x_ref, xp_ref, xn_ref, rt_ref, ct_ref, cw_ref, cb_ref, mod_ref, g_ref, w_ref, wa_ref, ba_ref = refs[:12]
        outs = refs[12:]
    else:
        x_ref, mod_ref, g_ref, w_ref, wa_ref, ba_ref = refs[:6]
        outs = refs[6:]
    x = x_ref[0]
    if with_pos:
        i = pl.program_id(0)
        n_tiles = pl.num_programs(0)
        rows_per_tile = tm // GRID_W
        n_rows = rt_ref.shape[0]
        halo = xp_ref.shape[1]
        r0 = pl.multiple_of(i * rows_per_tile, rows_per_tile)
        rt = rt_ref[pl.ds(r0, rows_per_tile), :]
        half = D_MODEL // 2
        x3 = x.reshape(rows_per_tile, GRID_W, D_MODEL)
        x3 = jnp.concatenate([x3[:, :, :half] + rt[:, None, :], x3[:, :, half:] + ct_ref[...][None]], axis=-1)
        x = x3.reshape(tm, D_MODEL)
        outs[0][0] = x
        outs = outs[1:]
        rt_prev = rt_ref[pl.ds(jnp.maximum(r0 - 1, 0), 1), :]
        rt_next = rt_ref[pl.ds(jnp.minimum(r0 + rows_per_tile, n_rows - 1), 1), :]
        x_prev = xp_ref[0] + jnp.concatenate(
            [jnp.broadcast_to(rt_prev, (halo, half)), ct_ref[GRID_W - halo:GRID_W, :]], axis=-1)
        x_next = xn_ref[0] + jnp.concatenate([jnp.broadcast_to(rt_next, (halo, half)), ct_ref[0:halo, :]], axis=-1)
        x = jnp.concatenate([x, x_prev, x_next], axis=0)
    shift = mod_ref[0, :, 0:D_MODEL]
    scale = mod_ref[0, :, D_MODEL:2 * D_MODEL]
    hn = (_rms(x) * g_ref[...]) * (1.0 + scale) + shift
    p_all = _bdot(hn, w_ref[...])
    p = p_all[:tm]
    off = 0
    for n, (o_ref, w) in enumerate(zip(outs[:-2], widths)):
        blk = p[:, off:off + w]
        if q_scale_first and n == 0:
            blk = blk * (GLA_DK ** -0.5)
        if with_pos and n == len(widths) - 1:
            before = jnp.where(i == 0, 0.0, p_all[tm + halo - 1:tm + halo, off:off + w])
            after = jnp.where(i == n_tiles - 1, 0.0, p_all[tm + halo:tm + halo + 1, off:off + w])
            t = lax.broadcasted_iota(jnp.int32, blk.shape, 0)
            up = jnp.where(t == 0, before, pltpu.roll(blk, 1, 0))
            dn = jnp.where(t == tm - 1, after, pltpu.roll(blk, tm - 1, 0))
            blk = up * cw_ref[0:1, :] + blk * cw_ref[1:2, :] + dn * cw_ref[2:3, :] + cb_ref[...]
        o_ref[0] = blk.astype(o_ref.dtype)
        off += w
    a = p[:, off:off + LANES]
    z = _bdot(a, wa_ref[...]) + ba_ref[...]
    la = jax.nn.log_sigmoid(z) / GLA_GATE_NORM
    outs[-2][0] = la[:, :GLA_QK_W]
    outs[-1][0] = la[:, GLA_QK_W:]


def _project(x, modrows, mod_row0, norm_g, w_cols, wa, ba, tabs, widths, q_scale_first, tm, conv=None):
    B, L, D = x.shape
    with_pos = tabs is not None
    ncols = w_cols.shape[1]
    in_specs = [pl.BlockSpec((1, tm, D), lambda i, b: (b, i, 0))]
    args = [x]
    if with_pos:
        rt, ct = tabs
        cw, cb = conv
        halo = 8
        per = tm // halo
        in_specs += [pl.BlockSpec((1, halo, D), lambda i, b: (b, jnp.maximum(i * per - 1, 0), 0)),
                     pl.BlockSpec((1, halo, D), lambda i, b: (b, jnp.minimum((i + 1) * per, L // halo - 1), 0)),
                     pl.BlockSpec(rt.shape, lambda i, b: (0, 0)), pl.BlockSpec(ct.shape, lambda i, b: (0, 0)),
                     pl.BlockSpec(cw.shape, lambda i, b: (0, 0)), pl.BlockSpec(cb.shape, lambda i, b: (0, 0))]
        args += [x, x, rt, ct, cw, cb]
    if mod_row0 is None:
        mod_map = lambda i, b: (b, 0, 0)
    else:
        mod_map = lambda i, b: (mod_row0, 0, 0)
    in_specs += [pl.BlockSpec((1, 1, modrows.shape[-1]), mod_map),
                 pl.BlockSpec((1, D), lambda i, b: (0, 0)),
                 pl.BlockSpec((D, ncols), lambda i, b: (0, 0)),
                 pl.BlockSpec(wa.shape, lambda i, b: (0, 0)),
                 pl.BlockSpec(ba.shape, lambda i, b: (0, 0))]
    args += [modrows, norm_g.reshape(1, D), w_cols, wa, ba]
    out_w = ([D] if with_pos else []) + list(widths) + [GLA_QK_W, GLA_QK_W]
    out_specs = [pl.BlockSpec((1, tm, w), lambda i, b: (b, i, 0)) for w in out_w]
    out_shape = [jax.ShapeDtypeStruct((B, L, w), F32) for w in out_w]
    v_pos = (1 if with_pos else 0) + list(widths).index(GLA_V_W)
    out_shape[v_pos] = jax.ShapeDtypeStruct((B, L, GLA_V_W), BF16)
    return pl.pallas_call(
        functools.partial(_proj_kernel, with_pos=with_pos, tm=tm, widths=tuple(widths), q_scale_first=q_scale_first),
        grid=(L // tm, B),
        in_specs=in_specs, out_specs=out_specs, out_shape=out_shape,
        compiler_params=_cparams(2, VMEM_LIMIT_BIG),
        name="project_x" if with_pos else "project_ctx",
    )(*args)


def _gla_kernel(*refs, reverse, with_output, nb):
    if with_output:
        q_ref, k_ref, v_ref, la_ref, s0_ref, o_ref, sfin_ref, s_scr = refs
    else:
        k_ref, v_ref, la_ref, s0_ref, sfin_ref, s_scr = refs
    C = GLA_CHUNK

    @pl.when(pl.program_id(1) == 0)
    def _():
        s_scr[...] = s0_ref[0]

    H = GLA_HEADS
    nt = (((1,), (1,)), ((), ()))
    tn = (((0,), (0,)), ((), ()))
    row = lax.broadcasted_iota(jnp.int32, (C, GLA_QK_W), 0)
    sc_r = lax.broadcasted_iota(jnp.int32, (C, H * C), 0)
    sc_s = lax.broadcasted_iota(jnp.int32, (C, H * C), 1) % C
    keep = (sc_r <= sc_s) if reverse else (sc_r >= sc_s)
    def head_mask(shape, rdiv, cdiv):
        return (lax.broadcasted_iota(jnp.int32, shape, 0) // rdiv) == (lax.broadcasted_iota(jnp.int32, shape, 1) // cdiv)
    m_k = head_mask((H * C, GLA_QK_W), C, GLA_DK)
    m_v = head_mask((H * C, GLA_V_W), C, GLA_DV)
    m_s = head_mask((GLA_V_W, GLA_QK_W), GLA_DV, GLA_DK)

    def prefix_sum(x):
        s = 1
        while s < C:
            if reverse:
                x = x + jnp.where(row < C - s, pltpu.roll(x, C - s, 0), 0.0)
            else:
                x = x + jnp.where(row >= s, pltpu.roll(x, s, 0), 0.0)
            s *= 2
        return x

    order = range(nb - 1, -1, -1) if reverse else range(nb)
    for ci in order:
        sl = slice(ci * C, (ci + 1) * C)
        k = k_ref[0, sl, :]
        v = v_ref[0, sl, :]
        b = prefix_sum(la_ref[0, sl, :])
        last = 0 if reverse else C - 1
        b_tot = b[last:last + 1, :]
        kdec = (k * jnp.exp(b_tot - b)).astype(BF16)
        vb = v.astype(BF16)
        upd_t = jnp.where(m_s, lax.dot_general(vb, kdec, tn, preferred_element_type=F32), 0.0)
        st = s_scr[...]
        if with_output:
            q = q_ref[0, sl, :]
            mid = C // 2 - 1 if reverse else C // 2
            ref = b[mid:mid + 1, :]
            qs = (q * jnp.exp(b - ref)).astype(BF16)
            ks = (k * jnp.exp(ref - b)).astype(BF16)
            qb = (q * jnp.exp(b)).astype(BF16)
            o_inter = lax.dot_general(qb, st.astype(BF16), nt, preferred_element_type=F32)
            ks4 = jnp.where(m_k, jnp.concatenate([ks] * H, axis=0), jnp.zeros((), BF16))
            sc = lax.dot_general(qs, ks4, nt, preferred_element_type=F32)
            sc = jnp.where(keep, sc, 0.0).astype(BF16)
            v4 = jnp.where(m_v, jnp.concatenate([vb] * H, axis=0), jnp.zeros((), BF16))
            o_ref[0, sl, :] = o_inter + jnp.dot(sc, v4, preferred_element_type=F32)
        s_scr[...] = st * jnp.exp(b_tot) + upd_t
    sfin_ref[0] = s_scr[...]


def _gla_pass(q, k, v, la, s0, reverse, nb):
    B, L, _ = k.shape
    with_output = q is not None
    tm = nb * GLA_CHUNK
    nblk = L // tm
    if reverse:
        tmap = lambda b, i: (b, nblk - 1 - i, 0)
    else:
        tmap = lambda b, i: (b, i, 0)
    smap = lambda b, i: (b, 0, 0)
    in_specs, args = [], []
    if with_output:
        in_specs.append(pl.BlockSpec((1, tm, GLA_QK_W), tmap))
        args.append(q)
    in_specs += [pl.BlockSpec((1, tm, GLA_QK_W), tmap), pl.BlockSpec((1, tm, GLA_V_W), tmap),
                 pl.BlockSpec((1, tm, GLA_QK_W), tmap), pl.BlockSpec((1, GLA_V_W, GLA_QK_W), smap)]
    args += [k, v, la, s0]
    out_specs, out_shape = [], []
    if with_output:
        out_specs.append(pl.BlockSpec((1, tm, GLA_V_W), tmap))
        out_shape.append(jax.ShapeDtypeStruct((B, L, GLA_V_W), F32))
    out_specs.append(pl.BlockSpec((1, GLA_V_W, GLA_QK_W), smap))
    out_shape.append(jax.ShapeDtypeStruct((B, GLA_V_W, GLA_QK_W), F32))
    res = pl.pallas_call(
        functools.partial(_gla_kernel, reverse=reverse, with_output=with_output, nb=nb),
        grid=(B, nblk),
        in_specs=in_specs, out_specs=out_specs, out_shape=out_shape,
        scratch_shapes=[pltpu.VMEM((GLA_V_W, GLA_QK_W), F32)],
        compiler_params=_cparams(2),
        name=("gla_" + ("bwd" if reverse else "fwd") + ("" if with_output else "_state")),
    )(*args)
    return (res[0], res[1]) if with_output else (None, res[0])


def _filter_kernel(z_ref, w1_ref, b1_ref, w2_ref, b2_ref, w3_ref, fr_ref, dl_ref, o_ref, *, tm, L):
    hi = lax.Precision.HIGHEST
    fr = fr_ref[...]
    h1 = jnp.sin(fr * (jnp.dot(z_ref[...], w1_ref[...], precision=hi, preferred_element_type=F32) + b1_ref[...]))
    h2 = jnp.sin(fr * (jnp.dot(h1, w2_ref[...], precision=hi, preferred_element_type=F32) + b2_ref[...]))
    h = _bdot(h2, w3_ref[...])
    t = z_ref[:, 0:1]
    decay = jnp.exp(-t * dl_ref[...])
    o_ref[...] = (h * jnp.concatenate([decay] * (h.shape[1] // decay.shape[1]), axis=-1)).astype(o_ref.dtype)


def _hyena_filters(zfeat, w1p, b1, w2, b2, w3, freq, decay_rates):
    L = zfeat.shape[0]
    tm = 512
    n = w3.shape[1]
    full = lambda a: pl.BlockSpec(a.shape, lambda i: (0,) * a.ndim)
    return pl.pallas_call(
        functools.partial(_filter_kernel, tm=tm, L=L),
        grid=(L // tm,),
        in_specs=[pl.BlockSpec((tm, zfeat.shape[1]), lambda i: (i, 0)), full(w1p), full(b1), full(w2), full(b2),
                  full(w3), full(freq), full(decay_rates)],
        out_specs=pl.BlockSpec((tm, n), lambda i: (i, 0)),
        out_shape=jax.ShapeDtypeStruct((L, n), BF16),
        compiler_params=_cparams(1),
        name="hyena_filters",
    )(zfeat, w1p, b1, w2, b2, w3, freq, decay_rates)


def _dft_tables():
    T = CONV_T
    f = np.arange(T, dtype=np.float64)[:, None] + 0.5
    n = np.arange(T, dtype=np.float64)[None, :]
    th = 2.0 * np.pi * f * n / (2 * T)
    thn = 2.0 * np.pi * f * (n - T) / (2 * T)
    fwd = np.concatenate([np.cos(th), -np.sin(th)], axis=0)
    inv = np.concatenate([np.cos(th).T, -np.sin(th).T], axis=1) / T
    nz = np.ones((1, T)); nz[0, 0] = 0.0
    P = np.concatenate([np.cos(th), -np.sin(th)], axis=0)
    Pc = np.concatenate([np.cos(th), np.sin(th)], axis=0)
    N = np.concatenate([np.cos(thn), -np.sin(thn)], axis=0) * nz
    Nc = np.concatenate([np.cos(thn), np.sin(thn)], axis=0) * nz
    m_pos = np.concatenate([N, P], axis=1)
    m_zero = np.concatenate([Pc * nz, P], axis=1)
    m_neg = np.concatenate([Nc, Pc], axis=1)
    mats = np.stack([m_pos, m_zero, m_neg], axis=0)
    as_bf16 = lambda a: jnp.asarray(a, dtype=F32).astype(BF16)
    return as_bf16(fwd), as_bf16(inv), as_bf16(mats)


def _spec_kernel(m_ref, a_ref, b_ref, o_ref):
    T = CONV_T
    o_ref[0, 0] = _bdot(m_ref[0, :, :T], a_ref[...]) + _bdot(m_ref[0, :, T:], b_ref[...])


def _filter_spectra(h, mats):
    L = h.shape[0]
    T = CONV_T
    J = L // T
    W = HY_WIDTH
    ct = W
    nct = W // ct
    nlag = 2 * J - 1

    def sel(dl):
        d = dl - (J - 1)
        return jnp.where(d > 0, 0, jnp.where(d == 0, 1, 2))

    def a_map(o, dl, c):
        d = dl - (J - 1)
        row = jnp.where(d > 0, d - 1, jnp.where(d == 0, 0, -d - 1))
        dirn = jnp.where(d > 0, 0, 1)
        return (row, (o * 2 + dirn) * nct + c)

    def b_map(o, dl, c):
        d = dl - (J - 1)
        row = jnp.where(d >= 0, d, -d)
        dirn = jnp.where(d >= 0, 0, 1)
        return (row, (o * 2 + dirn) * nct + c)

    return pl.pallas_call(
        _spec_kernel,
        grid=(HY_ORDER, nlag, nct),
        in_specs=[pl.BlockSpec((1, 2 * T, 2 * T), lambda o, dl, c: (sel(dl), 0, 0)),
                  pl.BlockSpec((T, ct), a_map),
                  pl.BlockSpec((T, ct), b_map)],
        out_specs=pl.BlockSpec((1, 1, 2 * T, ct), lambda o, dl, c: (o, dl, 0, c)),
        out_shape=jax.ShapeDtypeStruct((HY_ORDER, nlag, 2 * T, W), F32),
        compiler_params=_cparams(3),
        name="filter_spectra",
    )(mats, h, h)


def _lconv_kernel(u_ref, gate_ref, bias_ref, fwd_ref, inv_ref, h_hbm, o_ref, u_scr, y_scr, h_buf, hsem, *,
                  J, ct, GI, order):
    T = CONV_T
    c = pl.program_id(0)
    g = pl.program_id(2)
    new_tile = (pl.program_id(1) == 0) & (g == 0)
    h_copy = pltpu.make_async_copy(h_hbm.at[order, :, :, pl.ds(pl.multiple_of(c * ct, ct), ct)], h_buf, hsem.at[0])

    @pl.when(new_tile)
    def _():
        h_copy.start()

    @pl.when(g == 0)
    def _():
        for j in range(J):
            u_scr[j] = _bdot(fwd_ref[...], u_ref[0, j * T:(j + 1) * T, :])

    @pl.when(new_tile)
    def _():
        h_copy.wait()

    def run_group(i0):
        def tile_body(rt, carry):
            r0 = pl.multiple_of(rt * 8, 8)
            for lt in range(ct // LANES):
                cs = slice(lt * LANES, (lt + 1) * LANES)
                ure = [u_scr[j, pl.ds(r0, 8), cs] for j in range(J)]
                uim = [u_scr[j, pl.ds(T + r0, 8), cs] for j in range(J)]
                for ii in range(GI):
                    re_terms, im_terms = [], []
                    for j in range(J):
                        lag = i0 + ii - j + (J - 1)
                        hre = h_buf[lag, pl.ds(r0, 8), cs]
                        him = h_buf[lag, pl.ds(T + r0, 8), cs]
                        re_terms.append(hre * ure[j] - him * uim[j])
                        im_terms.append(hre * uim[j] + him * ure[j])
                    y_scr[ii, pl.ds(r0, 8), cs] = _tree_sum(re_terms)
                    y_scr[ii, pl.ds(T + r0, 8), cs] = _tree_sum(im_terms)
            return carry

        lax.fori_loop(0, T // 8, tile_body, 0)
        for ii in range(GI):
            y = _bdot(inv_ref[...], y_scr[ii])
            ui = u_ref[0, (i0 + ii) * T:(i0 + ii + 1) * T, :]
            o_ref[0, ii * T:(ii + 1) * T, :] = gate_ref[0, ii * T:(ii + 1) * T, :] * (y + ui * bias_ref[...])

    for gg in range(J // GI):
        pl.when(g == gg)(functools.partial(run_group, gg * GI))


def _long_conv(u_arr, u_blk0, gate_arr, gate_blk0, bias_row, fwd, inv, hspec, order):
    B, L, _ = u_arr.shape
    T = CONV_T
    J = L // T
    ct = 256
    nct = HY_WIDTH // ct
    nlag = 2 * J - 1
    GI = 4
    return pl.pallas_call(
        functools.partial(_lconv_kernel, J=J, ct=ct, GI=GI, order=order),
        grid=(nct, B, J // GI),
        in_specs=[pl.BlockSpec((1, L, ct), lambda c, b, i: (b, 0, u_blk0 + c)),
                  pl.BlockSpec((1, GI * T, ct), lambda c, b, i: (b, i, gate_blk0 + c)),
                  pl.BlockSpec((1, ct), lambda c, b, i: (0, c)),
                  pl.BlockSpec(fwd.shape, lambda c, b, i: (0, 0)),
                  pl.BlockSpec(inv.shape, lambda c, b, i: (0, 0)),
                  pl.BlockSpec(memory_space=pl.ANY)],
        out_specs=pl.BlockSpec((1, GI * T, ct), lambda c, b, i: (b, i, c)),
        out_shape=jax.ShapeDtypeStruct((B, L, HY_WIDTH), F32),
        scratch_shapes=[pltpu.VMEM((J, 2 * T, ct), F32), pltpu.VMEM((GI, 2 * T, ct), F32),
                        pltpu.VMEM((nlag, 2 * T, ct), F32), pltpu.SemaphoreType.DMA((1,))],
        compiler_params=_cparams(3, VMEM_LIMIT_BIG),
        name="hyena_long_conv%d" % order,
    )(u_arr, gate_arr, bias_row, fwd, inv, hspec)


def _merge_kernel(of_ref, ob_ref, g_ref, hy_ref, x0_ref, mod_ref, gn_ref, wout_ref, fg_ref, wr_ref,
                  x1_ref, lg_ref):
    o = of_ref[0] + ob_ref[0]
    g = g_ref[0]
    parts = []
    for h in range(GLA_HEADS):
        oh = o[:, h * GLA_DV:(h + 1) * GLA_DV]
        parts.append(_rms(oh) * gn_ref[...])
    y_gla = jnp.concatenate(parts, axis=-1) * (g * jax.nn.sigmoid(g))
    ycat = jnp.concatenate([y_gla, hy_ref[0]], axis=-1)
    m = _bdot(ycat, wout_ref[...])
    gate = mod_ref[0, :, 2 * D_MODEL:3 * D_MODEL]
    x1 = x0_ref[0] + gate * m
    x1_ref[0] = x1
    shift = mod_ref[0, :, 3 * D_MODEL:4 * D_MODEL]
    scale = mod_ref[0, :, 4 * D_MODEL:5 * D_MODEL]
    hn = ((_rms(x1) * fg_ref[...]) * (1.0 + scale) + shift).astype(BF16)
    lg_ref[0] = lax.dot_general(wr_ref[...], hn, (((1,), (1,)), ((), ())), preferred_element_type=F32)


def _merge(o_f, o_b, g, y_hy, x0, modrows, gla_norm_g, w_out, norm_ffn_g, w_router_t):
    B, L, D = x0.shape
    tm = 1024
    tmap = lambda b, i: (b, i, 0)
    full2 = lambda a: pl.BlockSpec(a.shape, lambda b, i: (0, 0))
    return pl.pallas_call(
        _merge_kernel,
        grid=(B, L // tm),
        in_specs=[pl.BlockSpec((1, tm, GLA_V_W), tmap), pl.BlockSpec((1, tm, GLA_V_W), tmap),
                  pl.BlockSpec((1, tm, GLA_V_W), tmap), pl.BlockSpec((1, tm, HY_WIDTH), tmap),
                  pl.BlockSpec((1, tm, D), tmap),
                  pl.BlockSpec((1, 1, modrows.shape[-1]), lambda b, i: (b, 0, 0)),
                  full2(gla_norm_g), full2(w_out), full2(norm_ffn_g), full2(w_router_t)],
        out_specs=[pl.BlockSpec((1, tm, D), tmap), pl.BlockSpec((1, N_EXPERTS, tm), lambda b, i: (b, 0, i))],
        out_shape=[jax.ShapeDtypeStruct((B, L, D), F32), jax.ShapeDtypeStruct((B, N_EXPERTS, L), F32)],
        compiler_params=_cparams(2, VMEM_LIMIT_BIG),
        name="merge_router",
    )(o_f, o_b, g, y_hy, x0, modrows, gla_norm_g, w_out, norm_ffn_g, w_router_t)


def _topk_kernel(lg_ref, idx_ref, wt_ref, pos_scr, *, cap):
    E, L = lg_ref.shape[1], lg_ref.shape[2]
    lg = lg_ref[0]
    mx = jnp.max(lg, axis=0, keepdims=True)
    ex = jnp.exp(lg - mx)
    aff = ex / jnp.sum(ex, axis=0, keepdims=True)

    def count(mask):
        return jnp.sum(mask.astype(jnp.int32), axis=1, keepdims=True)

    def tbody(n, cur):
        cand = cur | lax.shift_left(jnp.int32(1), 30 - n)
        return jnp.where(count(aff >= lax.bitcast_convert_type(cand, F32)) >= cap, cand, cur)

    thr = lax.bitcast_convert_type(lax.fori_loop(0, 31, tbody, jnp.zeros((E, 1), jnp.int32)), F32)
    gt = aff > thr
    eq = aff == thr
    need = cap - count(gt)
    tok = lax.broadcasted_iota(jnp.int32, (E, L), 1)

    def ibody(n, cur):
        cand = cur + lax.shift_left(jnp.int32(1), 12 - n)
        ok = (cand <= L) & (count(eq & (tok < cand)) <= need)
        return jnp.where(ok, cand, cur)

    bound = lax.fori_loop(0, 13, ibody, jnp.zeros((E, 1), jnp.int32))
    sel = gt | (eq & (tok < bound))

    li = lax.broadcasted_iota(jnp.int32, (LANES, LANES), 0)
    lj = lax.broadcasted_iota(jnp.int32, (LANES, LANES), 1)
    ustrict = (li < lj).astype(BF16)
    off = jnp.zeros((E, 1), F32)
    for r in range(L // LANES):
        s_r = sel[:, r * LANES:(r + 1) * LANES]
        sf = s_r.astype(F32)
        pre = jnp.dot(sf.astype(BF16), ustrict, preferred_element_type=F32)
        pos_scr[:, r * LANES:(r + 1) * LANES] = jnp.where(s_r, pre + off, -1.0)
        off = off + jnp.sum(sf, axis=1, keepdims=True)

    slot = lax.broadcasted_iota(jnp.int32, (cap, L), 0).astype(F32)
    tokf = lax.broadcasted_iota(jnp.int32, (8, L), 1)
    row = lax.broadcasted_iota(jnp.int32, (8, L), 0)
    t_hi = (tokf // 64).astype(F32)
    t_lo = (tokf % 64).astype(F32)
    for e in range(E):
        a = aff[e:e + 1, :]
        a1 = a.astype(BF16).astype(F32)
        a2 = (a - a1).astype(BF16).astype(F32)
        a3 = a - a1 - a2
        vals = jnp.where(row == 0, t_hi, jnp.where(row == 1, t_lo, jnp.where(row == 2, a1, jnp.where(
            row == 3, a2, jnp.where(row == 4, a3, 0.0)))))
        onehot = jnp.where(pos_scr[e:e + 1, :] == slot, 1.0, 0.0).astype(BF16)
        res = lax.dot_general(vals.astype(BF16), onehot, (((1,), (1,)), ((), ())), preferred_element_type=F32)
        idx_ref[0, e:e + 1, :] = (res[0:1, :] * 64.0 + res[1:2, :]).astype(jnp.int32)
        wt_ref[0, e:e + 1, :] = res[2:3, :] + res[3:4, :] + res[4:5, :]


def _topk(logits_t, cap):
    B, E, L = logits_t.shape
    return pl.pallas_call(
        functools.partial(_topk_kernel, cap=cap),
        grid=(B,),
        in_specs=[pl.BlockSpec((1, E, L), lambda b: (b, 0, 0))],
        out_specs=[pl.BlockSpec((1, E, cap), lambda b: (b, 0, 0)), pl.BlockSpec((1, E, cap), lambda b: (b, 0, 0))],
        out_shape=[jax.ShapeDtypeStruct((B, E, cap), jnp.int32), jax.ShapeDtypeStruct((B, E, cap), F32)],
        scratch_shapes=[pltpu.VMEM((E, L), F32)],
        compiler_params=_cparams(1),
        name="ec_topk",
    )(logits_t)


def _moe_kernel(idxc_ref, wtc_ref, idxn_ref, idxp_ref, wtp_ref, x1_hbm, mod_ref, ng_ref, wg_hbm, wu_hbm, wd_hbm,
                fg_ref, out_hbm, acc, tab, xsp, xs_scr, y_part, y_fin, wg_buf, wu_buf, wd_buf, sem, wsem, *,
                cap, n_exp):
    b = pl.program_id(0)
    e = pl.program_id(1)
    hf = pl.program_id(2)
    n_slots = wg_buf.shape[0]
    fc = wg_buf.shape[2]
    step = (b * n_exp + e) * 2 + hf
    n_steps = pl.num_programs(0) * n_exp * 2
    slot = step % n_slots

    def weight_copies(t):
        te, th, ts = (t // 2) % n_exp, t % 2, t % n_slots
        cols = pl.ds(th * fc if isinstance(th, int) else pl.multiple_of(th * fc, fc), fc)
        return (pltpu.make_async_copy(wg_hbm.at[te, :, cols], wg_buf.at[ts], wsem.at[ts, 0]),
                pltpu.make_async_copy(wu_hbm.at[te, :, cols], wu_buf.at[ts], wsem.at[ts, 1]),
                pltpu.make_async_copy(wd_hbm.at[te, cols, :], wd_buf.at[ts], wsem.at[ts, 2]))

    @pl.when(step == 0)
    def _():
        for t in range(n_slots - 1):
            for cp in weight_copies(t):
                cp.start()

    @pl.when(step + n_slots - 1 < n_steps)
    def _():
        for cp in weight_copies(step + n_slots - 1):
            cp.start()

    for cp in weight_copies(step):
        cp.wait()
    L = acc.shape[0]
    rows = MOE_ROWS
    n_ch = L // rows
    group = 8
    half = D_MODEL // 2

    def chunk(n):
        return pl.ds(n * rows if isinstance(n, int) else pl.multiple_of(n * rows, rows), rows)

    def in_copy(n):
        return pltpu.make_async_copy(x1_hbm.at[b, chunk(n)], acc.at[chunk(n)], sem.at[1 + n])

    def out_copy(n):
        return pltpu.make_async_copy(acc.at[chunk(n)], out_hbm.at[b, chunk(n)], sem.at[0])

    def gather_row(idx_ref, i):
        xsp[pl.ds(i, 1), :] = tab[pl.ds(idx_ref[0, 0, i], 1), :]

    def scatter_group(idx_ref, wt_ref, i0):
        ts = [idx_ref[0, 0, i0 + r] for r in range(group)]
        new = [acc[pl.ds(ts[r], 1), :] + y_fin[pl.ds(i0 + r, 1), :] * wt_ref[0, 0, i0 + r] for r in range(group)]
        for r in range(group):
            acc[pl.ds(ts[r], 1), :] = new[r]

    def ffn_part():
        xs = xs_scr[...]
        y = None
        for f0 in range(0, fc, fc // 2):
            fs = slice(f0, f0 + fc // 2)
            gte = _bdot(xs, wg_buf[slot, :, fs])
            up = _bdot(xs, wu_buf[slot, :, fs])
            part = _bdot(gte * jax.nn.sigmoid(gte) * up, wd_buf[slot, fs, :])
            y = part if y is None else y + part
        return y

    def unpack_rows():
        xp = xsp[...]
        lo = pltpu.unpack_elementwise(xp, index=0, packed_dtype=BF16, unpacked_dtype=F32)
        hi = pltpu.unpack_elementwise(xp, index=1, packed_dtype=BF16, unpacked_dtype=F32)
        xs_scr[...] = jnp.concatenate([lo, hi], axis=-1).astype(BF16)

    def scatter_prev():
        for i0 in range(0, cap, group):
            scatter_group(idxp_ref, wtp_ref, i0)

    def gather_next():
        for i in range(cap):
            gather_row(idxn_ref, i)

    mod_gate = lambda: mod_ref[0, :, 5 * D_MODEL:6 * D_MODEL]

    @pl.when((e == 0) & (hf == 0))
    def _():
        for n in range(n_ch):
            in_copy(n).start()
        shift = mod_ref[0, :, 3 * D_MODEL:4 * D_MODEL]
        scale = mod_ref[0, :, 4 * D_MODEL:5 * D_MODEL]

        def norm(n, carry):
            in_copy(n).wait()
            r0 = pl.multiple_of(n * rows, rows)
            hn = (_rms(acc[pl.ds(r0, rows), :]) * ng_ref[...]) * (1.0 + scale) + shift
            tab[pl.ds(r0, rows), :] = pltpu.pack_elementwise([hn[:, :half], hn[:, half:]], packed_dtype=BF16)
            return carry

        lax.fori_loop(0, n_ch, norm, 0)

        def gather0(gi, carry):
            i0 = pl.multiple_of(gi * group, group)
            for r in range(group):
                gather_row(idxc_ref, i0 + r)
            return carry

        lax.fori_loop(0, cap // group, gather0, 0)
        unpack_rows()
        y_part[...] = ffn_part()

    @pl.when((e > 0) & (hf == 0))
    def _():
        scatter_prev()
        y_part[...] = ffn_part()

    @pl.when(hf == 1)
    def _():
        gather_next()
        y_fin[...] = (y_part[...] + ffn_part()) * mod_gate()
        unpack_rows()

    @pl.when((e == n_exp - 1) & (hf == 1))
    def _():
        def scatter_last(gi, carry):
            scatter_group(idxc_ref, wtc_ref, pl.multiple_of(gi * group, group))
            return carry

        lax.fori_loop(0, cap // group, scatter_last, 0)

        def fin(n, carry):
            r0 = pl.multiple_of(n * rows, rows)
            acc[pl.ds(r0, rows), :] = _rms(acc[pl.ds(r0, rows), :]) * fg_ref[...]
            out_copy(n).start()
            return carry

        lax.fori_loop(0, n_ch, fin, 0)
        for n in range(n_ch):
            out_copy(n).wait()


def _moe(idx, wts, x1, modrows, norm_ffn_g, wg, wu, wd, norm_final_g):
    B, L, D = x1.shape
    E, cap = idx.shape[1], idx.shape[2]
    idx3 = idx.reshape(B * E, 1, cap)
    wts3 = wts.reshape(B * E, 1, cap)

    def smem_spec(shift):
        return pl.BlockSpec((1, 1, cap), lambda b, e, h: (b * E + jnp.clip(e + shift, 0, E - 1), 0, 0),
                            memory_space=pltpu.SMEM)

    n_half = 2
    fc = D_EXPERT // n_half
    n_slots = 3
    hbm = lambda: pl.BlockSpec(memory_space=pl.ANY)
    return pl.pallas_call(
        functools.partial(_moe_kernel, cap=cap, n_exp=E),
        grid=(B, E, n_half),
        in_specs=[smem_spec(0), smem_spec(0), smem_spec(1), smem_spec(-1), smem_spec(-1), hbm(),
                  pl.BlockSpec((1, 1, modrows.shape[-1]), lambda b, e, h: (b, 0, 0)),
                  pl.BlockSpec((1, D), lambda b, e, h: (0, 0)),
                  hbm(), hbm(), hbm(),
                  pl.BlockSpec((1, D), lambda b, e, h: (0, 0))],
        out_specs=hbm(),
        out_shape=jax.ShapeDtypeStruct((B, L, D), F32),
        scratch_shapes=[pltpu.VMEM((L, D), F32), pltpu.VMEM((L, D // 2), jnp.uint32),
                        pltpu.VMEM((cap, D // 2), jnp.uint32), pltpu.VMEM((cap, D), BF16),
                        pltpu.VMEM((cap, D), F32), pltpu.VMEM((cap, D), F32),
                        pltpu.VMEM((n_slots, D, fc), F32), pltpu.VMEM((n_slots, D, fc), F32),
                        pltpu.VMEM((n_slots, fc, D), F32),
                        pltpu.SemaphoreType.DMA((1 + L // MOE_ROWS,)), pltpu.SemaphoreType.DMA((n_slots, 3))],
        compiler_params=_cparams(3, VMEM_LIMIT_BIG),
        name="ec_moe",
    )(idx3, wts3, idx3, idx3, wts3, x1, modrows, norm_ffn_g.reshape(1, D), wg, wu, wd, norm_final_g.reshape(1, D))


def _pos_tables():
    quarter = D_MODEL // 4
    omega = 1.0 / (POS_BASE ** (np.arange(quarter, dtype=np.float64) / quarter))
    n = np.arange(GRID_W, dtype=np.float64)[:, None] * omega[None, :]
    tab = np.concatenate([np.sin(n), np.cos(n)], axis=-1).astype(np.float32)
    return tab


def _filter_features(L):
    t = np.linspace(0.0, 1.0, L, dtype=np.float32).astype(np.float64)[:, None]
    w = 2.0 * math.pi * np.arange(L, dtype=np.float64)[:, None] / L
    f = np.linspace(1e-4, HY_BANDS - 1, HY_BANDS, dtype=np.float32).astype(np.float64)[None, :]
    z = np.concatenate([t, np.cos(f * w), -np.sin(f * w)], axis=-1)
    zp = np.zeros((L, LANES), np.float32)
    zp[:, :HY_EMB] = z.astype(np.float32)
    return zp


def kernel(x, c, ctx, c_ctx, w_ada, b_ada, norm_mix_g, w_in, gla_wa_f, gla_ba_f, gla_wa_b, gla_ba_b, gla_norm_g,
           hy_conv_w, hy_conv_b, hy_w1, hy_b1, hy_w2, hy_b2, hy_w3, hy_freq, hy_bias, w_out, norm_ffn_g, w_router,
           w_gate, w_up, w_down, norm_final_g):
    B, L, D = x.shape
    assert w_ada.shape[0] == 1 and D == D_MODEL and L % (GRID_W * 8) == 0
    l = 0
    cap = EC_FACTOR * L // N_EXPERTS

    cc = jnp.zeros((8, D), F32).at[:B].set(c).at[B].set(c_ctx)
    modrows = _adaln(cc, w_ada[l], b_ada[l]).reshape(8, 1, N_MOD * D)

    s = (GLA_QK_W, 2 * GLA_QK_W, 2 * GLA_QK_W + GLA_V_W, 2 * GLA_QK_W + 2 * GLA_V_W,
         2 * GLA_QK_W + 2 * GLA_V_W + GLA_RANK, 2 * GLA_QK_W + 2 * GLA_V_W + 2 * GLA_RANK)
    w = w_in[l]
    w_q, w_k, w_v, w_g, w_af, w_ab, w_hy = (w[:, :s[0]], w[:, s[0]:s[1]], w[:, s[1]:s[2]], w[:, s[2]:s[3]],
                                            w[:, s[3]:s[4]], w[:, s[4]:s[5]], w[:, s[5]:])
    w_a = jnp.concatenate([w_af, w_ab, jnp.zeros((D, LANES - 2 * GLA_RANK), F32)], axis=1)
    w_x = jnp.concatenate([w_q, w_k, w_v, w_g, w_hy, w_a], axis=1).astype(BF16)
    w_c = jnp.concatenate([w_k, w_v, w_a], axis=1).astype(BF16)
    wa = jnp.zeros((LANES, 2 * GLA_QK_W), F32)
    wa = wa.at[:GLA_RANK, :GLA_QK_W].set(gla_wa_f[l]).at[GLA_RANK:2 * GLA_RANK, GLA_QK_W:].set(gla_wa_b[l])
    wa = wa.astype(BF16)
    ba = jnp.concatenate([gla_ba_f[l], gla_ba_b[l]]).reshape(1, 2 * GLA_QK_W)

    tab = jnp.asarray(_pos_tables())
    x0, q, k, v, g, u, la_f, la_b = _project(
        x, modrows, None, norm_mix_g[l], w_x, wa, ba, (tab, tab),
        (GLA_QK_W, GLA_QK_W, GLA_V_W, GLA_V_W, 3 * HY_WIDTH), True, 1024,
        conv=(hy_conv_w[l], hy_conv_b[l].reshape(1, -1)))
    k_c, v_c, laf_c, lab_c = _project(ctx, modrows, B, norm_mix_g[l], w_c, wa, ba, None,
                                      (GLA_QK_W, GLA_V_W), False, ctx.shape[1])

    s_zero = jnp.zeros((B, GLA_V_W, GLA_QK_W), F32)
    nb_c = ctx.shape[1] // GLA_CHUNK
    _, s_f = _gla_pass(None, k_c, v_c, laf_c, s_zero, False, nb_c)
    _, s_b = _gla_pass(None, k_c, v_c, lab_c, s_zero, True, nb_c)
    o_f, _ = _gla_pass(q, k, v, la_f, s_f, False, 16)
    o_b, _ = _gla_pass(q, k, v, la_b, s_b, True, 16)

    w1p = jnp.zeros((LANES, HY_HIDDEN), F32).at[:HY_EMB].set(hy_w1[l])
    deltas = np.abs(np.linspace(HY_MIN_DECAY, HY_MAX_DECAY, HY_WIDTH, dtype=np.float32))
    decay_rates = jnp.asarray(deltas.reshape(1, -1))
    h = _hyena_filters(jnp.asarray(_filter_features(L)), w1p, hy_b1[l].reshape(1, -1), hy_w2[l],
                       hy_b2[l].reshape(1, -1), hy_w3[l], hy_freq[l].reshape(1, -1), decay_rates)
    fwd, inv, mats = _dft_tables()
    hspec = _filter_spectra(h, mats)

    nct = HY_WIDTH // 256
    z1 = _long_conv(u, 0, u, nct, hy_bias[l][0:1], fwd, inv, hspec, 0)
    y_hy = _long_conv(z1, 0, u, 2 * nct, hy_bias[l][1:2], fwd, inv, hspec, 1)

    x1, logits_t = _merge(o_f, o_b, g, y_hy, x0, modrows, gla_norm_g[l].reshape(1, GLA_DV),
                                   w_out[l].astype(BF16), norm_ffn_g[l].reshape(1, D),
                                   w_router[l].T.astype(BF16))
    idx, wts = _topk(logits_t, cap)
    return _moe(idx, wts, x1, modrows, norm_ffn_g[l], w_gate[l], w_up[l], w_down[l], norm_final_g)
```
